```python
import math
import jax, jax.numpy as jnp
from jax import lax
import numpy as np

D_MODEL = 1024
BATCH = 32
SEQ = 256
DEPTH = 1
DEC_BATCH = 4
DEC_SEQ = 1024
PAST_LEN = 256

GRID_W = 64
N_HEADS = 8
N_KV_HEADS = 2
GROUP = N_HEADS // N_KV_HEADS
HEAD_DIM = 64
ATTN_WIDTH = N_HEADS * HEAD_DIM
KV_WIDTH = N_KV_HEADS * HEAD_DIM
WINDOW = 128
BLOCK = 128
ROPE_THETA = 10000.0
AXIS_DIM = HEAD_DIM // 2
AXIS_FREQS = AXIS_DIM // 2
CONV_CH = D_MODEL - ATTN_WIDTH
CONV_K = 31
CONV_PAD = CONV_K // 2
IN_WIDTH = ATTN_WIDTH + 2 * KV_WIDTH + 2 * CONV_CH
MIX_WIDTH = ATTN_WIDTH + CONV_CH
N_KEYS = 128
N_EXPERTS = N_KEYS * N_KEYS
PEER_HEADS = 8
PEER_QDIM = 256
PEER_HALF = PEER_QDIM // 2
PEER_TOPK = 16
PEER_TOKEN_BLOCK = 128
EPS = 1e-6

kernel_name = "hybrid_dit_swa_conformer_peer_step"


def rms_norm(x, g):
    xf = x.astype(jnp.float32)
    y = xf * lax.rsqrt(jnp.mean(xf * xf, axis=-1, keepdims=True) + EPS)
    return (y * g.astype(jnp.float32)).astype(x.dtype)


def layer_norm(x, g, b):
    xf = x.astype(jnp.float32)
    mu = jnp.mean(xf, axis=-1, keepdims=True)
    var = jnp.mean(jnp.square(xf - mu), axis=-1, keepdims=True)
    y = (xf - mu) * lax.rsqrt(var + EPS)
    return (y * g.astype(jnp.float32) + b.astype(jnp.float32)).astype(x.dtype)


def axial_rope_tables(seq_len):
    rows = seq_len // GRID_W
    row = jnp.repeat(jnp.arange(rows, dtype=jnp.float32), GRID_W)
    col = jnp.tile(jnp.arange(GRID_W, dtype=jnp.float32), rows)
    inv = ROPE_THETA ** (-jnp.arange(AXIS_FREQS, dtype=jnp.float32) / AXIS_FREQS)
    ang = jnp.concatenate([row[:, None] * inv, col[:, None] * inv], axis=-1)
    return jnp.cos(ang), jnp.sin(ang)


def apply_axial_rope(x, cos, sin):
    xf = x.astype(jnp.float32)
    c = cos[None, :, None, :]
    s = sin[None, :, None, :]

    def rot(t, cc, ss):
        t1, t2 = t[..., :AXIS_FREQS], t[..., AXIS_FREQS:]
        return jnp.concatenate([t1 * cc - t2 * ss, t2 * cc + t1 * ss], axis=-1)

    out = jnp.concatenate([
        rot(xf[..., :AXIS_DIM], c[..., :AXIS_FREQS], s[..., :AXIS_FREQS]),
        rot(xf[..., AXIS_DIM:], c[..., AXIS_FREQS:], s[..., AXIS_FREQS:])], axis=-1)
    return out.astype(x.dtype)


def modulation(cond, w_ada, b_ada):
    m = jax.nn.silu(cond) @ w_ada + b_ada
    return jnp.split(m, 6, axis=-1)


def project(h, w_in, q_norm, k_norm):
    B, S, _ = h.shape
    z = h @ w_in
    q = z[..., :ATTN_WIDTH].reshape(B, S, N_HEADS, HEAD_DIM)
    k = z[..., ATTN_WIDTH:ATTN_WIDTH + KV_WIDTH].reshape(B, S, N_KV_HEADS, HEAD_DIM)
    v = z[..., ATTN_WIDTH + KV_WIDTH:ATTN_WIDTH + 2 * KV_WIDTH].reshape(B, S, N_KV_HEADS, HEAD_DIM)
    u = z[..., ATTN_WIDTH + 2 * KV_WIDTH:]
    return rms_norm(q, q_norm), rms_norm(k, k_norm), v, u


def context_attention(q, k, v, sink):
    B, C = q.shape[:2]
    scale = 1.0 / math.sqrt(HEAD_DIM)
    qg = q.reshape(B, C, N_KV_HEADS, GROUP, HEAD_DIM)
    s = jnp.einsum('bqkgd,bckd->bkgqc', qg, k).astype(jnp.float32) * scale
    s_sink = jnp.broadcast_to(sink.astype(jnp.float32).reshape(1, N_KV_HEADS, GROUP, 1, 1), (B, N_KV_HEADS, GROUP, C, 1))
    p = jax.nn.softmax(jnp.concatenate([s_sink, s], axis=-1), axis=-1)[..., 1:].astype(v.dtype)
    o = jnp.einsum('bkgqc,bckd->bqkgd', p, v)
    return o.reshape(B, C, ATTN_WIDTH)


def latent_attention(q, k, v, k_ctx, v_ctx, sink):
    B, S = q.shape[:2]
    nb = S // BLOCK
    scale = 1.0 / math.sqrt(HEAD_DIM)
    qb = q.reshape(B, nb, BLOCK, N_KV_HEADS, GROUP, HEAD_DIM)
    pad = ((0, 0), (BLOCK, BLOCK), (0, 0), (0, 0))
    kp = jnp.pad(k, pad).reshape(B, nb + 2, BLOCK, N_KV_HEADS, HEAD_DIM)
    vp = jnp.pad(v, pad).reshape(B, nb + 2, BLOCK, N_KV_HEADS, HEAD_DIM)
    kw = jnp.concatenate([kp[:, :-2], kp[:, 1:-1], kp[:, 2:]], axis=2)
    vw = jnp.concatenate([vp[:, :-2], vp[:, 1:-1], vp[:, 2:]], axis=2)
    s_win = jnp.einsum('bnqkgd,bnjkd->bnkgqj', qb, kw).astype(jnp.float32) * scale
    a = jnp.arange(BLOCK)
    j = jnp.arange(3 * BLOCK)
    rel = j[None, :] - BLOCK - a[:, None]
    kpos = (jnp.arange(nb)[:, None] - 1) * BLOCK + j[None, :]
    valid = (jnp.abs(rel) <= WINDOW)[None] & ((kpos >= 0) & (kpos < S))[:, None, :]
    s_win = jnp.where(valid[None, :, None, None], s_win, -jnp.inf)
    s_ctx = jnp.einsum('bnqkgd,bckd->bnkgqc', qb, k_ctx).astype(jnp.float32) * scale
    s_sink = jnp.broadcast_to(sink.astype(jnp.float32).reshape(1, 1, N_KV_HEADS, GROUP, 1, 1),
                              (B, nb, N_KV_HEADS, GROUP, BLOCK, 1))
    p = jax.nn.softmax(jnp.concatenate([s_sink, s_win, s_ctx], axis=-1), axis=-1)
    p_win = p[..., 1:1 + 3 * BLOCK].astype(v.dtype)
    p_ctx = p[..., 1 + 3 * BLOCK:].astype(v.dtype)
    o = (jnp.einsum('bnkgqj,bnjkd->bnqkgd', p_win, vw)
         + jnp.einsum('bnkgqc,bckd->bnqkgd', p_ctx, v_ctx))
    return o.reshape(B, S, ATTN_WIDTH)


def conformer_conv(u, w_dw, b_dw, ln_g, ln_b, w_pw):
    x = u[..., :CONV_CH] * jax.nn.sigmoid(u[..., CONV_CH:])
    y = lax.conv_general_dilated(x, w_dw[:, None, :].astype(x.dtype), window_strides=(1,),
                                 padding=((CONV_PAD, CONV_PAD),),
                                 dimension_numbers=('NWC', 'WIO', 'NWC'),
                                 feature_group_count=CONV_CH) + b_dw
    y = jax.nn.silu(layer_norm(y, ln_g, ln_b))
    return y @ w_pw


def peer(h, w_q, sub_keys, u_tab, v_tab):
    B, S, D = h.shape
    T = B * S
    hf = h.reshape(T, D)
    q = (hf @ w_q).reshape(T, PEER_HEADS, 2, PEER_HALF)
    s1 = jnp.einsum('thd,hkd->thk', q[:, :, 0], sub_keys[:, 0])
    s2 = jnp.einsum('thd,hkd->thk', q[:, :, 1], sub_keys[:, 1])
    v1, i1 = lax.top_k(s1, PEER_TOPK)
    v2, i2 = lax.top_k(s2, PEER_TOPK)
    cand = (v1[..., :, None] + v2[..., None, :]).reshape(T, PEER_HEADS, PEER_TOPK * PEER_TOPK)
    sc, ci = lax.top_k(cand, PEER_TOPK)
    e = (jnp.take_along_axis(i1, ci // PEER_TOPK, axis=-1) * N_KEYS
         + jnp.take_along_axis(i2, ci % PEER_TOPK, axis=-1))
    g = jax.nn.softmax(sc.astype(jnp.float32), axis=-1).astype(h.dtype)
    nblk = T // PEER_TOKEN_BLOCK
    eb = e.reshape(nblk, PEER_TOKEN_BLOCK, PEER_HEADS * PEER_TOPK)
    gb = g.reshape(nblk, PEER_TOKEN_BLOCK, PEER_HEADS * PEER_TOPK)
    xb = hf.reshape(nblk, PEER_TOKEN_BLOCK, D)

    def block(args):
        xt, et, gt = args
        act = jnp.einsum('td,tkd->tk', xt, u_tab[et])
        return jnp.einsum('tk,tkd->td', jax.nn.gelu(act, approximate=False) * gt, v_tab[et])

    out = lax.map(block, (xb, eb, gb))
    return out.reshape(B, S, D)


def trunk_layer(x, cond, norm1, norm2, w_ada, b_ada, w_in, w_out, q_norm, k_norm, attn_sink,
                conv_dw, conv_dw_b, conv_ln_g, conv_ln_b, conv_pw, peer_wq, peer_keys, peer_u, peer_v,
                ctx_kv=None):
    sh1, sc1, g1, sh2, sc2, g2 = modulation(cond, w_ada, b_ada)
    h = rms_norm(x, norm1) * (1.0 + sc1) + sh1
    q, k, v, u = project(h, w_in, q_norm, k_norm)
    if ctx_kv is None:
        attn = context_attention(q, k, v, attn_sink)
        kv_out = (k, v)
    else:
        cos, sin = axial_rope_tables(x.shape[1])
        q = apply_axial_rope(q, cos, sin)
        k = apply_axial_rope(k, cos, sin)
        attn = latent_attention(q, k, v, ctx_kv[0], ctx_kv[1], attn_sink)
        kv_out = None
    conv = conformer_conv(u, conv_dw, conv_dw_b, conv_ln_g, conv_ln_b, conv_pw)
    x = x + g1 * (jnp.concatenate([attn, conv], axis=-1) @ w_out)
    h2 = rms_norm(x, norm2) * (1.0 + sc2) + sh2
    x = x + g2 * peer(h2, peer_wq, peer_keys, peer_u, peer_v)
    return x, kv_out


def setup_inputs(seed: int = 0) -> dict:
    key = jax.random.key(seed)
    ks = jax.random.split(key, 24)
    f = jnp.float32
    n = lambda k, shape, s: jax.random.normal(k, shape, f) * s
    return {
        "x_prompt": n(ks[0], (BATCH, SEQ, D_MODEL), 1.0),
        "x_sample": n(ks[1], (DEC_BATCH, DEC_SEQ, D_MODEL), 1.0),
        "cache_k": n(ks[2], (DEC_BATCH, DEPTH, PAST_LEN, N_KV_HEADS, HEAD_DIM), 1.0),
        "cache_v": n(ks[3], (DEC_BATCH, DEPTH, PAST_LEN, N_KV_HEADS, HEAD_DIM), 1.0),
        "c": n(ks[4], (DEC_BATCH, D_MODEL), 1.0),
        "c_ctx": n(ks[5], (D_MODEL,), 1.0),
        "norm1": 1.0 + n(ks[6], (DEPTH, D_MODEL), 0.01),
        "norm2": 1.0 + n(ks[7], (DEPTH, D_MODEL), 0.01),
        "w_ada": n(ks[8], (DEPTH, D_MODEL, 6 * D_MODEL), 0.5 * D_MODEL ** -0.5),
        "b_ada": n(ks[9], (DEPTH, 6 * D_MODEL), 0.02),
        "w_in": n(ks[10], (DEPTH, D_MODEL, IN_WIDTH), D_MODEL ** -0.5),
        "w_out": n(ks[11], (DEPTH, MIX_WIDTH, D_MODEL), MIX_WIDTH ** -0.5),
        "q_norm": 1.0 + n(ks[12], (DEPTH, HEAD_DIM), 0.01),
        "k_norm": 1.0 + n(ks[13], (DEPTH, HEAD_DIM), 0.01),
        "attn_sink": n(ks[14], (DEPTH, N_HEADS), 0.5),
        "conv_dw": n(ks[15], (DEPTH, CONV_K, CONV_CH), CONV_K ** -0.5),
        "conv_dw_b": n(ks[16], (DEPTH, CONV_CH), 0.02),
        "conv_ln_g": 1.0 + n(ks[17], (DEPTH, CONV_CH), 0.01),
        "conv_ln_b": n(ks[18], (DEPTH, CONV_CH), 0.02),
        "conv_pw": n(ks[19], (DEPTH, CONV_CH, CONV_CH), CONV_CH ** -0.5),
        "peer_wq": n(ks[20], (DEPTH, D_MODEL, PEER_HEADS * PEER_QDIM), D_MODEL ** -0.5),
        "peer_keys": n(ks[21], (DEPTH, PEER_HEADS, 2, N_KEYS, PEER_HALF), PEER_HALF ** -0.5),
        "peer_u": n(ks[22], (DEPTH, N_EXPERTS, D_MODEL), D_MODEL ** -0.5),
        "peer_v": n(ks[23], (DEPTH, N_EXPERTS, D_MODEL), 1.0),
    }


def reference(x_prompt, x_sample, cache_k, cache_v, c, c_ctx, norm1, norm2, w_ada, b_ada, w_in, w_out,
              q_norm, k_norm, attn_sink, conv_dw, conv_dw_b, conv_ln_g, conv_ln_b, conv_pw,
              peer_wq, peer_keys, peer_u, peer_v):
    cond_ctx = c_ctx.reshape(1, 1, D_MODEL)
    cond_lat = c[:, None, :]
    xp = x_prompt
    xs = x_sample
    new_ks = []
    new_vs = []
    for l in range(DEPTH):
        w = (norm1[l], norm2[l], w_ada[l], b_ada[l], w_in[l], w_out[l], q_norm[l], k_norm[l], attn_sink[l],
             conv_dw[l], conv_dw_b[l], conv_ln_g[l], conv_ln_b[l], conv_pw[l],
             peer_wq[l], peer_keys[l], peer_u[l], peer_v[l])
        xp, (k_c, v_c) = trunk_layer(xp, cond_ctx, *w)
        new_ks.append(k_c)
        new_vs.append(v_c)
        xs, _ = trunk_layer(xs, cond_lat, *w, ctx_kv=(cache_k[:, l], cache_v[:, l]))
    new_k = jnp.stack(new_ks, axis=1)
    new_v = jnp.stack(new_vs, axis=1)
    return (xp, xs, new_k, new_v)
```

```python
import functools
import math

import jax
import jax.numpy as jnp
from jax import lax
from jax.experimental import pallas as pl
from jax.experimental.pallas import tpu as pltpu

F32 = jnp.float32
BF16 = jnp.bfloat16

D_MODEL = 1024
GRID_W = 64
N_HEADS = 8
N_KV_HEADS = 2
GROUP = N_HEADS // N_KV_HEADS
HEAD_DIM = 64
ATTN_WIDTH = N_HEADS * HEAD_DIM
KV_WIDTH = N_KV_HEADS * HEAD_DIM
WINDOW = 128
BLOCK = 128
ROPE_THETA = 10000.0
AXIS_DIM = HEAD_DIM // 2
AXIS_FREQS = AXIS_DIM // 2
CONV_CH = D_MODEL - ATTN_WIDTH
CONV_K = 31
CONV_PAD = CONV_K // 2
IN_WIDTH = ATTN_WIDTH + 2 * KV_WIDTH + 2 * CONV_CH
N_KEYS = 128
N_EXPERTS = N_KEYS * N_KEYS
PEER_HEADS = 8
PEER_QDIM = 256
PEER_HALF = PEER_QDIM // 2
PEER_TOPK = 16
EPS = 1e-6

LANES = 128
BF16_SUBLANES = 16
ROW_TILE = 256
HALO = 16
ROUTE_TILE = 256
PEER_TOKENS = 512
PEER_EXPERTS = 1024
VMEM_LIMIT = 48 * 1024 * 1024

NEG_INF = float("-inf")


def _params(*sem):
    return pltpu.CompilerParams(dimension_semantics=sem, vmem_limit_bytes=VMEM_LIMIT)


def _full(shape):
    return pl.BlockSpec(shape, lambda *_: (0,) * len(shape))


def _mod_kernel(cond_ref, w_ref, b_ref, o_ref):
    cnd = cond_ref[...]
    act = cnd * jax.nn.sigmoid(cnd)
    o_ref[...] = jnp.dot(act, w_ref[...], precision=lax.Precision.HIGHEST,
                         preferred_element_type=F32) + b_ref[...]


def _modulation(cond8, w_ada, b_ada):
    n = w_ada.shape[1]
    nb = n // D_MODEL
    return pl.pallas_call(
        _mod_kernel,
        grid=(nb,),
        in_specs=[_full((8, D_MODEL)),
                  pl.BlockSpec((D_MODEL, D_MODEL), lambda j: (0, j)),
                  pl.BlockSpec((1, D_MODEL), lambda j: (0, j))],
        out_specs=pl.BlockSpec((8, D_MODEL), lambda j: (0, j)),
        out_shape=jax.ShapeDtypeStruct((8, n), F32),
        compiler_params=_params("arbitrary"),
        name="modulation",
    )(cond8, w_ada, b_ada.reshape(1, n))


def _group_rms(t, gmat, gain):
    ms = jnp.dot((t * t).astype(BF16), gmat, preferred_element_type=F32)
    return t * lax.rsqrt(ms + EPS) * gain


def _rope(t, cos, sin_lo, sin_hi):
    w = t.shape[1]
    return (t * cos + pltpu.roll(t, w - AXIS_FREQS, 1) * sin_lo
            + pltpu.roll(t, AXIS_FREQS, 1) * sin_hi)


def _inproj_kernel(latent, x_ref, mod_ref, n1_ref, win_ref, qn_ref, kn_ref, gq_ref, gk_ref, *rest):
    if latent:
        cos_ref, slo_ref, shi_ref, q_ref, k_ref, v_ref, xg_ref = rest
    else:
        q_ref, k_ref, v_ref, xg_ref = rest
    x = x_ref[...]
    sh1 = mod_ref[0:1, :]
    sc1 = mod_ref[1:2, :]
    h = x * lax.rsqrt(jnp.mean(x * x, axis=-1, keepdims=True) + EPS) * n1_ref[...]
    h = h * (1.0 + sc1) + sh1
    z = jnp.dot(h.astype(BF16), win_ref[...], preferred_element_type=F32)
    q = _group_rms(z[:, :ATTN_WIDTH], gq_ref[...], qn_ref[...])
    k = _group_rms(z[:, ATTN_WIDTH:ATTN_WIDTH + KV_WIDTH], gk_ref[...], kn_ref[...])
    if latent:
        q = _rope(q, cos_ref[...], slo_ref[...], shi_ref[...])
        k = _rope(k, cos_ref[:, :KV_WIDTH], slo_ref[:, :KV_WIDTH], shi_ref[:, :KV_WIDTH])
    q_ref[...] = (q * (1.0 / math.sqrt(HEAD_DIM))).astype(BF16)
    k_ref[...] = k
    v_ref[...] = z[:, ATTN_WIDTH + KV_WIDTH:ATTN_WIDTH + 2 * KV_WIDTH]
    u = z[:, ATTN_WIDTH + 2 * KV_WIDTH:]
    xg_ref[...] = u[:, :CONV_CH] * jax.nn.sigmoid(u[:, CONV_CH:])


def _inproj(x2d, mod3, mod_row, tiles_per_seq, norm1, w_in, qn, kn, gq, gk, rope):
    t = x2d.shape[0]
    latent = rope is not None
    row = lambda i: (i, 0)
    in_specs = [pl.BlockSpec((ROW_TILE, D_MODEL), row),
                pl.BlockSpec((None, 6, D_MODEL), lambda i: (mod_row(i), 0, 0)),
                _full((1, D_MODEL)), _full((D_MODEL, IN_WIDTH)),
                _full((1, ATTN_WIDTH)), _full((1, KV_WIDTH)),
                _full((ATTN_WIDTH, ATTN_WIDTH)), _full((KV_WIDTH, KV_WIDTH))]
    args = [x2d, mod3, norm1, w_in, qn, kn, gq, gk]
    if latent:
        pos = lambda i: (i % tiles_per_seq, 0)
        in_specs += [pl.BlockSpec((ROW_TILE, ATTN_WIDTH), pos)] * 3
        args += list(rope)
    return pl.pallas_call(
        functools.partial(_inproj_kernel, latent),
        grid=(t // ROW_TILE,),
        in_specs=in_specs,
        out_specs=[pl.BlockSpec((ROW_TILE, ATTN_WIDTH), row),
                   pl.BlockSpec((ROW_TILE, KV_WIDTH), row),
                   pl.BlockSpec((ROW_TILE, KV_WIDTH), row),
                   pl.BlockSpec((ROW_TILE, CONV_CH), row)],
        out_shape=[jax.ShapeDtypeStruct((t, ATTN_WIDTH), BF16),
                   jax.ShapeDtypeStruct((t, KV_WIDTH), F32),
                   jax.ShapeDtypeStruct((t, KV_WIDTH), F32),
                   jax.ShapeDtypeStruct((t, CONV_CH), F32)],
        compiler_params=_params("arbitrary"),
        name="inproj_latent" if latent else "inproj_context",
    )(*args)


def _conv_kernel(tiles_per_seq, prev_ref, cur_ref, nxt_ref, dw_ref, db_ref, lg_ref, lb_ref, pw_ref,
                 o_ref, xs_ref):
    i = pl.program_id(0)
    pos = i % tiles_per_seq
    zero = jnp.zeros((HALO, CONV_CH), F32)
    xs_ref[0, 0:HALO, :] = jnp.where(pos > 0, prev_ref[...], zero)
    xs_ref[0, HALO:HALO + ROW_TILE, :] = cur_ref[...]
    xs_ref[0, HALO + ROW_TILE:2 * HALO + ROW_TILE, :] = jnp.where(pos < tiles_per_seq - 1, nxt_ref[...], zero)
    span = ROW_TILE + 2 * HALO - 8
    for r in range(1, 8):
        xs_ref[r, 0:span, :] = xs_ref[0, r:r + span, :]
    chunk = 32

    def body(c, carry):
        base = pl.multiple_of(c * chunk, chunk)
        acc = jnp.zeros((chunk, CONV_CH), F32) + db_ref[...]
        for k in range(CONV_K):
            kk = k + HALO - CONV_PAD
            start = pl.multiple_of(base + 8 * (kk // 8), 8)
            acc = acc + xs_ref[kk % 8, pl.ds(start, chunk), :] * dw_ref[k:k + 1, :]
        mu = jnp.mean(acc, axis=-1, keepdims=True)
        cen = acc - mu
        var = jnp.mean(cen * cen, axis=-1, keepdims=True)
        y = cen * lax.rsqrt(var + EPS) * lg_ref[...] + lb_ref[...]
        y = y * jax.nn.sigmoid(y)
        o_ref[pl.ds(base, chunk), :] = jnp.dot(y.astype(BF16), pw_ref[...],
                                               preferred_element_type=F32).astype(BF16)
        return carry

    lax.fori_loop(0, ROW_TILE // chunk, body, 0)


def _conv_module(xg, tiles_per_seq, dw, db, lg, lb, pw):
    t = xg.shape[0]
    nt = t // ROW_TILE
    per = ROW_TILE // HALO
    last = t // HALO - 1
    return pl.pallas_call(
        functools.partial(_conv_kernel, tiles_per_seq),
        grid=(nt,),
        in_specs=[pl.BlockSpec((HALO, CONV_CH), lambda i: (jnp.maximum(i * per - 1, 0), 0)),
                  pl.BlockSpec((ROW_TILE, CONV_CH), lambda i: (i, 0)),
                  pl.BlockSpec((HALO, CONV_CH), lambda i: (jnp.minimum((i + 1) * per, last), 0)),
                  _full((CONV_K, CONV_CH)), _full((1, CONV_CH)), _full((1, CONV_CH)), _full((1, CONV_CH)),
                  _full((CONV_CH, CONV_CH))],
        out_specs=pl.BlockSpec((ROW_TILE, CONV_CH), lambda i: (i, 0)),
        out_shape=jax.ShapeDtypeStruct((t, CONV_CH), BF16),
        scratch_shapes=[pltpu.VMEM((8, ROW_TILE + 2 * HALO, CONV_CH), F32)],
        compiler_params=_params("arbitrary"),
        name="conv_module",
    )(xg, xg, xg, dw, db, lg, lb, pw)


def _attend(qg, sink_col, parts):
    m = sink_col
    scores = []
    for keys, _, mask in parts:
        s = lax.dot_general(qg, keys, (((1,), (1,)), ((), ())), preferred_element_type=F32)
        if mask is not None:
            s = jnp.where(mask, s, NEG_INF)
        m = jnp.maximum(m, jnp.max(s, axis=1, keepdims=True))
        scores.append(s)
    denom = jnp.exp(sink_col - m)
    out = None
    for s, (_, vals, _) in zip(scores, parts):
        p = jnp.exp(s - m)
        denom = denom + jnp.sum(p, axis=1, keepdims=True)
        o = jnp.dot(p.astype(BF16), vals, preferred_element_type=F32)
        out = o if out is None else out + o
    return out / denom


def _mix_kernel(latent, seq, x_ref, mod_ref, q_ref, k_ref, v_ref, cv_ref, *rest):
    if latent:
        ck_ref, cvv_ref, sink_ref, wout_ref, n2_ref, x1_ref, h2_ref, attn_ref = rest
    else:
        sink_ref, wout_ref, n2_ref, x1_ref, h2_ref, attn_ref = rest
    width = min(seq, ROW_TILE + 2 * WINDOW)
    if latent:
        q0 = pl.program_id(1) * ROW_TILE
        start = pl.multiple_of(jnp.clip(q0 - WINDOW, 0, seq - width), BLOCK)
        qpos = q0 + (lax.broadcasted_iota(jnp.int32, (GROUP * ROW_TILE, width), 0) & (ROW_TILE - 1))
        kpos = start + lax.broadcasted_iota(jnp.int32, (GROUP * ROW_TILE, width), 1)
        mask = jnp.abs(kpos - qpos) <= WINDOW
    else:
        start = 0
        mask = None
    for kv in range(N_KV_HEADS):
        lanes = slice(kv * HEAD_DIM, (kv + 1) * HEAD_DIM)
        parts = [(k_ref[pl.ds(start, width), lanes].astype(BF16),
                  v_ref[pl.ds(start, width), lanes].astype(BF16), mask)]
        if latent:
            parts.append((ck_ref[:, lanes].astype(BF16), cvv_ref[:, lanes].astype(BF16), None))
        sink_col = jnp.concatenate(
            [jnp.full((ROW_TILE, 1), sink_ref[kv * GROUP + g], F32) for g in range(GROUP)], axis=0)
        qg = jnp.concatenate(
            [q_ref[:, (kv * GROUP + g) * HEAD_DIM:(kv * GROUP + g + 1) * HEAD_DIM] for g in range(GROUP)],
            axis=0)
        o = _attend(qg, sink_col, parts)
        for g in range(GROUP):
            h = kv * GROUP + g
            attn_ref[:, h * HEAD_DIM:(h + 1) * HEAD_DIM] = o[g * ROW_TILE:(g + 1) * ROW_TILE].astype(BF16)
    g1 = mod_ref[2:3, :]
    sh2 = mod_ref[3:4, :]
    sc2 = mod_ref[4:5, :]
    mixed = (jnp.dot(attn_ref[...], wout_ref[0:ATTN_WIDTH, :], preferred_element_type=F32)
             + jnp.dot(cv_ref[...], wout_ref[ATTN_WIDTH:, :], preferred_element_type=F32))
    x1 = x_ref[...] + g1 * mixed
    x1_ref[...] = x1
    h2 = x1 * lax.rsqrt(jnp.mean(x1 * x1, axis=-1, keepdims=True) + EPS) * n2_ref[...]
    h2_ref[...] = (h2 * (1.0 + sc2) + sh2).astype(BF16)


def _mix(x3, mod3, mod_row, q3, k3, v3, conv3, cache, sink, w_out, norm2):
    b, seq, _ = x3.shape
    latent = cache is not None
    tile = lambda w: pl.BlockSpec((None, ROW_TILE, w), lambda i, r: (i, r, 0))
    whole = lambda n, w: pl.BlockSpec((None, n, w), lambda i, r: (i, 0, 0))
    in_specs = [tile(D_MODEL),
                pl.BlockSpec((None, 6, D_MODEL), lambda i, r: (mod_row(i), 0, 0)),
                tile(ATTN_WIDTH), whole(seq, KV_WIDTH), whole(seq, KV_WIDTH), tile(CONV_CH)]
    args = [x3, mod3, q3, k3, v3, conv3]
    if latent:
        past = cache[0].shape[1]
        in_specs += [whole(past, KV_WIDTH)] * 2
        args += list(cache)
    in_specs += [pl.BlockSpec(memory_space=pltpu.SMEM), _full((D_MODEL, D_MODEL)), _full((1, D_MODEL))]
    args += [sink, w_out, norm2]
    return pl.pallas_call(
        functools.partial(_mix_kernel, latent, seq),
        grid=(b, seq // ROW_TILE),
        in_specs=in_specs,
        out_specs=[tile(D_MODEL), tile(D_MODEL)],
        out_shape=[jax.ShapeDtypeStruct((b, seq, D_MODEL), F32),
                   jax.ShapeDtypeStruct((b, seq, D_MODEL), BF16)],
        scratch_shapes=[pltpu.VMEM((ROW_TILE, ATTN_WIDTH), BF16)],
        compiler_params=_params("arbitrary", "arbitrary"),
        name="mix_latent" if latent else "mix_context",
    )(*args)


def _top_desc(x, count):
    slot = lax.broadcasted_iota(jnp.int32, (count, x.shape[1]), 0)
    out = jnp.zeros((count, x.shape[1]), F32)
    for k in range(count):
        m = jnp.max(x, axis=0, keepdims=True)
        out = jnp.where(slot == k, m, out)
        x = jnp.where(x == m, NEG_INF, x)
    return out


def _route_kernel(h2_ref, wq_ref, keys_ref, h2t_ref, rank_ref, p2_ref, cnt_ref, p1_ref, qp_ref):
    h2 = h2_ref[...]
    h2t_ref[...] = h2.astype(F32).T.astype(BF16)
    qp_ref[...] = jnp.dot(h2, wq_ref[...], preferred_element_type=F32).astype(BF16)
    nt = (((1,), (1,)), ((), ()))

    def head(h, carry):
        col = pl.multiple_of(h * PEER_QDIM, PEER_QDIM)
        s1 = lax.dot_general(keys_ref[h, 0], qp_ref[:, pl.ds(col, PEER_HALF)], nt,
                             preferred_element_type=F32)
        s2 = lax.dot_general(keys_ref[h, 1], qp_ref[:, pl.ds(col + PEER_HALF, PEER_HALF)], nt,
                             preferred_element_type=F32)
        v1 = _top_desc(s1, PEER_TOPK)
        v2 = _top_desc(s2, PEER_TOPK)
        slot = lax.broadcasted_iota(jnp.int32, (8, ROUTE_TILE), 0)
        cand = [v1[0:1] + v2]
        for j in range(1, 8):
            cand.append(jnp.where(slot < PEER_TOPK // (j + 1), v1[j:j + 1] + v2[0:8], NEG_INF))
        cand.append(v1[8:PEER_TOPK] + v2[0:1])
        best = _top_desc(jnp.concatenate(cand, axis=0), PEER_TOPK)
        tau = best[PEER_TOPK - 1:PEER_TOPK]
        z = jnp.sum(jnp.exp(best - best[0:1]), axis=0, keepdims=True)
        cnt = jnp.zeros_like(s1)
        rank = jnp.zeros_like(s2)
        for k in range(PEER_TOPK):
            vk = v2[k:k + 1]
            cnt = cnt + jnp.where(s1 + vk >= tau, 1.0, 0.0)
            rank = rank + jnp.where(vk > s2, 1.0, 0.0)
        steps = N_EXPERTS // PEER_EXPERTS
        cnt_ref[:, h] = cnt.reshape(steps, N_KEYS // steps, ROUTE_TILE)
        p1_ref[:, h] = jnp.exp(s1 - v1[0:1]).reshape(steps, N_KEYS // steps, ROUTE_TILE)
        rank_ref[h] = rank.astype(BF16).reshape(N_KEYS // BF16_SUBLANES, BF16_SUBLANES, ROUTE_TILE)
        p2_ref[h] = (jnp.exp(s2 - v2[0:1]) / z).astype(BF16).reshape(
            N_KEYS // BF16_SUBLANES, BF16_SUBLANES, ROUTE_TILE)
        return carry

    lax.fori_loop(0, PEER_HEADS, head, 0)


def _route(h2, wq, keys):
    t = h2.shape[0]
    grp = N_KEYS // BF16_SUBLANES
    tok4 = pl.BlockSpec((PEER_HEADS, grp, BF16_SUBLANES, ROUTE_TILE), lambda i: (0, 0, 0, i))
    steps = N_EXPERTS // PEER_EXPERTS
    tok3 = pl.BlockSpec((steps, PEER_HEADS, N_KEYS // steps, ROUTE_TILE), lambda i: (0, 0, 0, i))
    return pl.pallas_call(
        _route_kernel,
        grid=(t // ROUTE_TILE,),
        in_specs=[pl.BlockSpec((ROUTE_TILE, D_MODEL), lambda i: (i, 0)),
                  _full((D_MODEL, PEER_HEADS * PEER_QDIM)),
                  _full((PEER_HEADS, 2, N_KEYS, PEER_HALF))],
        out_specs=[pl.BlockSpec((D_MODEL, ROUTE_TILE), lambda i: (0, i)), tok4, tok4, tok3, tok3],
        out_shape=[jax.ShapeDtypeStruct((D_MODEL, t), BF16),
                   jax.ShapeDtypeStruct((PEER_HEADS, grp, BF16_SUBLANES, t), BF16),
                   jax.ShapeDtypeStruct((PEER_HEADS, grp, BF16_SUBLANES, t), BF16),
                   jax.ShapeDtypeStruct((steps, PEER_HEADS, N_KEYS // steps, t), F32),
                   jax.ShapeDtypeStruct((steps, PEER_HEADS, N_KEYS // steps, t), F32)],
        scratch_shapes=[pltpu.VMEM((ROUTE_TILE, PEER_HEADS * PEER_QDIM), BF16)],
        compiler_params=_params("arbitrary"),
        name="peer_route",
    )(h2, wq, keys)


def _expert_kernel(u_ref, vt_ref, h2t_ref, rank_ref, p2_ref, cnt_ref, p1_ref, x1_ref, mod_ref,
                   y_ref, acc_ref, act_ref, wt_ref):
    j = pl.program_id(1)
    rows_per_step = PEER_EXPERTS // N_KEYS
    grp = N_KEYS // BF16_SUBLANES

    @pl.when(j == 0)
    def _():
        acc_ref[...] = jnp.zeros_like(acc_ref)

    act_ref[...] = jnp.dot(u_ref[...], h2t_ref[...], preferred_element_type=F32)

    for c in range(rows_per_step):
        rows = slice(c * N_KEYS, (c + 1) * N_KEYS)
        for lt in range(PEER_TOKENS // LANES):
            lanes = slice(lt * LANES, (lt + 1) * LANES)
            gate = jnp.zeros((grp, BF16_SUBLANES, LANES), BF16)
            for h in range(PEER_HEADS):
                cnt = jnp.broadcast_to(cnt_ref[h, c:c + 1, lanes], (BF16_SUBLANES, LANES)).astype(BF16)
                p1 = jnp.broadcast_to(p1_ref[h, c:c + 1, lanes], (BF16_SUBLANES, LANES)).astype(BF16)
                sel = jnp.where(rank_ref[h, :, :, lanes] < cnt[None], p2_ref[h, :, :, lanes],
                                jnp.zeros((), BF16))
                gate = gate + sel * p1[None]
            a = act_ref[rows, lanes]
            gl = 0.5 * a * (1.0 + lax.erf(a * (1.0 / math.sqrt(2.0))))
            wt_ref[rows, lanes] = gl.astype(BF16) * gate.reshape(N_KEYS, LANES)
    acc_ref[...] += jnp.dot(vt_ref[...], wt_ref[...], preferred_element_type=F32)

    @pl.when(j == pl.num_programs(1) - 1)
    def _():
        g2 = mod_ref[5:6, :]
        y_ref[...] = x1_ref[...] + g2 * acc_ref[...].T


def _experts(u_bf, vt_bf, h2t, rank, p2, cnt, p1, x1, mod3, mod_row):
    t = x1.shape[0]
    grp = N_KEYS // BF16_SUBLANES
    tok4 = pl.BlockSpec((PEER_HEADS, grp, BF16_SUBLANES, PEER_TOKENS), lambda i, j: (0, 0, 0, i))
    tok3 = pl.BlockSpec((None, PEER_HEADS, PEER_EXPERTS // N_KEYS, PEER_TOKENS), lambda i, j: (j, 0, 0, i))
    return pl.pallas_call(
        _expert_kernel,
        grid=(t // PEER_TOKENS, N_EXPERTS // PEER_EXPERTS),
        in_specs=[pl.BlockSpec((PEER_EXPERTS, D_MODEL), lambda i, j: (j, 0)),
                  pl.BlockSpec((D_MODEL, PEER_EXPERTS), lambda i, j: (0, j)),
                  pl.BlockSpec((D_MODEL, PEER_TOKENS), lambda i, j: (0, i)),
                  tok4, tok4, tok3, tok3,
                  pl.BlockSpec((PEER_TOKENS, D_MODEL), lambda i, j: (i, 0)),
                  pl.BlockSpec((None, 6, D_MODEL), lambda i, j: (mod_row(i), 0, 0))],
        out_specs=pl.BlockSpec((PEER_TOKENS, D_MODEL), lambda i, j: (i, 0)),
        out_shape=jax.ShapeDtypeStruct((t, D_MODEL), F32),
        scratch_shapes=[pltpu.VMEM((D_MODEL, PEER_TOKENS), F32),
                        pltpu.VMEM((PEER_EXPERTS, PEER_TOKENS), F32),
                        pltpu.VMEM((PEER_EXPERTS, PEER_TOKENS), BF16)],
        compiler_params=_params("arbitrary", "arbitrary"),
        name="peer_experts",
    )(u_bf, vt_bf, h2t, rank, p2, cnt, p1, x1, mod3)


def _rope_tables(seq_len):
    rows = seq_len // GRID_W
    row = jnp.repeat(jnp.arange(rows, dtype=F32), GRID_W)
    col = jnp.tile(jnp.arange(GRID_W, dtype=F32), rows)
    inv = ROPE_THETA ** (-jnp.arange(AXIS_FREQS, dtype=F32) / AXIS_FREQS)
    ang_row = row[:, None] * inv
    ang_col = col[:, None] * inv
    zeros = jnp.zeros_like(ang_row)
    cos = jnp.concatenate([jnp.cos(ang_row)] * 2 + [jnp.cos(ang_col)] * 2, axis=-1)
    sin_lo = jnp.concatenate([-jnp.sin(ang_row), zeros, -jnp.sin(ang_col), zeros], axis=-1)
    sin_hi = jnp.concatenate([zeros, jnp.sin(ang_row), zeros, jnp.sin(ang_col)], axis=-1)
    return tuple(jnp.tile(t, (1, N_HEADS)) for t in (cos, sin_lo, sin_hi))


def _head_mean_matrix(width):
    idx = jnp.arange(width) // HEAD_DIM
    return jnp.where(idx[:, None] == idx[None, :], 1.0 / HEAD_DIM, 0.0).astype(BF16)


def _trunk(x3, mod3, mod_of_seq, cache, w, rope):
    b, seq, _ = x3.shape
    t = b * seq
    tiles_per_seq = seq // ROW_TILE
    q, k, v, xg = _inproj(x3.reshape(t, D_MODEL), mod3, lambda i: mod_of_seq(i // tiles_per_seq),
                          tiles_per_seq, w["norm1"], w["w_in"], w["qn"], w["kn"], w["gq"], w["gk"], rope)
    conv = _conv_module(xg, tiles_per_seq, w["conv_dw"], w["conv_dw_b"], w["conv_ln_g"], w["conv_ln_b"],
                        w["conv_pw"])
    x1, h2 = _mix(x3, mod3, mod_of_seq, q.reshape(b, seq, ATTN_WIDTH), k.reshape(b, seq, KV_WIDTH),
                  v.reshape(b, seq, KV_WIDTH), conv.reshape(b, seq, CONV_CH), cache, w["sink"],
                  w["w_out"], w["norm2"])
    h2t, rank, p2, cnt, p1 = _route(h2.reshape(t, D_MODEL), w["peer_wq"], w["peer_keys"])
    per_seq = seq // PEER_TOKENS
    if per_seq == 0:
        seqs_per_block = PEER_TOKENS // seq
        row_of_block = lambda i: mod_of_seq(i * seqs_per_block)
    else:
        row_of_block = lambda i: mod_of_seq(i // per_seq)
    y = _experts(w["peer_u"], w["peer_vt"], h2t, rank, p2, cnt, p1, x1.reshape(t, D_MODEL), mod3, row_of_block)
    return y.reshape(b, seq, D_MODEL), k, v


def kernel(x_prompt, x_sample, cache_k, cache_v, c, c_ctx, norm1, norm2, w_ada, b_ada, w_in, w_out,
           q_norm, k_norm, attn_sink, conv_dw, conv_dw_b, conv_ln_g, conv_ln_b, conv_pw,
           peer_wq, peer_keys, peer_u, peer_v):
    depth = norm1.shape[0]
    batch, seq, _ = x_prompt.shape
    dec_batch, dec_seq, _ = x_sample.shape
    past = cache_k.shape[2]
    assert dec_batch + 1 <= 8 and depth >= 1
    cond8 = jnp.concatenate([c_ctx[None, :], c, jnp.zeros((8 - 1 - dec_batch, D_MODEL), F32)], axis=0)
    rope = _rope_tables(dec_seq)
    xp, xs = x_prompt, x_sample
    new_ks, new_vs = [], []
    for l in range(depth):
        w = dict(
            norm1=norm1[l][None], norm2=norm2[l][None],
            w_in=w_in[l].astype(BF16), w_out=w_out[l].astype(BF16),
            qn=jnp.tile(q_norm[l], N_HEADS)[None], kn=jnp.tile(k_norm[l], N_KV_HEADS)[None],
            gq=_head_mean_matrix(ATTN_WIDTH), gk=_head_mean_matrix(KV_WIDTH),
            sink=attn_sink[l],
            conv_dw=conv_dw[l], conv_dw_b=conv_dw_b[l][None], conv_ln_g=conv_ln_g[l][None],
            conv_ln_b=conv_ln_b[l][None], conv_pw=conv_pw[l].astype(BF16),
            peer_wq=peer_wq[l].astype(BF16), peer_keys=peer_keys[l].astype(BF16),
            peer_u=peer_u[l].astype(BF16), peer_vt=peer_v[l].T.astype(BF16),
        )
        mod3 = _modulation(cond8, w_ada[l], b_ada[l]).reshape(8, 6, D_MODEL)
        xp, k_c, v_c = _trunk(xp, mod3, lambda s: 0, None, w, None)
        new_ks.append(k_c.reshape(batch, seq, N_KV_HEADS, HEAD_DIM))
        new_vs.append(v_c.reshape(batch, seq, N_KV_HEADS, HEAD_DIM))
        cache = (cache_k[:, l].reshape(dec_batch, past, KV_WIDTH), cache_v[:, l].reshape(dec_batch, past, KV_WIDTH))
        xs, _, _ = _trunk(xs, mod3, lambda s: s + 1, cache, w, rope)
    return (xp, xs, jnp.stack(new_ks, axis=1), jnp.stack(new_vs, axis=1))
```

```python
import functools
import math

import jax
import jax.numpy as jnp
from jax import lax
from jax.experimental import pallas as pl
from jax.experimental.pallas import tpu as pltpu

F32 = jnp.float32
BF16 = jnp.bfloat16

D_MODEL = 1024
GRID_W = 64
N_HEADS = 8
N_KV_HEADS = 2
GROUP = N_HEADS // N_KV_HEADS
HEAD_DIM = 64
ATTN_WIDTH = N_HEADS * HEAD_DIM
KV_WIDTH = N_KV_HEADS * HEAD_DIM
WINDOW = 128
BLOCK = 128
ROPE_THETA = 10000.0
AXIS_DIM = HEAD_DIM // 2
AXIS_FREQS = AXIS_DIM // 2
CONV_CH = D_MODEL - ATTN_WIDTH
CONV_K = 31
CONV_PAD = CONV_K // 2
IN_WIDTH = ATTN_WIDTH + 2 * KV_WIDTH + 2 * CONV_CH
N_KEYS = 128
N_EXPERTS = N_KEYS * N_KEYS
PEER_HEADS = 8
PEER_QDIM = 256
PEER_HALF = PEER_QDIM // 2
PEER_TOPK = 16
EPS = 1e-6

LANES = 128
BF16_SUBLANES = 16
ROW_TILE = 256
HALO = 16
ROUTE_TILE = 256
PEER_TOKENS = 512
PEER_EXPERTS = 1024
VMEM_LIMIT = 48 * 1024 * 1024

NEG_INF = float("-inf")


def _params(*sem, flags=None):
    return pltpu.CompilerParams(dimension_semantics=sem, vmem_limit_bytes=VMEM_LIMIT, flags=flags)


def _full(shape):
    return pl.BlockSpec(shape, lambda *_: (0,) * len(shape))


def _mod_kernel(cond_ref, w_ref, b_ref, o_ref):
    cnd = cond_ref[...]
    act = cnd * jax.nn.sigmoid(cnd)
    o_ref[...] = jnp.dot(act, w_ref[...], precision=lax.Precision.HIGHEST,
                         preferred_element_type=F32) + b_ref[...]


def _modulation(cond8, w_ada, b_ada):
    n = w_ada.shape[1]
    nb = n // D_MODEL
    return pl.pallas_call(
        _mod_kernel,
        grid=(nb,),
        in_specs=[_full((8, D_MODEL)),
                  pl.BlockSpec((D_MODEL, D_MODEL), lambda j: (0, j)),
                  pl.BlockSpec((1, D_MODEL), lambda j: (0, j))],
        out_specs=pl.BlockSpec((8, D_MODEL), lambda j: (0, j)),
        out_shape=jax.ShapeDtypeStruct((8, n), F32),
        compiler_params=_params("arbitrary"),
        name="modulation",
    )(cond8, w_ada, b_ada.reshape(1, n))


def _group_rms(t, gmat, gain):
    ms = jnp.dot((t * t).astype(BF16), gmat, preferred_element_type=F32)
    return t * lax.rsqrt(ms + EPS) * gain


def _rope(t, cos, sin_lo, sin_hi):
    w = t.shape[1]
    return (t * cos + pltpu.roll(t, w - AXIS_FREQS, 1) * sin_lo
            + pltpu.roll(t, AXIS_FREQS, 1) * sin_hi)


def _inproj_kernel(latent, x_ref, mod_ref, n1_ref, win_ref, qn_ref, kn_ref, gq_ref, gk_ref, *rest):
    if latent:
        cos_ref, slo_ref, shi_ref, q_ref, k_ref, v_ref, xg_ref = rest
    else:
        q_ref, k_ref, v_ref, xg_ref = rest
    x = x_ref[...]
    sh1 = mod_ref[0:1, :]
    sc1 = mod_ref[1:2, :]
    h = x * lax.rsqrt(jnp.mean(x * x, axis=-1, keepdims=True) + EPS) * n1_ref[...]
    h = h * (1.0 + sc1) + sh1
    z = jnp.dot(h.astype(BF16), win_ref[...], preferred_element_type=F32)
    q = _group_rms(z[:, :ATTN_WIDTH], gq_ref[...], qn_ref[...])
    k = _group_rms(z[:, ATTN_WIDTH:ATTN_WIDTH + KV_WIDTH], gk_ref[...], kn_ref[...])
    if latent:
        q = _rope(q, cos_ref[...], slo_ref[...], shi_ref[...])
        k = _rope(k, cos_ref[:, :KV_WIDTH], slo_ref[:, :KV_WIDTH], shi_ref[:, :KV_WIDTH])
    q_ref[...] = (q * (1.0 / math.sqrt(HEAD_DIM))).astype(BF16)
    k_ref[...] = k
    v_ref[...] = z[:, ATTN_WIDTH + KV_WIDTH:ATTN_WIDTH + 2 * KV_WIDTH]
    u = z[:, ATTN_WIDTH + 2 * KV_WIDTH:]
    xg_ref[...] = u[:, :CONV_CH] * jax.nn.sigmoid(u[:, CONV_CH:])


def _inproj(x2d, mod3, mod_row, tiles_per_seq, norm1, w_in, qn, kn, gq, gk, rope):
    t = x2d.shape[0]
    latent = rope is not None
    row = lambda i: (i, 0)
    in_specs = [pl.BlockSpec((ROW_TILE, D_MODEL), row),
                pl.BlockSpec((None, 6, D_MODEL), lambda i: (mod_row(i), 0, 0)),
                _full((1, D_MODEL)), _full((D_MODEL, IN_WIDTH)),
                _full((1, ATTN_WIDTH)), _full((1, KV_WIDTH)),
                _full((ATTN_WIDTH, ATTN_WIDTH)), _full((KV_WIDTH, KV_WIDTH))]
    args = [x2d, mod3, norm1, w_in, qn, kn, gq, gk]
    if latent:
        pos = lambda i: (i % tiles_per_seq, 0)
        in_specs += [pl.BlockSpec((ROW_TILE, ATTN_WIDTH), pos)] * 3
        args += list(rope)
    return pl.pallas_call(
        functools.partial(_inproj_kernel, latent),
        grid=(t // ROW_TILE,),
        in_specs=in_specs,
        out_specs=[pl.BlockSpec((ROW_TILE, ATTN_WIDTH), row),
                   pl.BlockSpec((ROW_TILE, KV_WIDTH), row),
                   pl.BlockSpec((ROW_TILE, KV_WIDTH), row),
                   pl.BlockSpec((ROW_TILE, CONV_CH), row)],
        out_shape=[jax.ShapeDtypeStruct((t, ATTN_WIDTH), BF16),
                   jax.ShapeDtypeStruct((t, KV_WIDTH), F32),
                   jax.ShapeDtypeStruct((t, KV_WIDTH), F32),
                   jax.ShapeDtypeStruct((t, CONV_CH), F32)],
        compiler_params=_params("arbitrary"),
        name="inproj_latent" if latent else "inproj_context",
    )(*args)


def _conv_kernel(tiles_per_seq, prev_ref, cur_ref, nxt_ref, dw_ref, db_ref, lg_ref, lb_ref, pw_ref,
                 o_ref, xs_ref):
    i = pl.program_id(0)
    pos = i % tiles_per_seq
    zero = jnp.zeros((HALO, CONV_CH), F32)
    xs_ref[0, 0:HALO, :] = jnp.where(pos > 0, prev_ref[...], zero)
    xs_ref[0, HALO:HALO + ROW_TILE, :] = cur_ref[...]
    xs_ref[0, HALO + ROW_TILE:2 * HALO + ROW_TILE, :] = jnp.where(pos < tiles_per_seq - 1, nxt_ref[...], zero)
    span = ROW_TILE + 2 * HALO - 8
    for r in range(1, 8):
        xs_ref[r, 0:span, :] = xs_ref[0, r:r + span, :]
    chunk = 32

    def body(c, carry):
        base = pl.multiple_of(c * chunk, chunk)
        acc = jnp.zeros((chunk, CONV_CH), F32) + db_ref[...]
        for k in range(CONV_K):
            kk = k + HALO - CONV_PAD
            start = pl.multiple_of(base + 8 * (kk // 8), 8)
            acc = acc + xs_ref[kk % 8, pl.ds(start, chunk), :] * dw_ref[k:k + 1, :]
        mu = jnp.mean(acc, axis=-1, keepdims=True)
        cen = acc - mu
        var = jnp.mean(cen * cen, axis=-1, keepdims=True)
        y = cen * lax.rsqrt(var + EPS) * lg_ref[...] + lb_ref[...]
        y = y * jax.nn.sigmoid(y)
        o_ref[pl.ds(base, chunk), :] = jnp.dot(y.astype(BF16), pw_ref[...],
                                               preferred_element_type=F32).astype(BF16)
        return carry

    lax.fori_loop(0, ROW_TILE // chunk, body, 0)


def _conv_module(xg, tiles_per_seq, dw, db, lg, lb, pw):
    t = xg.shape[0]
    nt = t // ROW_TILE
    per = ROW_TILE // HALO
    last = t // HALO - 1
    return pl.pallas_call(
        functools.partial(_conv_kernel, tiles_per_seq),
        grid=(nt,),
        in_specs=[pl.BlockSpec((HALO, CONV_CH), lambda i: (jnp.maximum(i * per - 1, 0), 0)),
                  pl.BlockSpec((ROW_TILE, CONV_CH), lambda i: (i, 0)),
                  pl.BlockSpec((HALO, CONV_CH), lambda i: (jnp.minimum((i + 1) * per, last), 0)),
                  _full((CONV_K, CONV_CH)), _full((1, CONV_CH)), _full((1, CONV_CH)), _full((1, CONV_CH)),
                  _full((CONV_CH, CONV_CH))],
        out_specs=pl.BlockSpec((ROW_TILE, CONV_CH), lambda i: (i, 0)),
        out_shape=jax.ShapeDtypeStruct((t, CONV_CH), BF16),
        scratch_shapes=[pltpu.VMEM((8, ROW_TILE + 2 * HALO, CONV_CH), F32)],
        compiler_params=_params("arbitrary"),
        name="conv_module",
    )(xg, xg, xg, dw, db, lg, lb, pw)


def _attend(qg, sink_col, parts):
    m = sink_col
    scores = []
    for keys, _, mask in parts:
        s = lax.dot_general(qg, keys, (((1,), (1,)), ((), ())), preferred_element_type=F32)
        if mask is not None:
            s = jnp.where(mask, s, NEG_INF)
        m = jnp.maximum(m, jnp.max(s, axis=1, keepdims=True))
        scores.append(s)
    denom = jnp.exp(sink_col - m)
    out = None
    for s, (_, vals, _) in zip(scores, parts):
        p = jnp.exp(s - m)
        denom = denom + jnp.sum(p, axis=1, keepdims=True)
        o = jnp.dot(p.astype(BF16), vals, preferred_element_type=F32)
        out = o if out is None else out + o
    return out / denom


def _mix_kernel(latent, seq, x_ref, mod_ref, q_ref, k_ref, v_ref, cv_ref, *rest):
    if latent:
        ck_ref, cvv_ref, sink_ref, wout_ref, n2_ref, x1_ref, h2_ref, attn_ref = rest
    else:
        sink_ref, wout_ref, n2_ref, x1_ref, h2_ref, attn_ref = rest
    width = min(seq, ROW_TILE + 2 * WINDOW)
    if latent:
        q0 = pl.program_id(1) * ROW_TILE
        start = pl.multiple_of(jnp.clip(q0 - WINDOW, 0, seq - width), BLOCK)
        qpos = q0 + (lax.broadcasted_iota(jnp.int32, (GROUP * ROW_TILE, width), 0) & (ROW_TILE - 1))
        kpos = start + lax.broadcasted_iota(jnp.int32, (GROUP * ROW_TILE, width), 1)
        mask = jnp.abs(kpos - qpos) <= WINDOW
    else:
        start = 0
        mask = None
    for kv in range(N_KV_HEADS):
        lanes = slice(kv * HEAD_DIM, (kv + 1) * HEAD_DIM)
        parts = [(k_ref[pl.ds(start, width), lanes].astype(BF16),
                  v_ref[pl.ds(start, width), lanes].astype(BF16), mask)]
        if latent:
            parts.append((ck_ref[:, lanes].astype(BF16), cvv_ref[:, lanes].astype(BF16), None))
        sink_col = jnp.concatenate(
            [jnp.full((ROW_TILE, 1), sink_ref[kv * GROUP + g], F32) for g in range(GROUP)], axis=0)
        qg = jnp.concatenate(
            [q_ref[:, (kv * GROUP + g) * HEAD_DIM:(kv * GROUP + g + 1) * HEAD_DIM] for g in range(GROUP)],
            axis=0)
        o = _attend(qg, sink_col, parts)
        for g in range(GROUP):
            h = kv * GROUP + g
            attn_ref[:, h * HEAD_DIM:(h + 1) * HEAD_DIM] = o[g * ROW_TILE:(g + 1) * ROW_TILE].astype(BF16)
    g1 = mod_ref[2:3, :]
    sh2 = mod_ref[3:4, :]
    sc2 = mod_ref[4:5, :]
    mixed = (jnp.dot(attn_ref[...], wout_ref[0:ATTN_WIDTH, :], preferred_element_type=F32)
             + jnp.dot(cv_ref[...], wout_ref[ATTN_WIDTH:, :], preferred_element_type=F32))
    x1 = x_ref[...] + g1 * mixed
    x1_ref[...] = x1
    h2 = x1 * lax.rsqrt(jnp.mean(x1 * x1, axis=-1, keepdims=True) + EPS) * n2_ref[...]
    h2_ref[...] = (h2 * (1.0 + sc2) + sh2).astype(BF16)


def _mix(x3, mod3, mod_row, q3, k3, v3, conv3, cache, sink, w_out, norm2):
    b, seq, _ = x3.shape
    latent = cache is not None
    tile = lambda w: pl.BlockSpec((None, ROW_TILE, w), lambda i, r: (i, r, 0))
    whole = lambda n, w: pl.BlockSpec((None, n, w), lambda i, r: (i, 0, 0))
    in_specs = [tile(D_MODEL),
                pl.BlockSpec((None, 6, D_MODEL), lambda i, r: (mod_row(i), 0, 0)),
                tile(ATTN_WIDTH), whole(seq, KV_WIDTH), whole(seq, KV_WIDTH), tile(CONV_CH)]
    args = [x3, mod3, q3, k3, v3, conv3]
    if latent:
        past = cache[0].shape[1]
        in_specs += [whole(past, KV_WIDTH)] * 2
        args += list(cache)
    in_specs += [pl.BlockSpec(memory_space=pltpu.SMEM), _full((D_MODEL, D_MODEL)), _full((1, D_MODEL))]
    args += [sink, w_out, norm2]
    return pl.pallas_call(
        functools.partial(_mix_kernel, latent, seq),
        grid=(b, seq // ROW_TILE),
        in_specs=in_specs,
        out_specs=[tile(D_MODEL), tile(D_MODEL)],
        out_shape=[jax.ShapeDtypeStruct((b, seq, D_MODEL), F32),
                   jax.ShapeDtypeStruct((b, seq, D_MODEL), BF16)],
        scratch_shapes=[pltpu.VMEM((ROW_TILE, ATTN_WIDTH), BF16)],
        compiler_params=_params("arbitrary", "arbitrary"),
        name="mix_latent" if latent else "mix_context",
    )(*args)


def _top_desc(x, count):
    slot = lax.broadcasted_iota(jnp.int32, (count, x.shape[1]), 0)
    out = jnp.zeros((count, x.shape[1]), F32)
    for k in range(count):
        m = jnp.max(x, axis=0, keepdims=True)
        out = jnp.where(slot == k, m, out)
        x = jnp.where(x == m, NEG_INF, x)
    return out


def _route_kernel(h2_ref, wq_ref, keys_ref, h2t_ref, rank_ref, p2_ref, cnt_ref, p1_ref, qp_ref):
    h2 = h2_ref[...]
    h2t_ref[...] = pltpu.bitcast(h2.astype(F32).T.astype(BF16), jnp.uint32)
    qp_ref[...] = jnp.dot(h2, wq_ref[...], preferred_element_type=F32).astype(BF16)
    nt = (((1,), (1,)), ((), ()))

    def head(h, carry):
        col = pl.multiple_of(h * PEER_QDIM, PEER_QDIM)
        s1 = lax.dot_general(keys_ref[h, 0], qp_ref[:, pl.ds(col, PEER_HALF)], nt,
                             preferred_element_type=F32)
        s2 = lax.dot_general(keys_ref[h, 1], qp_ref[:, pl.ds(col + PEER_HALF, PEER_HALF)], nt,
                             preferred_element_type=F32)
        v1 = _top_desc(s1, PEER_TOPK)
        v2 = _top_desc(s2, PEER_TOPK)
        slot = lax.broadcasted_iota(jnp.int32, (8, ROUTE_TILE), 0)
        cand = [v1[0:1] + v2]
        for j in range(1, 8):
            cand.append(jnp.where(slot < PEER_TOPK // (j + 1), v1[j:j + 1] + v2[0:8], NEG_INF))
        cand.append(v1[8:PEER_TOPK] + v2[0:1])
        best = _top_desc(jnp.concatenate(cand, axis=0), PEER_TOPK)
        tau = best[PEER_TOPK - 1:PEER_TOPK]
        z = jnp.sum(jnp.exp(best - best[0:1]), axis=0, keepdims=True)
        cnt = jnp.zeros_like(s1)
        rank = jnp.zeros_like(s2)
        for k in range(PEER_TOPK):
            vk = v2[k:k + 1]
            cnt = cnt + jnp.where(s1 + vk >= tau, 1.0, 0.0)
            rank = rank + jnp.where(vk > s2, 1.0, 0.0)
        steps = N_EXPERTS // PEER_EXPERTS
        cnt_ref[:, h] = cnt.reshape(steps, N_KEYS // steps, ROUTE_TILE)
        p1_ref[:, h] = jnp.exp(s1 - v1[0:1]).reshape(steps, N_KEYS // steps, ROUTE_TILE)
        rank_ref[h] = pltpu.bitcast(rank.astype(BF16), jnp.uint32)
        p2_ref[h] = pltpu.bitcast((jnp.exp(s2 - v2[0:1]) * (0.5 / z)).astype(BF16), jnp.uint32)
        return carry

    lax.fori_loop(0, PEER_HEADS, head, 0)


def _route(h2, wq, keys):
    t = h2.shape[0]
    tok4 = pl.BlockSpec((PEER_HEADS, N_KEYS // 2, ROUTE_TILE), lambda i: (0, 0, i))
    steps = N_EXPERTS // PEER_EXPERTS
    tok3 = pl.BlockSpec((steps, PEER_HEADS, N_KEYS // steps, ROUTE_TILE), lambda i: (0, 0, 0, i))
    return pl.pallas_call(
        _route_kernel,
        grid=(t // ROUTE_TILE,),
        in_specs=[pl.BlockSpec((ROUTE_TILE, D_MODEL), lambda i: (i, 0)),
                  _full((D_MODEL, PEER_HEADS * PEER_QDIM)),
                  _full((PEER_HEADS, 2, N_KEYS, PEER_HALF))],
        out_specs=[pl.BlockSpec((D_MODEL // 2, ROUTE_TILE), lambda i: (0, i)), tok4, tok4, tok3, tok3],
        out_shape=[jax.ShapeDtypeStruct((D_MODEL // 2, t), jnp.uint32),
                   jax.ShapeDtypeStruct((PEER_HEADS, N_KEYS // 2, t), jnp.uint32),
                   jax.ShapeDtypeStruct((PEER_HEADS, N_KEYS // 2, t), jnp.uint32),
                   jax.ShapeDtypeStruct((steps, PEER_HEADS, N_KEYS // steps, t), F32),
                   jax.ShapeDtypeStruct((steps, PEER_HEADS, N_KEYS // steps, t), F32)],
        scratch_shapes=[pltpu.VMEM((ROUTE_TILE, PEER_HEADS * PEER_QDIM), BF16)],
        compiler_params=_params("arbitrary"),
        name="peer_route",
    )(h2, wq, keys)


def _pack_kernel(transpose, x_ref, o_ref):
    x = x_ref[...]
    if transpose:
        x = x.T
    o_ref[...] = pltpu.bitcast(x.astype(BF16), jnp.uint32)


def _pack_table(table, transpose):
    rows, cols = table.shape
    blk = 512
    if transpose:
        out_spec = pl.BlockSpec((cols // 2, blk), lambda i: (0, i))
        out_shape = jax.ShapeDtypeStruct((cols // 2, rows), jnp.uint32)
    else:
        out_spec = pl.BlockSpec((blk // 2, cols), lambda i: (i, 0))
        out_shape = jax.ShapeDtypeStruct((rows // 2, cols), jnp.uint32)
    return pl.pallas_call(
        functools.partial(_pack_kernel, transpose),
        grid=(rows // blk,),
        in_specs=[pl.BlockSpec((blk, cols), lambda i: (i, 0))],
        out_specs=out_spec,
        out_shape=out_shape,
        compiler_params=_params("arbitrary"),
        name="pack_table_t" if transpose else "pack_table",
    )(table)


def _gate_rows(c, act_ref, wt_ref, rank_ref, p2_ref, cnt_ref, p1_ref):
    rows = slice(c * N_KEYS, (c + 1) * N_KEYS)
    for lt in range(PEER_TOKENS // LANES):
        lanes = slice(lt * LANES, (lt + 1) * LANES)
        tiles = (N_KEYS // BF16_SUBLANES, BF16_SUBLANES, LANES)
        gate = jnp.zeros(tiles, BF16)
        for h in range(PEER_HEADS):
            cnt = jnp.broadcast_to(cnt_ref[h, c:c + 1, lanes], (BF16_SUBLANES, LANES)).astype(BF16)
            p1 = jnp.broadcast_to(p1_ref[h, c:c + 1, lanes], (BF16_SUBLANES, LANES)).astype(BF16)
            rank = pltpu.bitcast(rank_ref[h, :, lanes], BF16).reshape(tiles)
            p2 = pltpu.bitcast(p2_ref[h, :, lanes], BF16).reshape(tiles)
            gate = gate + jnp.where(rank < cnt[None], p2, jnp.zeros((), BF16)) * p1[None]
        a = act_ref[rows, lanes]
        gl = a + a * lax.erf(a * (1.0 / math.sqrt(2.0)))
        wt_ref[rows, lanes] = gl.astype(BF16) * gate.reshape(N_KEYS, LANES)


def _expert_kernel(pairs_per_block, u_ref, vt_ref, h2t_ref, rank_a, p2_a, rank_b, p2_b, cnt_a, p1_a,
                   cnt_b, p1_b, x1_ref, mod_ref, y_ref, acc_ref, act_a, act_b, wt_a, wt_b):
    g = pl.program_id(0)

    @pl.when(g == 0)
    def _():
        acc_ref[...] = jnp.zeros_like(acc_ref)
        act_b[...] = jnp.zeros_like(act_b)
        wt_a[...] = jnp.zeros_like(wt_a)

    key_rows = PEER_EXPERTS // N_KEYS

    def half_step(half, wt_old, act_old, wt_new, act_new, rank_ref, p2_ref, cnt_ref, p1_ref):
        vt = pltpu.bitcast(vt_ref[:, half * PEER_EXPERTS:(half + 1) * PEER_EXPERTS], BF16)
        acc_ref[...] += jnp.dot(vt, wt_old[...], preferred_element_type=F32)
        for c in range(key_rows):
            _gate_rows(c, act_old, wt_new, rank_ref, p2_ref, cnt_ref, p1_ref)
        u = pltpu.bitcast(u_ref[half * (PEER_EXPERTS // 2):(half + 1) * (PEER_EXPERTS // 2), :], BF16)
        act_new[...] = jnp.dot(u, pltpu.bitcast(h2t_ref[...], BF16), preferred_element_type=F32)

    half_step(0, wt_a, act_b, wt_b, act_a, rank_a, p2_a, cnt_a, p1_a)
    half_step(1, wt_b, act_a, wt_a, act_b, rank_b, p2_b, cnt_b, p1_b)

    @pl.when((g > 0) & (g % pairs_per_block == 0))
    def _():
        g2 = mod_ref[5:6, :]
        y_ref[...] = x1_ref[...] + g2 * acc_ref[...].T
        acc_ref[...] = jnp.zeros_like(acc_ref)


def _experts(u_bf, vt_bf, h2t, rank, p2, cnt, p1, x1, mod3, mod_row):
    t = x1.shape[0]
    ppb = N_EXPERTS // (2 * PEER_EXPERTS)
    steps = (t // PEER_TOKENS) * ppb + 1
    cur = lambda g: jnp.minimum(g, steps - 2)
    prv = lambda g: jnp.maximum(g - 1, 0)
    packed = lambda f: pl.BlockSpec((PEER_HEADS, N_KEYS // 2, PEER_TOKENS), lambda g: (0, 0, f(g) // ppb))
    rows = lambda f, half: pl.BlockSpec((None, PEER_HEADS, PEER_EXPERTS // N_KEYS, PEER_TOKENS),
                                        lambda g: (2 * (f(g) % ppb) + half, 0, 0, f(g) // ppb))
    return pl.pallas_call(
        functools.partial(_expert_kernel, ppb),
        grid=(steps,),
        in_specs=[pl.BlockSpec((PEER_EXPERTS, D_MODEL), lambda g: (cur(g) % ppb, 0)),
                  pl.BlockSpec((D_MODEL // 2, 2 * PEER_EXPERTS), lambda g: (0, prv(g) % ppb)),
                  pl.BlockSpec((D_MODEL // 2, PEER_TOKENS), lambda g: (0, cur(g) // ppb)),
                  packed(prv), packed(prv), packed(cur), packed(cur),
                  rows(prv, 1), rows(prv, 1), rows(cur, 0), rows(cur, 0),
                  pl.BlockSpec((PEER_TOKENS, D_MODEL), lambda g: (prv(g) // ppb, 0)),
                  pl.BlockSpec((None, 6, D_MODEL), lambda g: (mod_row(prv(g) // ppb), 0, 0))],
        out_specs=pl.BlockSpec((PEER_TOKENS, D_MODEL), lambda g: (prv(g) // ppb, 0)),
        out_shape=jax.ShapeDtypeStruct((t, D_MODEL), F32),
        scratch_shapes=[pltpu.VMEM((D_MODEL, PEER_TOKENS), F32),
                        pltpu.VMEM((PEER_EXPERTS, PEER_TOKENS), F32),
                        pltpu.VMEM((PEER_EXPERTS, PEER_TOKENS), F32),
                        pltpu.VMEM((PEER_EXPERTS, PEER_TOKENS), BF16),
                        pltpu.VMEM((PEER_EXPERTS, PEER_TOKENS), BF16)],
        compiler_params=_params("arbitrary"),
        name="peer_experts",
    )(u_bf, vt_bf, h2t, rank, p2, rank, p2, cnt, p1, cnt, p1, x1, mod3)


def _rope_tables(seq_len):
    rows = seq_len // GRID_W
    row = jnp.repeat(jnp.arange(rows, dtype=F32), GRID_W)
    col = jnp.tile(jnp.arange(GRID_W, dtype=F32), rows)
    inv = ROPE_THETA ** (-jnp.arange(AXIS_FREQS, dtype=F32) / AXIS_FREQS)
    ang_row = row[:, None] * inv
    ang_col = col[:, None] * inv
    zeros = jnp.zeros_like(ang_row)
    cos = jnp.concatenate([jnp.cos(ang_row)] * 2 + [jnp.cos(ang_col)] * 2, axis=-1)
    sin_lo = jnp.concatenate([-jnp.sin(ang_row), zeros, -jnp.sin(ang_col), zeros], axis=-1)
    sin_hi = jnp.concatenate([zeros, jnp.sin(ang_row), zeros, jnp.sin(ang_col)], axis=-1)
    return tuple(jnp.tile(t, (1, N_HEADS)) for t in (cos, sin_lo, sin_hi))


def _head_mean_matrix(width):
    idx = jnp.arange(width) // HEAD_DIM
    return jnp.where(idx[:, None] == idx[None, :], 1.0 / HEAD_DIM, 0.0).astype(BF16)


def _trunk(x3, mod3, mod_of_seq, cache, w, rope):
    b, seq, _ = x3.shape
    t = b * seq
    tiles_per_seq = seq // ROW_TILE
    q, k, v, xg = _inproj(x3.reshape(t, D_MODEL), mod3, lambda i: mod_of_seq(i // tiles_per_seq),
                          tiles_per_seq, w["norm1"], w["w_in"], w["qn"], w["kn"], w["gq"], w["gk"], rope)
    conv = _conv_module(xg, tiles_per_seq, w["conv_dw"], w["conv_dw_b"], w["conv_ln_g"], w["conv_ln_b"],
                        w["conv_pw"])
    x1, h2 = _mix(x3, mod3, mod_of_seq, q.reshape(b, seq, ATTN_WIDTH), k.reshape(b, seq, KV_WIDTH),
                  v.reshape(b, seq, KV_WIDTH), conv.reshape(b, seq, CONV_CH), cache, w["sink"],
                  w["w_out"], w["norm2"])
    h2t, rank, p2, cnt, p1 = _route(h2.reshape(t, D_MODEL), w["peer_wq"], w["peer_keys"])
    per_seq = seq // PEER_TOKENS
    if per_seq == 0:
        seqs_per_block = PEER_TOKENS // seq
        row_of_block = lambda i: mod_of_seq(i * seqs_per_block)
    else:
        row_of_block = lambda i: mod_of_seq(i // per_seq)
    y = _experts(w["peer_u"], w["peer_vt"], h2t, rank, p2, cnt, p1, x1.reshape(t, D_MODEL), mod3, row_of_block)
    return y.reshape(b, seq, D_MODEL), k, v


def kernel(x_prompt, x_sample, cache_k, cache_v, c, c_ctx, norm1, norm2, w_ada, b_ada, w_in, w_out,
           q_norm, k_norm, attn_sink, conv_dw, conv_dw_b, conv_ln_g, conv_ln_b, conv_pw,
           peer_wq, peer_keys, peer_u, peer_v):
    depth = norm1.shape[0]
    batch, seq, _ = x_prompt.shape
    dec_batch, dec_seq, _ = x_sample.shape
    past = cache_k.shape[2]
    assert dec_batch + 1 <= 8 and depth >= 1
    cond8 = jnp.concatenate([c_ctx[None, :], c, jnp.zeros((8 - 1 - dec_batch, D_MODEL), F32)], axis=0)
    rope = _rope_tables(dec_seq)
    xp, xs = x_prompt, x_sample
    new_ks, new_vs = [], []
    for l in range(depth):
        w = dict(
            norm1=norm1[l][None], norm2=norm2[l][None],
            w_in=w_in[l].astype(BF16), w_out=w_out[l].astype(BF16),
            qn=jnp.tile(q_norm[l], N_HEADS)[None], kn=jnp.tile(k_norm[l], N_KV_HEADS)[None],
            gq=_head_mean_matrix(ATTN_WIDTH), gk=_head_mean_matrix(KV_WIDTH),
            sink=attn_sink[l],
            conv_dw=conv_dw[l], conv_dw_b=conv_dw_b[l][None], conv_ln_g=conv_ln_g[l][None],
            conv_ln_b=conv_ln_b[l][None], conv_pw=conv_pw[l].astype(BF16),
            peer_wq=peer_wq[l].astype(BF16), peer_keys=peer_keys[l].astype(BF16),
            peer_u=_pack_table(peer_u[l], False), peer_vt=_pack_table(peer_v[l], True),
        )
        mod3 = _modulation(cond8, w_ada[l], b_ada[l]).reshape(8, 6, D_MODEL)
        xp, k_c, v_c = _trunk(xp, mod3, lambda s: 0, None, w, None)
        new_ks.append(k_c.reshape(batch, seq, N_KV_HEADS, HEAD_DIM))
        new_vs.append(v_c.reshape(batch, seq, N_KV_HEADS, HEAD_DIM))
        cache = (cache_k[:, l].reshape(dec_batch, past, KV_WIDTH), cache_v[:, l].reshape(dec_batch, past, KV_WIDTH))
        xs, _, _ = _trunk(xs, mod3, lambda s: s + 1, cache, w, rope)
    return (xp, xs, jnp.stack(new_ks, axis=1), jnp.stack(new_vs, axis=1))
```

```python
import functools
import math

import jax
import jax.numpy as jnp
from jax import lax
from jax.experimental import pallas as pl
from jax.experimental.pallas import tpu as pltpu

F32 = jnp.float32
BF16 = jnp.bfloat16

D_MODEL = 1024
GRID_W = 64
N_HEADS = 8
N_KV_HEADS = 2
GROUP = N_HEADS // N_KV_HEADS
HEAD_DIM = 64
ATTN_WIDTH = N_HEADS * HEAD_DIM
KV_WIDTH = N_KV_HEADS * HEAD_DIM
WINDOW = 128
BLOCK = 128
ROPE_THETA = 10000.0
AXIS_DIM = HEAD_DIM // 2
AXIS_FREQS = AXIS_DIM // 2
CONV_CH = D_MODEL - ATTN_WIDTH
CONV_K = 31
CONV_PAD = CONV_K // 2
IN_WIDTH = ATTN_WIDTH + 2 * KV_WIDTH + 2 * CONV_CH
N_KEYS = 128
N_EXPERTS = N_KEYS * N_KEYS
PEER_HEADS = 8
PEER_QDIM = 256
PEER_HALF = PEER_QDIM // 2
PEER_TOPK = 16
EPS = 1e-6

LANES = 128
BF16_SUBLANES = 16
ROW_TILE = 256
MIX_TILE = 512
HALO = 16
ROUTE_TILE = 256
PEER_TOKENS = 512
PEER_EXPERTS = 1024
VMEM_LIMIT = 48 * 1024 * 1024

NEG_INF = float("-inf")


def _params(*sem, flags=None):
    return pltpu.CompilerParams(dimension_semantics=sem, vmem_limit_bytes=VMEM_LIMIT, flags=flags)


def _full(shape):
    return pl.BlockSpec(shape, lambda *_: (0,) * len(shape))


def _mod_kernel(cond_ref, w_ref, b_ref, o_ref):
    cnd = cond_ref[...]
    act = cnd * jax.nn.sigmoid(cnd)
    o_ref[...] = jnp.dot(act, w_ref[...], precision=lax.Precision.HIGHEST,
                         preferred_element_type=F32) + b_ref[...]


def _modulation(cond8, w_ada, b_ada):
    n = w_ada.shape[1]
    nb = n // D_MODEL
    return pl.pallas_call(
        _mod_kernel,
        grid=(nb,),
        in_specs=[_full((8, D_MODEL)),
                  pl.BlockSpec((D_MODEL, D_MODEL), lambda j: (0, j)),
                  pl.BlockSpec((1, D_MODEL), lambda j: (0, j))],
        out_specs=pl.BlockSpec((8, D_MODEL), lambda j: (0, j)),
        out_shape=jax.ShapeDtypeStruct((8, n), F32),
        compiler_params=_params("arbitrary"),
        name="modulation",
    )(cond8, w_ada, b_ada.reshape(1, n))


def _group_rms(t, gmat, gain):
    ms = jnp.dot((t * t).astype(BF16), gmat, preferred_element_type=F32)
    return t * lax.rsqrt(ms + EPS) * gain


def _rope(t, cos, sin_lo, sin_hi):
    w = t.shape[1]
    return (t * cos + pltpu.roll(t, w - AXIS_FREQS, 1) * sin_lo
            + pltpu.roll(t, AXIS_FREQS, 1) * sin_hi)


def _inproj_kernel(latent, x_ref, mod_ref, n1_ref, win_ref, qn_ref, kn_ref, gq_ref, gk_ref, *rest):
    if latent:
        cos_ref, slo_ref, shi_ref, q_ref, k_ref, v_ref, xg_ref = rest
    else:
        q_ref, k_ref, v_ref, xg_ref = rest
    x = x_ref[...]
    sh1 = mod_ref[0:1, :]
    sc1 = mod_ref[1:2, :]
    h = x * lax.rsqrt(jnp.mean(x * x, axis=-1, keepdims=True) + EPS) * n1_ref[...]
    h = h * (1.0 + sc1) + sh1
    z = jnp.dot(h.astype(BF16), win_ref[...], preferred_element_type=F32)
    q = _group_rms(z[:, :ATTN_WIDTH], gq_ref[...], qn_ref[...])
    k = _group_rms(z[:, ATTN_WIDTH:ATTN_WIDTH + KV_WIDTH], gk_ref[...], kn_ref[...])
    if latent:
        q = _rope(q, cos_ref[...], slo_ref[...], shi_ref[...])
        k = _rope(k, cos_ref[:, :KV_WIDTH], slo_ref[:, :KV_WIDTH], shi_ref[:, :KV_WIDTH])
    q_ref[...] = (q * (1.0 / math.sqrt(HEAD_DIM))).astype(BF16)
    k_ref[...] = k
    v_ref[...] = z[:, ATTN_WIDTH + KV_WIDTH:ATTN_WIDTH + 2 * KV_WIDTH]
    u = z[:, ATTN_WIDTH + 2 * KV_WIDTH:]
    xg_ref[...] = u[:, :CONV_CH] * jax.nn.sigmoid(u[:, CONV_CH:])


def _inproj(x2d, mod3, mod_row, tiles_per_seq, norm1, w_in, qn, kn, gq, gk, rope):
    t = x2d.shape[0]
    latent = rope is not None
    row = lambda i: (i, 0)
    in_specs = [pl.BlockSpec((ROW_TILE, D_MODEL), row),
                pl.BlockSpec((None, 6, D_MODEL), lambda i: (mod_row(i), 0, 0)),
                _full((1, D_MODEL)), _full((D_MODEL, IN_WIDTH)),
                _full((1, ATTN_WIDTH)), _full((1, KV_WIDTH)),
                _full((ATTN_WIDTH, ATTN_WIDTH)), _full((KV_WIDTH, KV_WIDTH))]
    args = [x2d, mod3, norm1, w_in, qn, kn, gq, gk]
    if latent:
        pos = lambda i: (i % tiles_per_seq, 0)
        in_specs += [pl.BlockSpec((ROW_TILE, ATTN_WIDTH), pos)] * 3
        args += list(rope)
    return pl.pallas_call(
        functools.partial(_inproj_kernel, latent),
        grid=(t // ROW_TILE,),
        in_specs=in_specs,
        out_specs=[pl.BlockSpec((ROW_TILE, ATTN_WIDTH), row),
                   pl.BlockSpec((ROW_TILE, KV_WIDTH), row),
                   pl.BlockSpec((ROW_TILE, KV_WIDTH), row),
                   pl.BlockSpec((ROW_TILE, CONV_CH), row)],
        out_shape=[jax.ShapeDtypeStruct((t, ATTN_WIDTH), BF16),
                   jax.ShapeDtypeStruct((t, KV_WIDTH), F32),
                   jax.ShapeDtypeStruct((t, KV_WIDTH), F32),
                   jax.ShapeDtypeStruct((t, CONV_CH), F32)],
        compiler_params=_params("arbitrary"),
        name="inproj_latent" if latent else "inproj_context",
    )(*args)


def _conv_kernel(tiles_per_seq, prev_ref, cur_ref, nxt_ref, dw_ref, db_ref, lg_ref, lb_ref, pw_ref,
                 o_ref, xs_ref, act_ref):
    i = pl.program_id(0)
    pos = i % tiles_per_seq
    zero = jnp.zeros((HALO, CONV_CH), F32)
    xs_ref[0, 0:HALO, :] = jnp.where(pos > 0, prev_ref[...], zero)
    xs_ref[0, HALO:HALO + ROW_TILE, :] = cur_ref[...]
    xs_ref[0, HALO + ROW_TILE:2 * HALO + ROW_TILE, :] = jnp.where(pos < tiles_per_seq - 1, nxt_ref[...], zero)
    span = ROW_TILE + 2 * HALO - 8
    for r in range(1, 8):
        xs_ref[r, 0:span, :] = xs_ref[0, r:r + span, :]
    chunk = 32

    def body(c, carry):
        base = pl.multiple_of(c * chunk, chunk)
        acc = jnp.zeros((chunk, CONV_CH), F32) + db_ref[...]
        for k in range(CONV_K):
            kk = k + HALO - CONV_PAD
            start = pl.multiple_of(base + 8 * (kk // 8), 8)
            acc = acc + xs_ref[kk % 8, pl.ds(start, chunk), :] * dw_ref[k:k + 1, :]
        act_ref[pl.ds(base, chunk), :] = acc
        return carry

    lax.fori_loop(0, ROW_TILE // chunk, body, 0)
    acc = act_ref[...]
    mu = jnp.mean(acc, axis=-1, keepdims=True)
    cen = acc - mu
    var = jnp.mean(cen * cen, axis=-1, keepdims=True)
    y = cen * lax.rsqrt(var + EPS) * lg_ref[...] + lb_ref[...]
    y = y * jax.nn.sigmoid(y)
    o_ref[...] = jnp.dot(y.astype(BF16), pw_ref[...], preferred_element_type=F32).astype(BF16)


def _conv_module(xg, tiles_per_seq, dw, db, lg, lb, pw):
    t = xg.shape[0]
    nt = t // ROW_TILE
    per = ROW_TILE // HALO
    last = t // HALO - 1
    return pl.pallas_call(
        functools.partial(_conv_kernel, tiles_per_seq),
        grid=(nt,),
        in_specs=[pl.BlockSpec((HALO, CONV_CH), lambda i: (jnp.maximum(i * per - 1, 0), 0)),
                  pl.BlockSpec((ROW_TILE, CONV_CH), lambda i: (i, 0)),
                  pl.BlockSpec((HALO, CONV_CH), lambda i: (jnp.minimum((i + 1) * per, last), 0)),
                  _full((CONV_K, CONV_CH)), _full((1, CONV_CH)), _full((1, CONV_CH)), _full((1, CONV_CH)),
                  _full((CONV_CH, CONV_CH))],
        out_specs=pl.BlockSpec((ROW_TILE, CONV_CH), lambda i: (i, 0)),
        out_shape=jax.ShapeDtypeStruct((t, CONV_CH), BF16),
        scratch_shapes=[pltpu.VMEM((8, ROW_TILE + 2 * HALO, CONV_CH), F32),
                        pltpu.VMEM((ROW_TILE, CONV_CH), F32)],
        compiler_params=_params("arbitrary"),
        name="conv_module",
    )(xg, xg, xg, dw, db, lg, lb, pw)


def _attend(qg, sink_col, parts):
    m = sink_col
    scores = []
    for keys, _, mask in parts:
        s = lax.dot_general(qg, keys, (((1,), (1,)), ((), ())), preferred_element_type=F32)
        if mask is not None:
            s = jnp.where(mask, s, NEG_INF)
        m = jnp.maximum(m, jnp.max(s, axis=1, keepdims=True))
        scores.append(s)
    denom = jnp.exp(sink_col - m)
    out = None
    for s, (_, vals, _) in zip(scores, parts):
        p = jnp.exp(s - m)
        denom = denom + jnp.sum(p, axis=1, keepdims=True)
        o = jnp.dot(p.astype(BF16), vals, preferred_element_type=F32)
        out = o if out is None else out + o
    return out / denom


def _mix_kernel(latent, seq, x_ref, mod_ref, q_ref, k_ref, v_ref, cv_ref, *rest):
    if latent:
        ck_ref, cvv_ref, sink_ref, wout_ref, n2_ref, x1_ref, h2_ref, attn_ref = rest
    else:
        sink_ref, wout_ref, n2_ref, x1_ref, h2_ref, attn_ref = rest
    width = min(seq, ROW_TILE + 2 * WINDOW)
    mix_tile = x_ref.shape[0]
    steps_per_seq = max(seq // mix_tile, 1)
    for sub in range(mix_tile // ROW_TILE):
        rows = slice(sub * ROW_TILE, (sub + 1) * ROW_TILE)
        if latent:
            q0 = (pl.program_id(0) % steps_per_seq) * mix_tile + sub * ROW_TILE
            start = pl.multiple_of(jnp.clip(q0 - WINDOW, 0, seq - width), BLOCK)
            qpos = q0 + (lax.broadcasted_iota(jnp.int32, (GROUP * ROW_TILE, width), 0) & (ROW_TILE - 1))
            kpos = start + lax.broadcasted_iota(jnp.int32, (GROUP * ROW_TILE, width), 1)
            mask = jnp.abs(kpos - qpos) <= WINDOW
        else:
            start = sub * ROW_TILE
            mask = None
        for kv in range(N_KV_HEADS):
            lanes = slice(kv * HEAD_DIM, (kv + 1) * HEAD_DIM)
            parts = [(k_ref[pl.ds(start, width), lanes].astype(BF16),
                      v_ref[pl.ds(start, width), lanes].astype(BF16), mask)]
            if latent:
                parts.append((ck_ref[:, lanes].astype(BF16), cvv_ref[:, lanes].astype(BF16), None))
            sink_col = jnp.concatenate(
                [jnp.full((ROW_TILE, 1), sink_ref[kv * GROUP + g], F32) for g in range(GROUP)], axis=0)
            qg = jnp.concatenate(
                [q_ref[rows, (kv * GROUP + g) * HEAD_DIM:(kv * GROUP + g + 1) * HEAD_DIM]
                 for g in range(GROUP)], axis=0)
            o = _attend(qg, sink_col, parts)
            for g in range(GROUP):
                h = kv * GROUP + g
                attn_ref[rows, h * HEAD_DIM:(h + 1) * HEAD_DIM] = o[g * ROW_TILE:(g + 1) * ROW_TILE].astype(BF16)
    g1 = mod_ref[2:3, :]
    sh2 = mod_ref[3:4, :]
    sc2 = mod_ref[4:5, :]
    mixed = (jnp.dot(attn_ref[...], wout_ref[0:ATTN_WIDTH, :], preferred_element_type=F32)
             + jnp.dot(cv_ref[...], wout_ref[ATTN_WIDTH:, :], preferred_element_type=F32))
    x1 = x_ref[...] + g1 * mixed
    x1_ref[...] = x1
    h2 = x1 * lax.rsqrt(jnp.mean(x1 * x1, axis=-1, keepdims=True) + EPS) * n2_ref[...]
    h2_ref[...] = (h2 * (1.0 + sc2) + sh2).astype(BF16)


def _mix(x3, mod3, mod_row, q3, k3, v3, conv3, cache, sink, w_out, norm2):
    b, seq, _ = x3.shape
    latent = cache is not None
    t = b * seq
    if latent:
        mix_tile = ROW_TILE
        assert seq % mix_tile == 0
        steps_per_seq = seq // mix_tile
        seq_of = lambda i: i // steps_per_seq
        kv_spec = pl.BlockSpec((seq, KV_WIDTH), lambda i: (seq_of(i), 0))
    else:
        mix_tile = MIX_TILE
        assert seq == ROW_TILE and t % mix_tile == 0
        seqs_per_step = mix_tile // seq
        seq_of = lambda i: i * seqs_per_step
        kv_spec = pl.BlockSpec((mix_tile, KV_WIDTH), lambda i: (i, 0))
    tile = lambda w: pl.BlockSpec((mix_tile, w), lambda i: (i, 0))
    flat = lambda a: a.reshape(t, a.shape[-1])
    in_specs = [tile(D_MODEL),
                pl.BlockSpec((None, 6, D_MODEL), lambda i: (mod_row(seq_of(i)), 0, 0)),
                tile(ATTN_WIDTH), kv_spec, kv_spec, tile(CONV_CH)]
    args = [flat(x3), mod3, flat(q3), flat(k3), flat(v3), flat(conv3)]
    if latent:
        past = cache[0].shape[1]
        in_specs += [pl.BlockSpec((None, past, KV_WIDTH), lambda i: (seq_of(i), 0, 0))] * 2
        args += list(cache)
    in_specs += [pl.BlockSpec(memory_space=pltpu.SMEM), _full((D_MODEL, D_MODEL)), _full((1, D_MODEL))]
    args += [sink, w_out, norm2]
    x1, h2 = pl.pallas_call(
        functools.partial(_mix_kernel, latent, seq),
        grid=(t // mix_tile,),
        in_specs=in_specs,
        out_specs=[tile(D_MODEL), tile(D_MODEL)],
        out_shape=[jax.ShapeDtypeStruct((t, D_MODEL), F32),
                   jax.ShapeDtypeStruct((t, D_MODEL), BF16)],
        scratch_shapes=[pltpu.VMEM((mix_tile, ATTN_WIDTH), BF16)],
        compiler_params=_params("arbitrary"),
        name="mix_latent" if latent else "mix_context",
    )(*args)
    return x1.reshape(b, seq, D_MODEL), h2.reshape(b, seq, D_MODEL)


def _top_desc(x, count):
    slot = lax.broadcasted_iota(jnp.int32, (count, x.shape[1]), 0)
    out = jnp.zeros((count, x.shape[1]), F32)
    for k in range(count):
        m = jnp.max(x, axis=0, keepdims=True)
        out = jnp.where(slot == k, m, out)
        x = jnp.where(x == m, NEG_INF, x)
    return out


def _merge_exchange_pairs(n):
    pairs, p = [], 1
    while p < n:
        k = p
        while k >= 1:
            for j in range(k % p, n - k, 2 * k):
                for i in range(min(k, n - j - k)):
                    if (i + j) // (2 * p) == (i + j + k) // (2 * p):
                        pairs.append((i + j, i + j + k))
            k //= 2
        p *= 2
    return pairs


def _top_desc_sorted(x, count):
    lists = [x[8 * g:8 * (g + 1)] for g in range(count)]
    for a, b in _merge_exchange_pairs(count):
        lists[a], lists[b] = jnp.maximum(lists[a], lists[b]), jnp.minimum(lists[a], lists[b])
    slot = lax.broadcasted_iota(jnp.int32, (count, x.shape[1]), 0)
    out = jnp.zeros((count, x.shape[1]), F32)
    for k in range(count):
        m = jnp.max(lists[0], axis=0, keepdims=True)
        out = jnp.where(slot == k, m, out)
        hit = lists[0] == m
        for lvl in range(count - 1 - k):
            lists[lvl] = jnp.where(hit, lists[lvl + 1], lists[lvl])
    return out


def _route_kernel(h2_ref, wq_ref, keys_ref, h2t_ref, rank_ref, p2_ref, cnt_ref, p1_ref, qp_ref):
    h2 = h2_ref[...]
    h2t_ref[...] = pltpu.bitcast(h2.astype(F32).T.astype(BF16), jnp.uint32)
    qp_ref[...] = jnp.dot(h2, wq_ref[...], preferred_element_type=F32).astype(BF16)
    nt = (((1,), (1,)), ((), ()))

    def head(h, carry):
        col = pl.multiple_of(h * PEER_QDIM, PEER_QDIM)
        s1 = lax.dot_general(keys_ref[h, 0], qp_ref[:, pl.ds(col, PEER_HALF)], nt,
                             preferred_element_type=F32)
        s2 = lax.dot_general(keys_ref[h, 1], qp_ref[:, pl.ds(col + PEER_HALF, PEER_HALF)], nt,
                             preferred_element_type=F32)
        v1 = _top_desc_sorted(s1, PEER_TOPK)
        v2 = _top_desc_sorted(s2, PEER_TOPK)
        slot = lax.broadcasted_iota(jnp.int32, (8, ROUTE_TILE), 0)
        cand = [v1[0:1] + v2]
        for j in range(1, 8):
            cand.append(jnp.where(slot < PEER_TOPK // (j + 1), v1[j:j + 1] + v2[0:8], NEG_INF))
        cand.append(v1[8:PEER_TOPK] + v2[0:1])
        best = _top_desc(jnp.concatenate(cand, axis=0), PEER_TOPK)
        tau = best[PEER_TOPK - 1:PEER_TOPK]
        z = jnp.sum(jnp.exp(best - best[0:1]), axis=0, keepdims=True)
        rank = jnp.zeros_like(s2)
        cnt_top = jnp.zeros_like(v1)
        for k in range(PEER_TOPK):
            vk = v2[k:k + 1]
            rank = jnp.where(vk > s2, k + 1.0, rank)
            cnt_top = jnp.where(v1 + vk >= tau, k + 1.0, cnt_top)
        cnt = jnp.zeros_like(s1)
        for j in range(PEER_TOPK):
            cnt = jnp.where(s1 == v1[j:j + 1], cnt_top[j:j + 1], cnt)
        steps = N_EXPERTS // PEER_EXPERTS
        cnt_ref[:, h] = cnt.reshape(steps, N_KEYS // steps, ROUTE_TILE)
        p1_ref[:, h] = jnp.exp(s1 - v1[0:1]).reshape(steps, N_KEYS // steps, ROUTE_TILE)
        rank_ref[h] = pltpu.bitcast(rank.astype(BF16), jnp.uint32)
        p2_ref[h] = pltpu.bitcast((jnp.exp(s2 - v2[0:1]) * (0.5 / z)).astype(BF16), jnp.uint32)
        return carry

    lax.fori_loop(0, PEER_HEADS, head, 0)


def _route(h2, wq, keys):
    t = h2.shape[0]
    tok4 = pl.BlockSpec((PEER_HEADS, N_KEYS // 2, ROUTE_TILE), lambda i: (0, 0, i))
    steps = N_EXPERTS // PEER_EXPERTS
    tok3 = pl.BlockSpec((steps, PEER_HEADS, N_KEYS // steps, ROUTE_TILE), lambda i: (0, 0, 0, i))
    return pl.pallas_call(
        _route_kernel,
        grid=(t // ROUTE_TILE,),
        in_specs=[pl.BlockSpec((ROUTE_TILE, D_MODEL), lambda i: (i, 0)),
                  _full((D_MODEL, PEER_HEADS * PEER_QDIM)),
                  _full((PEER_HEADS, 2, N_KEYS, PEER_HALF))],
        out_specs=[pl.BlockSpec((D_MODEL // 2, ROUTE_TILE), lambda i: (0, i)), tok4, tok4, tok3, tok3],
        out_shape=[jax.ShapeDtypeStruct((D_MODEL // 2, t), jnp.uint32),
                   jax.ShapeDtypeStruct((PEER_HEADS, N_KEYS // 2, t), jnp.uint32),
                   jax.ShapeDtypeStruct((PEER_HEADS, N_KEYS // 2, t), jnp.uint32),
                   jax.ShapeDtypeStruct((steps, PEER_HEADS, N_KEYS // steps, t), F32),
                   jax.ShapeDtypeStruct((steps, PEER_HEADS, N_KEYS // steps, t), F32)],
        scratch_shapes=[pltpu.VMEM((ROUTE_TILE, PEER_HEADS * PEER_QDIM), BF16)],
        compiler_params=_params("arbitrary"),
        name="peer_route",
    )(h2, wq, keys)


def _pack_kernel(transpose, x_ref, o_ref):
    x = x_ref[...]
    if transpose:
        x = x.T
    o_ref[...] = pltpu.bitcast(x.astype(BF16), jnp.uint32)


def _pack_table(table, transpose):
    rows, cols = table.shape
    blk = 512
    if transpose:
        out_spec = pl.BlockSpec((cols // 2, blk), lambda i: (0, i))
        out_shape = jax.ShapeDtypeStruct((cols // 2, rows), jnp.uint32)
    else:
        out_spec = pl.BlockSpec((blk // 2, cols), lambda i: (i, 0))
        out_shape = jax.ShapeDtypeStruct((rows // 2, cols), jnp.uint32)
    return pl.pallas_call(
        functools.partial(_pack_kernel, transpose),
        grid=(rows // blk,),
        in_specs=[pl.BlockSpec((blk, cols), lambda i: (i, 0))],
        out_specs=out_spec,
        out_shape=out_shape,
        compiler_params=_params("arbitrary"),
        name="pack_table_t" if transpose else "pack_table",
    )(table)


def _gate_rows(c, act_ref, wt_ref, rank_ref, p2_ref, cnt_ref, p1_ref):
    rows = slice(c * N_KEYS, (c + 1) * N_KEYS)
    for lt in range(PEER_TOKENS // LANES):
        lanes = slice(lt * LANES, (lt + 1) * LANES)
        tiles = (N_KEYS // BF16_SUBLANES, BF16_SUBLANES, LANES)
        gate = jnp.zeros(tiles, BF16)
        for h in range(PEER_HEADS):
            cnt = jnp.broadcast_to(cnt_ref[h, c:c + 1, lanes], (BF16_SUBLANES, LANES)).astype(BF16)
            p1 = jnp.broadcast_to(p1_ref[h, c:c + 1, lanes], (BF16_SUBLANES, LANES)).astype(BF16)
            rank = pltpu.bitcast(rank_ref[h, :, lanes], BF16).reshape(tiles)
            p2 = pltpu.bitcast(p2_ref[h, :, lanes], BF16).reshape(tiles)
            gate = gate + jnp.where(rank < cnt[None], p2, jnp.zeros((), BF16)) * p1[None]
        a = act_ref[rows, lanes]
        gl = a + a * lax.erf(a * (1.0 / math.sqrt(2.0)))
        wt_ref[rows, lanes] = gl.astype(BF16) * gate.reshape(N_KEYS, LANES)


def _expert_kernel(pairs_per_block, u_ref, vt_ref, h2t_ref, rank_a, p2_a, rank_b, p2_b, cnt_a, p1_a,
                   cnt_b, p1_b, x1_ref, mod_ref, y_ref, acc_ref, act_a, act_b, wt_a, wt_b):
    g = pl.program_id(0)

    @pl.when(g == 0)
    def _():
        acc_ref[...] = jnp.zeros_like(acc_ref)
        act_b[...] = jnp.zeros_like(act_b)
        wt_a[...] = jnp.zeros_like(wt_a)

    key_rows = PEER_EXPERTS // N_KEYS

    def half_step(half, wt_old, act_old, wt_new, act_new, rank_ref, p2_ref, cnt_ref, p1_ref):
        vt = pltpu.bitcast(vt_ref[:, half * PEER_EXPERTS:(half + 1) * PEER_EXPERTS], BF16)
        acc_ref[...] += jnp.dot(vt, wt_old[...], preferred_element_type=F32)
        for c in range(key_rows):
            _gate_rows(c, act_old, wt_new, rank_ref, p2_ref, cnt_ref, p1_ref)
        u = pltpu.bitcast(u_ref[half * (PEER_EXPERTS // 2):(half + 1) * (PEER_EXPERTS // 2), :], BF16)
        act_new[...] = jnp.dot(u, pltpu.bitcast(h2t_ref[...], BF16), preferred_element_type=F32)

    half_step(0, wt_a, act_b, wt_b, act_a, rank_a, p2_a, cnt_a, p1_a)
    half_step(1, wt_b, act_a, wt_a, act_b, rank_b, p2_b, cnt_b, p1_b)

    @pl.when((g > 0) & (g % pairs_per_block == 0))
    def _():
        g2 = mod_ref[5:6, :]
        y_ref[...] = x1_ref[...] + g2 * acc_ref[...].T
        acc_ref[...] = jnp.zeros_like(acc_ref)


def _experts(u_bf, vt_bf, h2t, rank, p2, cnt, p1, x1, mod3, mod_row):
    t = x1.shape[0]
    ppb = N_EXPERTS // (2 * PEER_EXPERTS)
    steps = (t // PEER_TOKENS) * ppb + 1
    cur = lambda g: jnp.minimum(g, steps - 2)
    prv = lambda g: jnp.maximum(g - 1, 0)
    packed = lambda f: pl.BlockSpec((PEER_HEADS, N_KEYS // 2, PEER_TOKENS), lambda g: (0, 0, f(g) // ppb))
    rows = lambda f, half: pl.BlockSpec((None, PEER_HEADS, PEER_EXPERTS // N_KEYS, PEER_TOKENS),
                                        lambda g: (2 * (f(g) % ppb) + half, 0, 0, f(g) // ppb))
    return pl.pallas_call(
        functools.partial(_expert_kernel, ppb),
        grid=(steps,),
        in_specs=[pl.BlockSpec((PEER_EXPERTS, D_MODEL), lambda g: (cur(g) % ppb, 0)),
                  pl.BlockSpec((D_MODEL // 2, 2 * PEER_EXPERTS), lambda g: (0, prv(g) % ppb)),
                  pl.BlockSpec((D_MODEL // 2, PEER_TOKENS), lambda g: (0, cur(g) // ppb)),
                  packed(prv), packed(prv), packed(cur), packed(cur),
                  rows(prv, 1), rows(prv, 1), rows(cur, 0), rows(cur, 0),
                  pl.BlockSpec((PEER_TOKENS, D_MODEL), lambda g: (prv(g) // ppb, 0)),
                  pl.BlockSpec((None, 6, D_MODEL), lambda g: (mod_row(prv(g) // ppb), 0, 0))],
        out_specs=pl.BlockSpec((PEER_TOKENS, D_MODEL), lambda g: (prv(g) // ppb, 0)),
        out_shape=jax.ShapeDtypeStruct((t, D_MODEL), F32),
        scratch_shapes=[pltpu.VMEM((D_MODEL, PEER_TOKENS), F32),
                        pltpu.VMEM((PEER_EXPERTS, PEER_TOKENS), F32),
                        pltpu.VMEM((PEER_EXPERTS, PEER_TOKENS), F32),
                        pltpu.VMEM((PEER_EXPERTS, PEER_TOKENS), BF16),
                        pltpu.VMEM((PEER_EXPERTS, PEER_TOKENS), BF16)],
        compiler_params=_params("arbitrary"),
        name="peer_experts",
    )(u_bf, vt_bf, h2t, rank, p2, rank, p2, cnt, p1, cnt, p1, x1, mod3)


def _rope_tables(seq_len):
    rows = seq_len // GRID_W
    row = jnp.repeat(jnp.arange(rows, dtype=F32), GRID_W)
    col = jnp.tile(jnp.arange(GRID_W, dtype=F32), rows)
    inv = ROPE_THETA ** (-jnp.arange(AXIS_FREQS, dtype=F32) / AXIS_FREQS)
    ang_row = row[:, None] * inv
    ang_col = col[:, None] * inv
    zeros = jnp.zeros_like(ang_row)
    cos = jnp.concatenate([jnp.cos(ang_row)] * 2 + [jnp.cos(ang_col)] * 2, axis=-1)
    sin_lo = jnp.concatenate([-jnp.sin(ang_row), zeros, -jnp.sin(ang_col), zeros], axis=-1)
    sin_hi = jnp.concatenate([zeros, jnp.sin(ang_row), zeros, jnp.sin(ang_col)], axis=-1)
    return tuple(jnp.tile(t, (1, N_HEADS)) for t in (cos, sin_lo, sin_hi))


def _head_mean_matrix(width):
    idx = jnp.arange(width) // HEAD_DIM
    return jnp.where(idx[:, None] == idx[None, :], 1.0 / HEAD_DIM, 0.0).astype(BF16)


def _trunk(x3, mod3, mod_of_seq, cache, w, rope):
    b, seq, _ = x3.shape
    t = b * seq
    tiles_per_seq = seq // ROW_TILE
    q, k, v, xg = _inproj(x3.reshape(t, D_MODEL), mod3, lambda i: mod_of_seq(i // tiles_per_seq),
                          tiles_per_seq, w["norm1"], w["w_in"], w["qn"], w["kn"], w["gq"], w["gk"], rope)
    conv = _conv_module(xg, tiles_per_seq, w["conv_dw"], w["conv_dw_b"], w["conv_ln_g"], w["conv_ln_b"],
                        w["conv_pw"])
    x1, h2 = _mix(x3, mod3, mod_of_seq, q.reshape(b, seq, ATTN_WIDTH), k.reshape(b, seq, KV_WIDTH),
                  v.reshape(b, seq, KV_WIDTH), conv.reshape(b, seq, CONV_CH), cache, w["sink"],
                  w["w_out"], w["norm2"])
    h2t, rank, p2, cnt, p1 = _route(h2.reshape(t, D_MODEL), w["peer_wq"], w["peer_keys"])
    per_seq = seq // PEER_TOKENS
    if per_seq == 0:
        seqs_per_block = PEER_TOKENS // seq
        row_of_block = lambda i: mod_of_seq(i * seqs_per_block)
    else:
        row_of_block = lambda i: mod_of_seq(i // per_seq)
    y = _experts(w["peer_u"], w["peer_vt"], h2t, rank, p2, cnt, p1, x1.reshape(t, D_MODEL), mod3, row_of_block)
    return y.reshape(b, seq, D_MODEL), k, v


def kernel(x_prompt, x_sample, cache_k, cache_v, c, c_ctx, norm1, norm2, w_ada, b_ada, w_in, w_out,
           q_norm, k_norm, attn_sink, conv_dw, conv_dw_b, conv_ln_g, conv_ln_b, conv_pw,
           peer_wq, peer_keys, peer_u, peer_v):
    depth = norm1.shape[0]
    batch, seq, _ = x_prompt.shape
    dec_batch, dec_seq, _ = x_sample.shape
    past = cache_k.shape[2]
    assert dec_batch + 1 <= 8 and depth >= 1
    cond8 = jnp.concatenate([c_ctx[None, :], c, jnp.zeros((8 - 1 - dec_batch, D_MODEL), F32)], axis=0)
    rope = _rope_tables(dec_seq)
    xp, xs = x_prompt, x_sample
    new_ks, new_vs = [], []
    for l in range(depth):
        w = dict(
            norm1=norm1[l][None], norm2=norm2[l][None],
            w_in=w_in[l].astype(BF16), w_out=w_out[l].astype(BF16),
            qn=jnp.tile(q_norm[l], N_HEADS)[None], kn=jnp.tile(k_norm[l], N_KV_HEADS)[None],
            gq=_head_mean_matrix(ATTN_WIDTH), gk=_head_mean_matrix(KV_WIDTH),
            sink=attn_sink[l],
            conv_dw=conv_dw[l], conv_dw_b=conv_dw_b[l][None], conv_ln_g=conv_ln_g[l][None],
            conv_ln_b=conv_ln_b[l][None], conv_pw=conv_pw[l].astype(BF16),
            peer_wq=peer_wq[l].astype(BF16), peer_keys=peer_keys[l].astype(BF16),
            peer_u=_pack_table(peer_u[l], False), peer_vt=_pack_table(peer_v[l], True),
        )
        mod3 = _modulation(cond8, w_ada[l], b_ada[l]).reshape(8, 6, D_MODEL)
        xp, k_c, v_c = _trunk(xp, mod3, lambda s: 0, None, w, None)
        new_ks.append(k_c.reshape(batch, seq, N_KV_HEADS, HEAD_DIM))
        new_vs.append(v_c.reshape(batch, seq, N_KV_HEADS, HEAD_DIM))
        cache = (cache_k[:, l].reshape(dec_batch, past, KV_WIDTH), cache_v[:, l].reshape(dec_batch, past, KV_WIDTH))
        xs, _, _ = _trunk(xs, mod3, lambda s: s + 1, cache, w, rope)
    return (xp, xs, jnp.stack(new_ks, axis=1), jnp.stack(new_vs, axis=1))
```

```python
import functools
import math

import jax
import jax.numpy as jnp
from jax import lax
from jax.experimental import pallas as pl
from jax.experimental.pallas import tpu as pltpu

F32 = jnp.float32
BF16 = jnp.bfloat16

D_MODEL = 1024
GRID_W = 64
N_HEADS = 8
N_KV_HEADS = 2
GROUP = N_HEADS // N_KV_HEADS
HEAD_DIM = 64
ATTN_WIDTH = N_HEADS * HEAD_DIM
KV_WIDTH = N_KV_HEADS * HEAD_DIM
WINDOW = 128
BLOCK = 128
ROPE_THETA = 10000.0
AXIS_DIM = HEAD_DIM // 2
AXIS_FREQS = AXIS_DIM // 2
CONV_CH = D_MODEL - ATTN_WIDTH
CONV_K = 31
CONV_PAD = CONV_K // 2
IN_WIDTH = ATTN_WIDTH + 2 * KV_WIDTH + 2 * CONV_CH
N_KEYS = 128
N_EXPERTS = N_KEYS * N_KEYS
PEER_HEADS = 8
PEER_QDIM = 256
PEER_HALF = PEER_QDIM // 2
PEER_TOPK = 16
EPS = 1e-6

LANES = 128
BF16_SUBLANES = 16
ROW_TILE = 256
MIX_TILE = 512
HALO = 16
ROUTE_TILE = 256
PEER_TOKENS = 512
PEER_EXPERTS = 1024
GATE_ROWS = 4
DOT_SPLIT = 1
VMEM_LIMIT = 48 * 1024 * 1024

NEG_INF = float("-inf")


def _params(*sem, flags=None):
    return pltpu.CompilerParams(dimension_semantics=sem, vmem_limit_bytes=VMEM_LIMIT, flags=flags)


def _full(shape):
    return pl.BlockSpec(shape, lambda *_: (0,) * len(shape))


def _mod_kernel(cond_ref, w_ref, b_ref, o_ref):
    cnd = cond_ref[...]
    act = cnd * jax.nn.sigmoid(cnd)
    o_ref[...] = jnp.dot(act, w_ref[...], precision=lax.Precision.HIGHEST,
                         preferred_element_type=F32) + b_ref[...]


def _modulation(cond8, w_ada, b_ada):
    n = w_ada.shape[1]
    nb = n // D_MODEL
    return pl.pallas_call(
        _mod_kernel,
        grid=(nb,),
        in_specs=[_full((8, D_MODEL)),
                  pl.BlockSpec((D_MODEL, D_MODEL), lambda j: (0, j)),
                  pl.BlockSpec((1, D_MODEL), lambda j: (0, j))],
        out_specs=pl.BlockSpec((8, D_MODEL), lambda j: (0, j)),
        out_shape=jax.ShapeDtypeStruct((8, n), F32),
        compiler_params=_params("arbitrary"),
        name="modulation",
    )(cond8, w_ada, b_ada.reshape(1, n))


def _group_rms(t, gmat, gain):
    ms = jnp.dot((t * t).astype(BF16), gmat, preferred_element_type=F32)
    return t * lax.rsqrt(ms + EPS) * gain


def _rope(t, cos, sin_lo, sin_hi):
    w = t.shape[1]
    return (t * cos + pltpu.roll(t, w - AXIS_FREQS, 1) * sin_lo
            + pltpu.roll(t, AXIS_FREQS, 1) * sin_hi)


def _inproj_kernel(latent, x_ref, mod_ref, n1_ref, win_ref, qn_ref, kn_ref, gq_ref, gk_ref, *rest):
    if latent:
        cos_ref, slo_ref, shi_ref, q_ref, k_ref, v_ref, xg_ref = rest
    else:
        q_ref, k_ref, v_ref, xg_ref = rest
    x = x_ref[...]
    sh1 = mod_ref[0:1, :]
    sc1 = mod_ref[1:2, :]
    h = x * lax.rsqrt(jnp.mean(x * x, axis=-1, keepdims=True) + EPS) * n1_ref[...]
    h = h * (1.0 + sc1) + sh1
    z = jnp.dot(h.astype(BF16), win_ref[...], preferred_element_type=F32)
    q = _group_rms(z[:, :ATTN_WIDTH], gq_ref[...], qn_ref[...])
    k = _group_rms(z[:, ATTN_WIDTH:ATTN_WIDTH + KV_WIDTH], gk_ref[...], kn_ref[...])
    if latent:
        q = _rope(q, cos_ref[...], slo_ref[...], shi_ref[...])
        k = _rope(k, cos_ref[:, :KV_WIDTH], slo_ref[:, :KV_WIDTH], shi_ref[:, :KV_WIDTH])
    q_ref[...] = (q * (1.0 / math.sqrt(HEAD_DIM))).astype(BF16)
    k_ref[...] = k
    v_ref[...] = z[:, ATTN_WIDTH + KV_WIDTH:ATTN_WIDTH + 2 * KV_WIDTH]
    u = z[:, ATTN_WIDTH + 2 * KV_WIDTH:]
    xg_ref[...] = u[:, :CONV_CH] * jax.nn.sigmoid(u[:, CONV_CH:])


def _inproj(x2d, mod3, mod_row, tiles_per_seq, norm1, w_in, qn, kn, gq, gk, rope):
    t = x2d.shape[0]
    latent = rope is not None
    row = lambda i: (i, 0)
    in_specs = [pl.BlockSpec((ROW_TILE, D_MODEL), row),
                pl.BlockSpec((None, 6, D_MODEL), lambda i: (mod_row(i), 0, 0)),
                _full((1, D_MODEL)), _full((D_MODEL, IN_WIDTH)),
                _full((1, ATTN_WIDTH)), _full((1, KV_WIDTH)),
                _full((ATTN_WIDTH, ATTN_WIDTH)), _full((KV_WIDTH, KV_WIDTH))]
    args = [x2d, mod3, norm1, w_in, qn, kn, gq, gk]
    if latent:
        pos = lambda i: (i % tiles_per_seq, 0)
        in_specs += [pl.BlockSpec((ROW_TILE, ATTN_WIDTH), pos)] * 3
        args += list(rope)
    return pl.pallas_call(
        functools.partial(_inproj_kernel, latent),
        grid=(t // ROW_TILE,),
        in_specs=in_specs,
        out_specs=[pl.BlockSpec((ROW_TILE, ATTN_WIDTH), row),
                   pl.BlockSpec((ROW_TILE, KV_WIDTH), row),
                   pl.BlockSpec((ROW_TILE, KV_WIDTH), row),
                   pl.BlockSpec((ROW_TILE, CONV_CH), row)],
        out_shape=[jax.ShapeDtypeStruct((t, ATTN_WIDTH), BF16),
                   jax.ShapeDtypeStruct((t, KV_WIDTH), F32),
                   jax.ShapeDtypeStruct((t, KV_WIDTH), F32),
                   jax.ShapeDtypeStruct((t, CONV_CH), F32)],
        compiler_params=_params("arbitrary"),
        name="inproj_latent" if latent else "inproj_context",
    )(*args)


def _conv_kernel(tiles_per_seq, prev_ref, cur_ref, nxt_ref, dw_ref, db_ref, lg_ref, lb_ref, pw_ref,
                 o_ref, xs_ref, act_ref):
    i = pl.program_id(0)
    pos = i % tiles_per_seq
    zero = jnp.zeros((HALO, CONV_CH), F32)
    xs_ref[0, 0:HALO, :] = jnp.where(pos > 0, prev_ref[...], zero)
    xs_ref[0, HALO:HALO + ROW_TILE, :] = cur_ref[...]
    xs_ref[0, HALO + ROW_TILE:2 * HALO + ROW_TILE, :] = jnp.where(pos < tiles_per_seq - 1, nxt_ref[...], zero)
    span = ROW_TILE + 2 * HALO - 8
    for r in range(1, 8):
        xs_ref[r, 0:span, :] = xs_ref[0, r:r + span, :]
    chunk = 32

    def body(c, carry):
        base = pl.multiple_of(c * chunk, chunk)
        acc = jnp.zeros((chunk, CONV_CH), F32) + db_ref[...]
        for k in range(CONV_K):
            kk = k + HALO - CONV_PAD
            start = pl.multiple_of(base + 8 * (kk // 8), 8)
            acc = acc + xs_ref[kk % 8, pl.ds(start, chunk), :] * dw_ref[k:k + 1, :]
        act_ref[pl.ds(base, chunk), :] = acc
        return carry

    lax.fori_loop(0, ROW_TILE // chunk, body, 0)
    acc = act_ref[...]
    mu = jnp.mean(acc, axis=-1, keepdims=True)
    cen = acc - mu
    var = jnp.mean(cen * cen, axis=-1, keepdims=True)
    y = cen * lax.rsqrt(var + EPS) * lg_ref[...] + lb_ref[...]
    y = y * jax.nn.sigmoid(y)
    o_ref[...] = jnp.dot(y.astype(BF16), pw_ref[...], preferred_element_type=F32).astype(BF16)


def _conv_module(xg, tiles_per_seq, dw, db, lg, lb, pw):
    t = xg.shape[0]
    nt = t // ROW_TILE
    per = ROW_TILE // HALO
    last = t // HALO - 1
    return pl.pallas_call(
        functools.partial(_conv_kernel, tiles_per_seq),
        grid=(nt,),
        in_specs=[pl.BlockSpec((HALO, CONV_CH), lambda i: (jnp.maximum(i * per - 1, 0), 0)),
                  pl.BlockSpec((ROW_TILE, CONV_CH), lambda i: (i, 0)),
                  pl.BlockSpec((HALO, CONV_CH), lambda i: (jnp.minimum((i + 1) * per, last), 0)),
                  _full((CONV_K, CONV_CH)), _full((1, CONV_CH)), _full((1, CONV_CH)), _full((1, CONV_CH)),
                  _full((CONV_CH, CONV_CH))],
        out_specs=pl.BlockSpec((ROW_TILE, CONV_CH), lambda i: (i, 0)),
        out_shape=jax.ShapeDtypeStruct((t, CONV_CH), BF16),
        scratch_shapes=[pltpu.VMEM((8, ROW_TILE + 2 * HALO, CONV_CH), F32),
                        pltpu.VMEM((ROW_TILE, CONV_CH), F32)],
        compiler_params=_params("arbitrary"),
        name="conv_module",
    )(xg, xg, xg, dw, db, lg, lb, pw)


def _attend(qg, sink_col, parts):
    m = sink_col
    scores = []
    for keys, _, mask in parts:
        s = lax.dot_general(qg, keys, (((1,), (1,)), ((), ())), preferred_element_type=F32)
        if mask is not None:
            s = jnp.where(mask, s, NEG_INF)
        m = jnp.maximum(m, jnp.max(s, axis=1, keepdims=True))
        scores.append(s)
    denom = jnp.exp(sink_col - m)
    out = None
    for s, (_, vals, _) in zip(scores, parts):
        p = jnp.exp(s - m)
        denom = denom + jnp.sum(p, axis=1, keepdims=True)
        o = jnp.dot(p.astype(BF16), vals, preferred_element_type=F32)
        out = o if out is None else out + o
    return out / denom


def _mix_kernel(latent, seq, x_ref, mod_ref, q_ref, k_ref, v_ref, cv_ref, *rest):
    if latent:
        ck_ref, cvv_ref, sink_ref, wout_ref, n2_ref, x1_ref, h2_ref, attn_ref = rest
    else:
        sink_ref, wout_ref, n2_ref, x1_ref, h2_ref, attn_ref = rest
    width = min(seq, ROW_TILE + 2 * WINDOW)
    mix_tile = x_ref.shape[0]
    steps_per_seq = max(seq // mix_tile, 1)
    for sub in range(mix_tile // ROW_TILE):
        rows = slice(sub * ROW_TILE, (sub + 1) * ROW_TILE)
        if latent:
            q0 = (pl.program_id(0) % steps_per_seq) * mix_tile + sub * ROW_TILE
            start = pl.multiple_of(jnp.clip(q0 - WINDOW, 0, seq - width), BLOCK)
            qpos = q0 + (lax.broadcasted_iota(jnp.int32, (GROUP * ROW_TILE, width), 0) & (ROW_TILE - 1))
            kpos = start + lax.broadcasted_iota(jnp.int32, (GROUP * ROW_TILE, width), 1)
            mask = jnp.abs(kpos - qpos) <= WINDOW
        else:
            start = sub * ROW_TILE
            mask = None
        for kv in range(N_KV_HEADS):
            lanes = slice(kv * HEAD_DIM, (kv + 1) * HEAD_DIM)
            parts = [(k_ref[pl.ds(start, width), lanes].astype(BF16),
                      v_ref[pl.ds(start, width), lanes].astype(BF16), mask)]
            if latent:
                parts.append((ck_ref[:, lanes].astype(BF16), cvv_ref[:, lanes].astype(BF16), None))
            sink_col = jnp.concatenate(
                [jnp.full((ROW_TILE, 1), sink_ref[kv * GROUP + g], F32) for g in range(GROUP)], axis=0)
            qg = jnp.concatenate(
                [q_ref[rows, (kv * GROUP + g) * HEAD_DIM:(kv * GROUP + g + 1) * HEAD_DIM]
                 for g in range(GROUP)], axis=0)
            o = _attend(qg, sink_col, parts)
            for g in range(GROUP):
                h = kv * GROUP + g
                attn_ref[rows, h * HEAD_DIM:(h + 1) * HEAD_DIM] = o[g * ROW_TILE:(g + 1) * ROW_TILE].astype(BF16)
    g1 = mod_ref[2:3, :]
    sh2 = mod_ref[3:4, :]
    sc2 = mod_ref[4:5, :]
    mixed = (jnp.dot(attn_ref[...], wout_ref[0:ATTN_WIDTH, :], preferred_element_type=F32)
             + jnp.dot(cv_ref[...], wout_ref[ATTN_WIDTH:, :], preferred_element_type=F32))
    x1 = x_ref[...] + g1 * mixed
    x1_ref[...] = x1
    h2 = x1 * lax.rsqrt(jnp.mean(x1 * x1, axis=-1, keepdims=True) + EPS) * n2_ref[...]
    h2_ref[...] = (h2 * (1.0 + sc2) + sh2).astype(BF16)


def _mix(x3, mod3, mod_row, q3, k3, v3, conv3, cache, sink, w_out, norm2):
    b, seq, _ = x3.shape
    latent = cache is not None
    t = b * seq
    if latent:
        mix_tile = ROW_TILE
        assert seq % mix_tile == 0
        steps_per_seq = seq // mix_tile
        seq_of = lambda i: i // steps_per_seq
        kv_spec = pl.BlockSpec((seq, KV_WIDTH), lambda i: (seq_of(i), 0))
    else:
        mix_tile = MIX_TILE
        assert seq == ROW_TILE and t % mix_tile == 0
        seqs_per_step = mix_tile // seq
        seq_of = lambda i: i * seqs_per_step
        kv_spec = pl.BlockSpec((mix_tile, KV_WIDTH), lambda i: (i, 0))
    tile = lambda w: pl.BlockSpec((mix_tile, w), lambda i: (i, 0))
    flat = lambda a: a.reshape(t, a.shape[-1])
    in_specs = [tile(D_MODEL),
                pl.BlockSpec((None, 6, D_MODEL), lambda i: (mod_row(seq_of(i)), 0, 0)),
                tile(ATTN_WIDTH), kv_spec, kv_spec, tile(CONV_CH)]
    args = [flat(x3), mod3, flat(q3), flat(k3), flat(v3), flat(conv3)]
    if latent:
        past = cache[0].shape[1]
        in_specs += [pl.BlockSpec((None, past, KV_WIDTH), lambda i: (seq_of(i), 0, 0))] * 2
        args += list(cache)
    in_specs += [pl.BlockSpec(memory_space=pltpu.SMEM), _full((D_MODEL, D_MODEL)), _full((1, D_MODEL))]
    args += [sink, w_out, norm2]
    x1, h2 = pl.pallas_call(
        functools.partial(_mix_kernel, latent, seq),
        grid=(t // mix_tile,),
        in_specs=in_specs,
        out_specs=[tile(D_MODEL), tile(D_MODEL)],
        out_shape=[jax.ShapeDtypeStruct((t, D_MODEL), F32),
                   jax.ShapeDtypeStruct((t, D_MODEL), BF16)],
        scratch_shapes=[pltpu.VMEM((mix_tile, ATTN_WIDTH), BF16)],
        compiler_params=_params("arbitrary"),
        name="mix_latent" if latent else "mix_context",
    )(*args)
    return x1.reshape(b, seq, D_MODEL), h2.reshape(b, seq, D_MODEL)


def _top_desc(x, count):
    slot = lax.broadcasted_iota(jnp.int32, (count, x.shape[1]), 0)
    out = jnp.zeros((count, x.shape[1]), F32)
    for k in range(count):
        m = jnp.max(x, axis=0, keepdims=True)
        out = jnp.where(slot == k, m, out)
        x = jnp.where(x == m, NEG_INF, x)
    return out


def _merge_exchange_pairs(n):
    pairs, p = [], 1
    while p < n:
        k = p
        while k >= 1:
            for j in range(k % p, n - k, 2 * k):
                for i in range(min(k, n - j - k)):
                    if (i + j) // (2 * p) == (i + j + k) // (2 * p):
                        pairs.append((i + j, i + j + k))
            k //= 2
        p *= 2
    return pairs


def _top_desc_sorted(x, count):
    lists = [x[8 * g:8 * (g + 1)] for g in range(count)]
    for a, b in _merge_exchange_pairs(count):
        lists[a], lists[b] = jnp.maximum(lists[a], lists[b]), jnp.minimum(lists[a], lists[b])
    slot = lax.broadcasted_iota(jnp.int32, (count, x.shape[1]), 0)
    out = jnp.zeros((count, x.shape[1]), F32)
    for k in range(count):
        m = jnp.max(lists[0], axis=0, keepdims=True)
        out = jnp.where(slot == k, m, out)
        hit = lists[0] == m
        for lvl in range(count - 1 - k):
            lists[lvl] = jnp.where(hit, lists[lvl + 1], lists[lvl])
    return out


def _route_kernel(h2_ref, wq_ref, keys_ref, h2t_ref, rank_ref, p2_ref, cnt_ref, p1_ref, qp_ref):
    h2 = h2_ref[...]
    h2t_ref[...] = pltpu.bitcast(h2.astype(F32).T.astype(BF16), jnp.uint32)
    qp_ref[...] = jnp.dot(h2, wq_ref[...], preferred_element_type=F32).astype(BF16)
    nt = (((1,), (1,)), ((), ()))

    def head(h, carry):
        col = pl.multiple_of(h * PEER_QDIM, PEER_QDIM)
        s1 = lax.dot_general(keys_ref[h, 0], qp_ref[:, pl.ds(col, PEER_HALF)], nt,
                             preferred_element_type=F32)
        s2 = lax.dot_general(keys_ref[h, 1], qp_ref[:, pl.ds(col + PEER_HALF, PEER_HALF)], nt,
                             preferred_element_type=F32)
        v1 = _top_desc_sorted(s1, PEER_TOPK)
        v2 = _top_desc_sorted(s2, PEER_TOPK)
        slot = lax.broadcasted_iota(jnp.int32, (8, ROUTE_TILE), 0)
        cand = [v1[0:1] + v2]
        for j in range(1, 8):
            cand.append(jnp.where(slot < PEER_TOPK // (j + 1), v1[j:j + 1] + v2[0:8], NEG_INF))
        cand.append(v1[8:PEER_TOPK] + v2[0:1])
        best = _top_desc(jnp.concatenate(cand, axis=0), PEER_TOPK)
        tau = best[PEER_TOPK - 1:PEER_TOPK]
        z = jnp.sum(jnp.exp(best - best[0:1]), axis=0, keepdims=True)
        rank = jnp.zeros_like(s2)
        cnt_top = jnp.zeros_like(v1)
        for k in range(PEER_TOPK):
            vk = v2[k:k + 1]
            rank = jnp.where(vk > s2, k + 1.0, rank)
            cnt_top = jnp.where(v1 + vk >= tau, k + 1.0, cnt_top)
        cnt = jnp.zeros_like(s1)
        for j in range(PEER_TOPK):
            cnt = jnp.where(s1 == v1[j:j + 1], cnt_top[j:j + 1], cnt)
        steps = N_EXPERTS // PEER_EXPERTS
        cnt_ref[:, h] = cnt.reshape(steps, N_KEYS // steps, ROUTE_TILE)
        p1_ref[:, h] = jnp.exp(s1 - v1[0:1]).reshape(steps, N_KEYS // steps, ROUTE_TILE)
        rank_ref[h] = pltpu.bitcast(rank.astype(BF16), jnp.uint32)
        p2_ref[h] = pltpu.bitcast((jnp.exp(s2 - v2[0:1]) * (0.5 / z)).astype(BF16), jnp.uint32)
        return carry

    lax.fori_loop(0, PEER_HEADS, head, 0)


def _route(h2, wq, keys):
    t = h2.shape[0]
    tok4 = pl.BlockSpec((PEER_HEADS, N_KEYS // 2, ROUTE_TILE), lambda i: (0, 0, i))
    steps = N_EXPERTS // PEER_EXPERTS
    tok3 = pl.BlockSpec((steps, PEER_HEADS, N_KEYS // steps, ROUTE_TILE), lambda i: (0, 0, 0, i))
    return pl.pallas_call(
        _route_kernel,
        grid=(t // ROUTE_TILE,),
        in_specs=[pl.BlockSpec((ROUTE_TILE, D_MODEL), lambda i: (i, 0)),
                  _full((D_MODEL, PEER_HEADS * PEER_QDIM)),
                  _full((PEER_HEADS, 2, N_KEYS, PEER_HALF))],
        out_specs=[pl.BlockSpec((D_MODEL // 2, ROUTE_TILE), lambda i: (0, i)), tok4, tok4, tok3, tok3],
        out_shape=[jax.ShapeDtypeStruct((D_MODEL // 2, t), jnp.uint32),
                   jax.ShapeDtypeStruct((PEER_HEADS, N_KEYS // 2, t), jnp.uint32),
                   jax.ShapeDtypeStruct((PEER_HEADS, N_KEYS // 2, t), jnp.uint32),
                   jax.ShapeDtypeStruct((steps, PEER_HEADS, N_KEYS // steps, t), F32),
                   jax.ShapeDtypeStruct((steps, PEER_HEADS, N_KEYS // steps, t), F32)],
        scratch_shapes=[pltpu.VMEM((ROUTE_TILE, PEER_HEADS * PEER_QDIM), BF16)],
        compiler_params=_params("arbitrary"),
        name="peer_route",
    )(h2, wq, keys)


def _pack_kernel(transpose, x_ref, o_ref):
    x = x_ref[...]
    if transpose:
        x = x.T
    o_ref[...] = pltpu.bitcast(x.astype(BF16), jnp.uint32)


def _pack_table(table, transpose):
    rows, cols = table.shape
    blk = 512
    if transpose:
        out_spec = pl.BlockSpec((cols // 2, blk), lambda i: (0, i))
        out_shape = jax.ShapeDtypeStruct((cols // 2, rows), jnp.uint32)
    else:
        out_spec = pl.BlockSpec((blk // 2, cols), lambda i: (i, 0))
        out_shape = jax.ShapeDtypeStruct((rows // 2, cols), jnp.uint32)
    return pl.pallas_call(
        functools.partial(_pack_kernel, transpose),
        grid=(rows // blk,),
        in_specs=[pl.BlockSpec((blk, cols), lambda i: (i, 0))],
        out_specs=out_spec,
        out_shape=out_shape,
        compiler_params=_params("arbitrary"),
        name="pack_table_t" if transpose else "pack_table",
    )(table)


def _gate_rows(cs, act_ref, wt_ref, rank_ref, p2_ref, cnt_ref, p1_ref):
    tiles = (N_KEYS // BF16_SUBLANES, BF16_SUBLANES, LANES)
    for lt in range(PEER_TOKENS // LANES):
        lanes = slice(lt * LANES, (lt + 1) * LANES)
        gates = [jnp.zeros(tiles, BF16) for _ in cs]
        for h in range(PEER_HEADS):
            rank = pltpu.bitcast(rank_ref[h, :, lanes], BF16).reshape(tiles)
            p2 = pltpu.bitcast(p2_ref[h, :, lanes], BF16).reshape(tiles)
            for n, c in enumerate(cs):
                cnt = jnp.broadcast_to(cnt_ref[h, c:c + 1, lanes], (BF16_SUBLANES, LANES)).astype(BF16)
                p1 = jnp.broadcast_to(p1_ref[h, c:c + 1, lanes], (BF16_SUBLANES, LANES)).astype(BF16)
                gates[n] = gates[n] + jnp.where(rank < cnt[None], p2, jnp.zeros((), BF16)) * p1[None]
        for n, c in enumerate(cs):
            rows = slice(c * N_KEYS, (c + 1) * N_KEYS)
            a = act_ref[rows, lanes]
            gl = a + a * lax.erf(a * (1.0 / math.sqrt(2.0)))
            wt_ref[rows, lanes] = gl.astype(BF16) * gates[n].reshape(N_KEYS, LANES)


def _expert_kernel(pairs_per_block, u_ref, vt_ref, h2t_ref, rank_a, p2_a, rank_b, p2_b, cnt_a, p1_a,
                   cnt_b, p1_b, x1_ref, mod_ref, y_ref, acc_ref, act_a, act_b, wt_a, wt_b):
    g = pl.program_id(0)

    @pl.when(g == 0)
    def _():
        acc_ref[...] = jnp.zeros_like(acc_ref)
        act_b[...] = jnp.zeros_like(act_b)
        wt_a[...] = jnp.zeros_like(wt_a)

    key_rows = PEER_EXPERTS // N_KEYS

    def half_step(half, wt_old, act_old, wt_new, act_new, rank_ref, p2_ref, cnt_ref, p1_ref):
        regions = 2 * DOT_SPLIT
        gate_rows = key_rows // regions
        for idx in range(regions):
            @pl.when(g >= -(half * regions + idx))
            def _(idx=idx):
                for c in range(idx * gate_rows, (idx + 1) * gate_rows, GATE_ROWS):
                    _gate_rows(range(c, c + GATE_ROWS), act_old, wt_new, rank_ref, p2_ref, cnt_ref, p1_ref)
                part = idx % DOT_SPLIT
                if idx < DOT_SPLIT:
                    m = D_MODEL // DOT_SPLIT
                    vt = pltpu.bitcast(vt_ref[part * m // 2:(part + 1) * m // 2,
                                              half * PEER_EXPERTS:(half + 1) * PEER_EXPERTS], BF16)
                    acc_ref[part * m:(part + 1) * m, :] += jnp.dot(vt, wt_old[...], preferred_element_type=F32)
                else:
                    m = PEER_EXPERTS // DOT_SPLIT
                    lo = (half * PEER_EXPERTS + part * m) // 2
                    u = pltpu.bitcast(u_ref[lo:lo + m // 2, :], BF16)
                    act_new[part * m:(part + 1) * m, :] = jnp.dot(u, pltpu.bitcast(h2t_ref[...], BF16),
                                                                  preferred_element_type=F32)

    half_step(0, wt_a, act_b, wt_b, act_a, rank_a, p2_a, cnt_a, p1_a)
    half_step(1, wt_b, act_a, wt_a, act_b, rank_b, p2_b, cnt_b, p1_b)

    @pl.when((g > 0) & (g % pairs_per_block == 0))
    def _():
        g2 = mod_ref[5:6, :]
        y_ref[...] = x1_ref[...] + g2 * acc_ref[...].T
        acc_ref[...] = jnp.zeros_like(acc_ref)


def _experts(u_bf, vt_bf, h2t, rank, p2, cnt, p1, x1, mod3, mod_row):
    t = x1.shape[0]
    ppb = N_EXPERTS // (2 * PEER_EXPERTS)
    steps = (t // PEER_TOKENS) * ppb + 1
    cur = lambda g: jnp.minimum(g, steps - 2)
    prv = lambda g: jnp.maximum(g - 1, 0)
    packed = lambda f: pl.BlockSpec((PEER_HEADS, N_KEYS // 2, PEER_TOKENS), lambda g: (0, 0, f(g) // ppb))
    rows = lambda f, half: pl.BlockSpec((None, PEER_HEADS, PEER_EXPERTS // N_KEYS, PEER_TOKENS),
                                        lambda g: (2 * (f(g) % ppb) + half, 0, 0, f(g) // ppb))
    return pl.pallas_call(
        functools.partial(_expert_kernel, ppb),
        grid=(steps,),
        in_specs=[pl.BlockSpec((PEER_EXPERTS, D_MODEL), lambda g: (cur(g) % ppb, 0)),
                  pl.BlockSpec((D_MODEL // 2, 2 * PEER_EXPERTS), lambda g: (0, prv(g) % ppb)),
                  pl.BlockSpec((D_MODEL // 2, PEER_TOKENS), lambda g: (0, cur(g) // ppb)),
                  packed(prv), packed(prv), packed(cur), packed(cur),
                  rows(prv, 1), rows(prv, 1), rows(cur, 0), rows(cur, 0),
                  pl.BlockSpec((PEER_TOKENS, D_MODEL), lambda g: (prv(g) // ppb, 0)),
                  pl.BlockSpec((None, 6, D_MODEL), lambda g: (mod_row(prv(g) // ppb), 0, 0))],
        out_specs=pl.BlockSpec((PEER_TOKENS, D_MODEL), lambda g: (prv(g) // ppb, 0)),
        out_shape=jax.ShapeDtypeStruct((t, D_MODEL), F32),
        scratch_shapes=[pltpu.VMEM((D_MODEL, PEER_TOKENS), F32),
                        pltpu.VMEM((PEER_EXPERTS, PEER_TOKENS), F32),
                        pltpu.VMEM((PEER_EXPERTS, PEER_TOKENS), F32),
                        pltpu.VMEM((PEER_EXPERTS, PEER_TOKENS), BF16),
                        pltpu.VMEM((PEER_EXPERTS, PEER_TOKENS), BF16)],
        compiler_params=_params("arbitrary"),
        name="peer_experts",
    )(u_bf, vt_bf, h2t, rank, p2, rank, p2, cnt, p1, cnt, p1, x1, mod3)


def _rope_tables(seq_len):
    rows = seq_len // GRID_W
    row = jnp.repeat(jnp.arange(rows, dtype=F32), GRID_W)
    col = jnp.tile(jnp.arange(GRID_W, dtype=F32), rows)
    inv = ROPE_THETA ** (-jnp.arange(AXIS_FREQS, dtype=F32) / AXIS_FREQS)
    ang_row = row[:, None] * inv
    ang_col = col[:, None] * inv
    zeros = jnp.zeros_like(ang_row)
    cos = jnp.concatenate([jnp.cos(ang_row)] * 2 + [jnp.cos(ang_col)] * 2, axis=-1)
    sin_lo = jnp.concatenate([-jnp.sin(ang_row), zeros, -jnp.sin(ang_col), zeros], axis=-1)
    sin_hi = jnp.concatenate([zeros, jnp.sin(ang_row), zeros, jnp.sin(ang_col)], axis=-1)
    return tuple(jnp.tile(t, (1, N_HEADS)) for t in (cos, sin_lo, sin_hi))


def _head_mean_matrix(width):
    idx = jnp.arange(width) // HEAD_DIM
    return jnp.where(idx[:, None] == idx[None, :], 1.0 / HEAD_DIM, 0.0).astype(BF16)


def _trunk(x3, mod3, mod_of_seq, cache, w, rope):
    b, seq, _ = x3.shape
    t = b * seq
    tiles_per_seq = seq // ROW_TILE
    q, k, v, xg = _inproj(x3.reshape(t, D_MODEL), mod3, lambda i: mod_of_seq(i // tiles_per_seq),
                          tiles_per_seq, w["norm1"], w["w_in"], w["qn"], w["kn"], w["gq"], w["gk"], rope)
    conv = _conv_module(xg, tiles_per_seq, w["conv_dw"], w["conv_dw_b"], w["conv_ln_g"], w["conv_ln_b"],
                        w["conv_pw"])
    x1, h2 = _mix(x3, mod3, mod_of_seq, q.reshape(b, seq, ATTN_WIDTH), k.reshape(b, seq, KV_WIDTH),
                  v.reshape(b, seq, KV_WIDTH), conv.reshape(b, seq, CONV_CH), cache, w["sink"],
                  w["w_out"], w["norm2"])
    h2t, rank, p2, cnt, p1 = _route(h2.reshape(t, D_MODEL), w["peer_wq"], w["peer_keys"])
    per_seq = seq // PEER_TOKENS
    if per_seq == 0:
        seqs_per_block = PEER_TOKENS // seq
        row_of_block = lambda i: mod_of_seq(i * seqs_per_block)
    else:
        row_of_block = lambda i: mod_of_seq(i // per_seq)
    y = _experts(w["peer_u"], w["peer_vt"], h2t, rank, p2, cnt, p1, x1.reshape(t, D_MODEL), mod3, row_of_block)
    return y.reshape(b, seq, D_MODEL), k, v


def kernel(x_prompt, x_sample, cache_k, cache_v, c, c_ctx, norm1, norm2, w_ada, b_ada, w_in, w_out,
           q_norm, k_norm, attn_sink, conv_dw, conv_dw_b, conv_ln_g, conv_ln_b, conv_pw,
           peer_wq, peer_keys, peer_u, peer_v):
    depth = norm1.shape[0]
    batch, seq, _ = x_prompt.shape
    dec_batch, dec_seq, _ = x_sample.shape
    past = cache_k.shape[2]
    assert dec_batch + 1 <= 8 and depth >= 1
    cond8 = jnp.concatenate([c_ctx[None, :], c, jnp.zeros((8 - 1 - dec_batch, D_MODEL), F32)], axis=0)
    rope = _rope_tables(dec_seq)
    xp, xs = x_prompt, x_sample
    new_ks, new_vs = [], []
    for l in range(depth):
        w = dict(
            norm1=norm1[l][None], norm2=norm2[l][None],
            w_in=w_in[l].astype(BF16), w_out=w_out[l].astype(BF16),
            qn=jnp.tile(q_norm[l], N_HEADS)[None], kn=jnp.tile(k_norm[l], N_KV_HEADS)[None],
            gq=_head_mean_matrix(ATTN_WIDTH), gk=_head_mean_matrix(KV_WIDTH),
            sink=attn_sink[l],
            conv_dw=conv_dw[l], conv_dw_b=conv_dw_b[l][None], conv_ln_g=conv_ln_g[l][None],
            conv_ln_b=conv_ln_b[l][None], conv_pw=conv_pw[l].astype(BF16),
            peer_wq=peer_wq[l].astype(BF16), peer_keys=peer_keys[l].astype(BF16),
            peer_u=_pack_table(peer_u[l], False), peer_vt=_pack_table(peer_v[l], True),
        )
        mod3 = _modulation(cond8, w_ada[l], b_ada[l]).reshape(8, 6, D_MODEL)
        xp, k_c, v_c = _trunk(xp, mod3, lambda s: 0, None, w, None)
        new_ks.append(k_c.reshape(batch, seq, N_KV_HEADS, HEAD_DIM))
        new_vs.append(v_c.reshape(batch, seq, N_KV_HEADS, HEAD_DIM))
        cache = (cache_k[:, l].reshape(dec_batch, past, KV_WIDTH), cache_v[:, l].reshape(dec_batch, past, KV_WIDTH))
        xs, _, _ = _trunk(xs, mod3, lambda s: s + 1, cache, w, rope)
    return (xp, xs, jnp.stack(new_ks, axis=1), jnp.stack(new_vs, axis=1))
```

```python
import functools
import math

import jax
import jax.numpy as jnp
from jax import lax
from jax.experimental import pallas as pl
from jax.experimental.pallas import tpu as pltpu

F32 = jnp.float32
BF16 = jnp.bfloat16

D_MODEL = 1024
GRID_W = 64
N_HEADS = 8
N_KV_HEADS = 2
GROUP = N_HEADS // N_KV_HEADS
HEAD_DIM = 64
ATTN_WIDTH = N_HEADS * HEAD_DIM
KV_WIDTH = N_KV_HEADS * HEAD_DIM
WINDOW = 128
BLOCK = 128
ROPE_THETA = 10000.0
AXIS_DIM = HEAD_DIM // 2
AXIS_FREQS = AXIS_DIM // 2
CONV_CH = D_MODEL - ATTN_WIDTH
CONV_K = 31
CONV_PAD = CONV_K // 2
IN_WIDTH = ATTN_WIDTH + 2 * KV_WIDTH + 2 * CONV_CH
N_KEYS = 128
N_EXPERTS = N_KEYS * N_KEYS
PEER_HEADS = 8
PEER_QDIM = 256
PEER_HALF = PEER_QDIM // 2
PEER_TOPK = 16
EPS = 1e-6

LANES = 128
BF16_SUBLANES = 16
ROW_TILE = 256
MIX_TILE = 512
HALO = 16
ROUTE_TILE = 512
PEER_TOKENS = 512
PEER_EXPERTS = 1024
GATE_ROWS = 4
DOT_SPLIT = 1
VMEM_LIMIT = 48 * 1024 * 1024

NEG_INF = float("-inf")


def _params(*sem, flags=None):
    return pltpu.CompilerParams(dimension_semantics=sem, vmem_limit_bytes=VMEM_LIMIT, flags=flags)


def _full(shape):
    return pl.BlockSpec(shape, lambda *_: (0,) * len(shape))


def _mod_kernel(cond_ref, w_ref, b_ref, o_ref):
    cnd = cond_ref[...]
    act = cnd * jax.nn.sigmoid(cnd)
    o_ref[...] = jnp.dot(act, w_ref[...], precision=lax.Precision.HIGHEST,
                         preferred_element_type=F32) + b_ref[...]


def _modulation(cond8, w_ada, b_ada):
    n = w_ada.shape[1]
    nb = n // D_MODEL
    return pl.pallas_call(
        _mod_kernel,
        grid=(nb,),
        in_specs=[_full((8, D_MODEL)),
                  pl.BlockSpec((D_MODEL, D_MODEL), lambda j: (0, j)),
                  pl.BlockSpec((1, D_MODEL), lambda j: (0, j))],
        out_specs=pl.BlockSpec((8, D_MODEL), lambda j: (0, j)),
        out_shape=jax.ShapeDtypeStruct((8, n), F32),
        compiler_params=_params("arbitrary"),
        name="modulation",
    )(cond8, w_ada, b_ada.reshape(1, n))


def _group_rms(t, gmat, gain):
    ms = jnp.dot((t * t).astype(BF16), gmat, preferred_element_type=F32)
    return t * lax.rsqrt(ms + EPS) * gain


def _rope(t, cos, sin_lo, sin_hi):
    w = t.shape[1]
    return (t * cos + pltpu.roll(t, w - AXIS_FREQS, 1) * sin_lo
            + pltpu.roll(t, AXIS_FREQS, 1) * sin_hi)


def _inproj_kernel(latent, x_ref, mod_ref, n1_ref, win_ref, qn_ref, kn_ref, gq_ref, gk_ref, *rest):
    if latent:
        cos_ref, slo_ref, shi_ref, q_ref, k_ref, v_ref, xg_ref = rest
    else:
        q_ref, k_ref, v_ref, xg_ref = rest
    x = x_ref[...]
    sh1 = mod_ref[0:1, :]
    sc1 = mod_ref[1:2, :]
    h = x * lax.rsqrt(jnp.mean(x * x, axis=-1, keepdims=True) + EPS) * n1_ref[...]
    h = h * (1.0 + sc1) + sh1
    z = jnp.dot(h.astype(BF16), win_ref[...], preferred_element_type=F32)
    q = _group_rms(z[:, :ATTN_WIDTH], gq_ref[...], qn_ref[...])
    k = _group_rms(z[:, ATTN_WIDTH:ATTN_WIDTH + KV_WIDTH], gk_ref[...], kn_ref[...])
    if latent:
        q = _rope(q, cos_ref[...], slo_ref[...], shi_ref[...])
        k = _rope(k, cos_ref[:, :KV_WIDTH], slo_ref[:, :KV_WIDTH], shi_ref[:, :KV_WIDTH])
    q_ref[...] = (q * (1.0 / math.sqrt(HEAD_DIM))).astype(BF16)
    k_ref[...] = k
    v_ref[...] = z[:, ATTN_WIDTH + KV_WIDTH:ATTN_WIDTH + 2 * KV_WIDTH]
    u = z[:, ATTN_WIDTH + 2 * KV_WIDTH:]
    xg_ref[...] = u[:, :CONV_CH] * jax.nn.sigmoid(u[:, CONV_CH:])


def _inproj(x2d, mod3, mod_row, tiles_per_seq, norm1, w_in, qn, kn, gq, gk, rope):
    t = x2d.shape[0]
    latent = rope is not None
    row = lambda i: (i, 0)
    in_specs = [pl.BlockSpec((ROW_TILE, D_MODEL), row),
                pl.BlockSpec((None, 6, D_MODEL), lambda i: (mod_row(i), 0, 0)),
                _full((1, D_MODEL)), _full((D_MODEL, IN_WIDTH)),
                _full((1, ATTN_WIDTH)), _full((1, KV_WIDTH)),
                _full((ATTN_WIDTH, ATTN_WIDTH)), _full((KV_WIDTH, KV_WIDTH))]
    args = [x2d, mod3, norm1, w_in, qn, kn, gq, gk]
    if latent:
        pos = lambda i: (i % tiles_per_seq, 0)
        in_specs += [pl.BlockSpec((ROW_TILE, ATTN_WIDTH), pos)] * 3
        args += list(rope)
    return pl.pallas_call(
        functools.partial(_inproj_kernel, latent),
        grid=(t // ROW_TILE,),
        in_specs=in_specs,
        out_specs=[pl.BlockSpec((ROW_TILE, ATTN_WIDTH), row),
                   pl.BlockSpec((ROW_TILE, KV_WIDTH), row),
                   pl.BlockSpec((ROW_TILE, KV_WIDTH), row),
                   pl.BlockSpec((ROW_TILE, CONV_CH), row)],
        out_shape=[jax.ShapeDtypeStruct((t, ATTN_WIDTH), BF16),
                   jax.ShapeDtypeStruct((t, KV_WIDTH), F32),
                   jax.ShapeDtypeStruct((t, KV_WIDTH), F32),
                   jax.ShapeDtypeStruct((t, CONV_CH), F32)],
        compiler_params=_params("arbitrary"),
        name="inproj_latent" if latent else "inproj_context",
    )(*args)


def _conv_kernel(tiles_per_seq, prev_ref, cur_ref, nxt_ref, dw_ref, db_ref, lg_ref, lb_ref, pw_ref,
                 o_ref, xs_ref, act_ref):
    i = pl.program_id(0)
    pos = i % tiles_per_seq
    zero = jnp.zeros((HALO, CONV_CH), F32)
    xs_ref[0, 0:HALO, :] = jnp.where(pos > 0, prev_ref[...], zero)
    xs_ref[0, HALO:HALO + ROW_TILE, :] = cur_ref[...]
    xs_ref[0, HALO + ROW_TILE:2 * HALO + ROW_TILE, :] = jnp.where(pos < tiles_per_seq - 1, nxt_ref[...], zero)
    span = ROW_TILE + 2 * HALO - 8
    for r in range(1, 8):
        xs_ref[r, 0:span, :] = xs_ref[0, r:r + span, :]
    chunk = 32

    def body(c, carry):
        base = pl.multiple_of(c * chunk, chunk)
        acc = jnp.zeros((chunk, CONV_CH), F32) + db_ref[...]
        for k in range(CONV_K):
            kk = k + HALO - CONV_PAD
            start = pl.multiple_of(base + 8 * (kk // 8), 8)
            acc = acc + xs_ref[kk % 8, pl.ds(start, chunk), :] * dw_ref[k:k + 1, :]
        act_ref[pl.ds(base, chunk), :] = acc
        return carry

    lax.fori_loop(0, ROW_TILE // chunk, body, 0)
    acc = act_ref[...]
    mu = jnp.mean(acc, axis=-1, keepdims=True)
    cen = acc - mu
    var = jnp.mean(cen * cen, axis=-1, keepdims=True)
    y = cen * lax.rsqrt(var + EPS) * lg_ref[...] + lb_ref[...]
    y = y * jax.nn.sigmoid(y)
    o_ref[...] = jnp.dot(y.astype(BF16), pw_ref[...], preferred_element_type=F32).astype(BF16)


def _conv_module(xg, tiles_per_seq, dw, db, lg, lb, pw):
    t = xg.shape[0]
    nt = t // ROW_TILE
    per = ROW_TILE // HALO
    last = t // HALO - 1
    return pl.pallas_call(
        functools.partial(_conv_kernel, tiles_per_seq),
        grid=(nt,),
        in_specs=[pl.BlockSpec((HALO, CONV_CH), lambda i: (jnp.maximum(i * per - 1, 0), 0)),
                  pl.BlockSpec((ROW_TILE, CONV_CH), lambda i: (i, 0)),
                  pl.BlockSpec((HALO, CONV_CH), lambda i: (jnp.minimum((i + 1) * per, last), 0)),
                  _full((CONV_K, CONV_CH)), _full((1, CONV_CH)), _full((1, CONV_CH)), _full((1, CONV_CH)),
                  _full((CONV_CH, CONV_CH))],
        out_specs=pl.BlockSpec((ROW_TILE, CONV_CH), lambda i: (i, 0)),
        out_shape=jax.ShapeDtypeStruct((t, CONV_CH), BF16),
        scratch_shapes=[pltpu.VMEM((8, ROW_TILE + 2 * HALO, CONV_CH), F32),
                        pltpu.VMEM((ROW_TILE, CONV_CH), F32)],
        compiler_params=_params("arbitrary"),
        name="conv_module",
    )(xg, xg, xg, dw, db, lg, lb, pw)


def _attend(qg, sink_col, parts):
    m = sink_col
    scores = []
    for keys, _, mask in parts:
        s = lax.dot_general(qg, keys, (((1,), (1,)), ((), ())), preferred_element_type=F32)
        if mask is not None:
            s = jnp.where(mask, s, NEG_INF)
        m = jnp.maximum(m, jnp.max(s, axis=1, keepdims=True))
        scores.append(s)
    denom = jnp.exp(sink_col - m)
    out = None
    for s, (_, vals, _) in zip(scores, parts):
        p = jnp.exp(s - m)
        denom = denom + jnp.sum(p, axis=1, keepdims=True)
        o = jnp.dot(p.astype(BF16), vals, preferred_element_type=F32)
        out = o if out is None else out + o
    return out / denom


def _mix_kernel(latent, seq, x_ref, mod_ref, q_ref, k_ref, v_ref, cv_ref, *rest):
    if latent:
        ck_ref, cvv_ref, sink_ref, wout_ref, n2_ref, x1_ref, h2_ref, attn_ref = rest
    else:
        sink_ref, wout_ref, n2_ref, x1_ref, h2_ref, attn_ref = rest
    width = min(seq, ROW_TILE + 2 * WINDOW)
    mix_tile = x_ref.shape[0]
    steps_per_seq = max(seq // mix_tile, 1)
    for sub in range(mix_tile // ROW_TILE):
        rows = slice(sub * ROW_TILE, (sub + 1) * ROW_TILE)
        if latent:
            q0 = (pl.program_id(0) % steps_per_seq) * mix_tile + sub * ROW_TILE
            start = pl.multiple_of(jnp.clip(q0 - WINDOW, 0, seq - width), BLOCK)
            qpos = q0 + (lax.broadcasted_iota(jnp.int32, (GROUP * ROW_TILE, width), 0) & (ROW_TILE - 1))
            kpos = start + lax.broadcasted_iota(jnp.int32, (GROUP * ROW_TILE, width), 1)
            mask = jnp.abs(kpos - qpos) <= WINDOW
        else:
            start = sub * ROW_TILE
            mask = None
        for kv in range(N_KV_HEADS):
            lanes = slice(kv * HEAD_DIM, (kv + 1) * HEAD_DIM)
            parts = [(k_ref[pl.ds(start, width), lanes].astype(BF16),
                      v_ref[pl.ds(start, width), lanes].astype(BF16), mask)]
            if latent:
                parts.append((ck_ref[:, lanes].astype(BF16), cvv_ref[:, lanes].astype(BF16), None))
            sink_col = jnp.concatenate(
                [jnp.full((ROW_TILE, 1), sink_ref[kv * GROUP + g], F32) for g in range(GROUP)], axis=0)
            qg = jnp.concatenate(
                [q_ref[rows, (kv * GROUP + g) * HEAD_DIM:(kv * GROUP + g + 1) * HEAD_DIM]
                 for g in range(GROUP)], axis=0)
            o = _attend(qg, sink_col, parts)
            for g in range(GROUP):
                h = kv * GROUP + g
                attn_ref[rows, h * HEAD_DIM:(h + 1) * HEAD_DIM] = o[g * ROW_TILE:(g + 1) * ROW_TILE].astype(BF16)
    g1 = mod_ref[2:3, :]
    sh2 = mod_ref[3:4, :]
    sc2 = mod_ref[4:5, :]
    mixed = (jnp.dot(attn_ref[...], wout_ref[0:ATTN_WIDTH, :], preferred_element_type=F32)
             + jnp.dot(cv_ref[...], wout_ref[ATTN_WIDTH:, :], preferred_element_type=F32))
    x1 = x_ref[...] + g1 * mixed
    x1_ref[...] = x1
    h2 = x1 * lax.rsqrt(jnp.mean(x1 * x1, axis=-1, keepdims=True) + EPS) * n2_ref[...]
    h2_ref[...] = (h2 * (1.0 + sc2) + sh2).astype(BF16)


def _mix(x3, mod3, mod_row, q3, k3, v3, conv3, cache, sink, w_out, norm2):
    b, seq, _ = x3.shape
    latent = cache is not None
    t = b * seq
    if latent:
        mix_tile = ROW_TILE
        assert seq % mix_tile == 0
        steps_per_seq = seq // mix_tile
        seq_of = lambda i: i // steps_per_seq
        kv_spec = pl.BlockSpec((seq, KV_WIDTH), lambda i: (seq_of(i), 0))
    else:
        mix_tile = MIX_TILE
        assert seq == ROW_TILE and t % mix_tile == 0
        seqs_per_step = mix_tile // seq
        seq_of = lambda i: i * seqs_per_step
        kv_spec = pl.BlockSpec((mix_tile, KV_WIDTH), lambda i: (i, 0))
    tile = lambda w: pl.BlockSpec((mix_tile, w), lambda i: (i, 0))
    flat = lambda a: a.reshape(t, a.shape[-1])
    in_specs = [tile(D_MODEL),
                pl.BlockSpec((None, 6, D_MODEL), lambda i: (mod_row(seq_of(i)), 0, 0)),
                tile(ATTN_WIDTH), kv_spec, kv_spec, tile(CONV_CH)]
    args = [flat(x3), mod3, flat(q3), flat(k3), flat(v3), flat(conv3)]
    if latent:
        past = cache[0].shape[1]
        in_specs += [pl.BlockSpec((None, past, KV_WIDTH), lambda i: (seq_of(i), 0, 0))] * 2
        args += list(cache)
    in_specs += [pl.BlockSpec(memory_space=pltpu.SMEM), _full((D_MODEL, D_MODEL)), _full((1, D_MODEL))]
    args += [sink, w_out, norm2]
    x1, h2 = pl.pallas_call(
        functools.partial(_mix_kernel, latent, seq),
        grid=(t // mix_tile,),
        in_specs=in_specs,
        out_specs=[tile(D_MODEL), tile(D_MODEL)],
        out_shape=[jax.ShapeDtypeStruct((t, D_MODEL), F32),
                   jax.ShapeDtypeStruct((t, D_MODEL), BF16)],
        scratch_shapes=[pltpu.VMEM((mix_tile, ATTN_WIDTH), BF16)],
        compiler_params=_params("arbitrary"),
        name="mix_latent" if latent else "mix_context",
    )(*args)
    return x1.reshape(b, seq, D_MODEL), h2.reshape(b, seq, D_MODEL)


def _merge_exchange_pairs(n):
    pairs, p = [], 1
    while p < n:
        k = p
        while k >= 1:
            for j in range(k % p, n - k, 2 * k):
                for i in range(min(k, n - j - k)):
                    if (i + j) // (2 * p) == (i + j + k) // (2 * p):
                        pairs.append((i + j, i + j + k))
            k //= 2
        p *= 2
    return pairs


def _top_desc_sorted(x, count):
    lists = [x[8 * g:8 * (g + 1)] for g in range(count)]
    for a, b in _merge_exchange_pairs(count):
        lists[a], lists[b] = jnp.maximum(lists[a], lists[b]), jnp.minimum(lists[a], lists[b])
    slot = lax.broadcasted_iota(jnp.int32, (count, x.shape[1]), 0)
    out = jnp.zeros((count, x.shape[1]), F32)
    for k in range(count):
        m = jnp.max(lists[0], axis=0, keepdims=True)
        out = jnp.where(slot == k, m, out)
        hit = lists[0] == m
        for lvl in range(count - 1 - k):
            lists[lvl] = jnp.where(hit, lists[lvl + 1], lists[lvl])
    return out


def _top_pair_sums(v1, v2, count):
    cols = v1.shape[1]
    row = lax.broadcasted_iota(jnp.int32, (8, cols), 0)
    lists = [jnp.where(row < count // (k + 1), v1[0:8] + v2[k:k + 1], NEG_INF) for k in range(count)]
    tail = v1[8:count] + v2[0:1]
    slot = lax.broadcasted_iota(jnp.int32, (count, cols), 0)
    out = jnp.zeros((count, cols), F32)
    for n in range(count):
        m = jnp.max(jnp.maximum(lists[0], tail), axis=0, keepdims=True)
        out = jnp.where(slot == n, m, out)
        hit = lists[0] == m
        for lvl in range(count - 1 - n):
            lists[lvl] = jnp.where(hit, lists[lvl + 1], lists[lvl])
        tail = jnp.where(tail == m, NEG_INF, tail)
    return out


def _route_kernel(h2_ref, wq_ref, keys_ref, h2t_ref, rank_ref, p2_ref, cnt_ref, p1_ref, qp_ref):
    h2 = h2_ref[...]
    h2t_ref[...] = pltpu.bitcast(h2.astype(F32).T.astype(BF16), jnp.uint32)
    qp_ref[...] = jnp.dot(h2, wq_ref[...], preferred_element_type=F32).astype(BF16)
    nt = (((1,), (1,)), ((), ()))

    def head(h, carry):
        col = pl.multiple_of(h * PEER_QDIM, PEER_QDIM)
        s1 = lax.dot_general(keys_ref[h, 0], qp_ref[:, pl.ds(col, PEER_HALF)], nt,
                             preferred_element_type=F32)
        s2 = lax.dot_general(keys_ref[h, 1], qp_ref[:, pl.ds(col + PEER_HALF, PEER_HALF)], nt,
                             preferred_element_type=F32)
        v1 = _top_desc_sorted(s1, PEER_TOPK)
        v2 = _top_desc_sorted(s2, PEER_TOPK)
        best = _top_pair_sums(v1, v2, PEER_TOPK)
        tau = best[PEER_TOPK - 1:PEER_TOPK]
        z = jnp.sum(jnp.exp(best - best[0:1]), axis=0, keepdims=True)
        rank = jnp.zeros_like(s2)
        cnt_top = jnp.zeros_like(v1)
        for k in range(PEER_TOPK):
            vk = v2[k:k + 1]
            rank = jnp.where(vk > s2, k + 1.0, rank)
            cnt_top = jnp.where(v1 + vk >= tau, k + 1.0, cnt_top)
        cnt = jnp.zeros_like(s1)
        for j in range(PEER_TOPK):
            cnt = jnp.where(s1 == v1[j:j + 1], cnt_top[j:j + 1], cnt)
        steps = N_EXPERTS // PEER_EXPERTS
        cnt_ref[:, h] = cnt.reshape(steps, N_KEYS // steps, ROUTE_TILE)
        p1_ref[:, h] = jnp.exp(s1 - v1[0:1]).reshape(steps, N_KEYS // steps, ROUTE_TILE)
        rank_ref[h] = pltpu.bitcast(rank.astype(BF16), jnp.uint32)
        p2_ref[h] = pltpu.bitcast((jnp.exp(s2 - v2[0:1]) * (0.5 / z)).astype(BF16), jnp.uint32)
        return carry

    lax.fori_loop(0, PEER_HEADS, head, 0, unroll=2)


def _route(h2, wq, keys):
    t = h2.shape[0]
    tok4 = pl.BlockSpec((PEER_HEADS, N_KEYS // 2, ROUTE_TILE), lambda i: (0, 0, i))
    steps = N_EXPERTS // PEER_EXPERTS
    tok3 = pl.BlockSpec((steps, PEER_HEADS, N_KEYS // steps, ROUTE_TILE), lambda i: (0, 0, 0, i))
    return pl.pallas_call(
        _route_kernel,
        grid=(t // ROUTE_TILE,),
        in_specs=[pl.BlockSpec((ROUTE_TILE, D_MODEL), lambda i: (i, 0)),
                  _full((D_MODEL, PEER_HEADS * PEER_QDIM)),
                  _full((PEER_HEADS, 2, N_KEYS, PEER_HALF))],
        out_specs=[pl.BlockSpec((D_MODEL // 2, ROUTE_TILE), lambda i: (0, i)), tok4, tok4, tok3, tok3],
        out_shape=[jax.ShapeDtypeStruct((D_MODEL // 2, t), jnp.uint32),
                   jax.ShapeDtypeStruct((PEER_HEADS, N_KEYS // 2, t), jnp.uint32),
                   jax.ShapeDtypeStruct((PEER_HEADS, N_KEYS // 2, t), jnp.uint32),
                   jax.ShapeDtypeStruct((steps, PEER_HEADS, N_KEYS // steps, t), F32),
                   jax.ShapeDtypeStruct((steps, PEER_HEADS, N_KEYS // steps, t), F32)],
        scratch_shapes=[pltpu.VMEM((ROUTE_TILE, PEER_HEADS * PEER_QDIM), BF16)],
        compiler_params=_params("arbitrary"),
        name="peer_route",
    )(h2, wq, keys)


def _pack_kernel(transpose, x_ref, o_ref):
    x = x_ref[...]
    if transpose:
        x = x.T
    o_ref[...] = pltpu.bitcast(x.astype(BF16), jnp.uint32)


def _pack_table(table, transpose):
    rows, cols = table.shape
    blk = 512
    if transpose:
        out_spec = pl.BlockSpec((cols // 2, blk), lambda i: (0, i))
        out_shape = jax.ShapeDtypeStruct((cols // 2, rows), jnp.uint32)
    else:
        out_spec = pl.BlockSpec((blk // 2, cols), lambda i: (i, 0))
        out_shape = jax.ShapeDtypeStruct((rows // 2, cols), jnp.uint32)
    return pl.pallas_call(
        functools.partial(_pack_kernel, transpose),
        grid=(rows // blk,),
        in_specs=[pl.BlockSpec((blk, cols), lambda i: (i, 0))],
        out_specs=out_spec,
        out_shape=out_shape,
        compiler_params=_params("arbitrary"),
        name="pack_table_t" if transpose else "pack_table",
    )(table)


def _gate_rows(cs, act_ref, wt_ref, rank_ref, p2_ref, cnt_ref, p1_ref):
    tiles = (N_KEYS // BF16_SUBLANES, BF16_SUBLANES, LANES)
    for lt in range(PEER_TOKENS // LANES):
        lanes = slice(lt * LANES, (lt + 1) * LANES)
        gates = [jnp.zeros(tiles, BF16) for _ in cs]
        for h in range(PEER_HEADS):
            rank = pltpu.bitcast(rank_ref[h, :, lanes], BF16).reshape(tiles)
            p2 = pltpu.bitcast(p2_ref[h, :, lanes], BF16).reshape(tiles)
            for n, c in enumerate(cs):
                cnt = jnp.broadcast_to(cnt_ref[h, c:c + 1, lanes], (BF16_SUBLANES, LANES)).astype(BF16)
                p1 = jnp.broadcast_to(p1_ref[h, c:c + 1, lanes], (BF16_SUBLANES, LANES)).astype(BF16)
                gates[n] = gates[n] + jnp.where(rank < cnt[None], p2, jnp.zeros((), BF16)) * p1[None]
        for n, c in enumerate(cs):
            rows = slice(c * N_KEYS, (c + 1) * N_KEYS)
            a = act_ref[rows, lanes]
            gl = a + a * lax.erf(a * (1.0 / math.sqrt(2.0)))
            wt_ref[rows, lanes] = gl.astype(BF16) * gates[n].reshape(N_KEYS, LANES)


def _expert_kernel(pairs_per_block, u_ref, vt_ref, h2t_ref, rank_a, p2_a, rank_b, p2_b, cnt_a, p1_a,
                   cnt_b, p1_b, x1_ref, mod_ref, y_ref, acc_ref, act_a, act_b, wt_a, wt_b):
    g = pl.program_id(0)

    @pl.when(g == 0)
    def _():
        acc_ref[...] = jnp.zeros_like(acc_ref)
        act_b[...] = jnp.zeros_like(act_b)
        wt_a[...] = jnp.zeros_like(wt_a)

    key_rows = PEER_EXPERTS // N_KEYS

    def half_step(half, wt_old, act_old, wt_new, act_new, rank_ref, p2_ref, cnt_ref, p1_ref):
        regions = 2 * DOT_SPLIT
        gate_rows = key_rows // regions
        for idx in range(regions):
            @pl.when(g >= -(half * regions + idx))
            def _(idx=idx):
                for c in range(idx * gate_rows, (idx + 1) * gate_rows, GATE_ROWS):
                    _gate_rows(range(c, c + GATE_ROWS), act_old, wt_new, rank_ref, p2_ref, cnt_ref, p1_ref)
                part = idx % DOT_SPLIT
                if idx < DOT_SPLIT:
                    m = D_MODEL // DOT_SPLIT
                    vt = pltpu.bitcast(vt_ref[part * m // 2:(part + 1) * m // 2,
                                              half * PEER_EXPERTS:(half + 1) * PEER_EXPERTS], BF16)
                    acc_ref[part * m:(part + 1) * m, :] += jnp.dot(vt, wt_old[...], preferred_element_type=F32)
                else:
                    m = PEER_EXPERTS // DOT_SPLIT
                    lo = (half * PEER_EXPERTS + part * m) // 2
                    u = pltpu.bitcast(u_ref[lo:lo + m // 2, :], BF16)
                    act_new[part * m:(part + 1) * m, :] = jnp.dot(u, pltpu.bitcast(h2t_ref[...], BF16),
                                                                  preferred_element_type=F32)

    half_step(0, wt_a, act_b, wt_b, act_a, rank_a, p2_a, cnt_a, p1_a)
    half_step(1, wt_b, act_a, wt_a, act_b, rank_b, p2_b, cnt_b, p1_b)

    @pl.when((g > 0) & (g % pairs_per_block == 0))
    def _():
        g2 = mod_ref[5:6, :]
        y_ref[...] = x1_ref[...] + g2 * acc_ref[...].T
        acc_ref[...] = jnp.zeros_like(acc_ref)


def _experts(u_bf, vt_bf, h2t, rank, p2, cnt, p1, x1, mod3, mod_row):
    t = x1.shape[0]
    ppb = N_EXPERTS // (2 * PEER_EXPERTS)
    steps = (t // PEER_TOKENS) * ppb + 1
    cur = lambda g: jnp.minimum(g, steps - 2)
    prv = lambda g: jnp.maximum(g - 1, 0)
    packed = lambda f: pl.BlockSpec((PEER_HEADS, N_KEYS // 2, PEER_TOKENS), lambda g: (0, 0, f(g) // ppb))
    rows = lambda f, half: pl.BlockSpec((None, PEER_HEADS, PEER_EXPERTS // N_KEYS, PEER_TOKENS),
                                        lambda g: (2 * (f(g) % ppb) + half, 0, 0, f(g) // ppb))
    return pl.pallas_call(
        functools.partial(_expert_kernel, ppb),
        grid=(steps,),
        in_specs=[pl.BlockSpec((PEER_EXPERTS, D_MODEL), lambda g: (cur(g) % ppb, 0)),
                  pl.BlockSpec((D_MODEL // 2, 2 * PEER_EXPERTS), lambda g: (0, prv(g) % ppb)),
                  pl.BlockSpec((D_MODEL // 2, PEER_TOKENS), lambda g: (0, cur(g) // ppb)),
                  packed(prv), packed(prv), packed(cur), packed(cur),
                  rows(prv, 1), rows(prv, 1), rows(cur, 0), rows(cur, 0),
                  pl.BlockSpec((PEER_TOKENS, D_MODEL), lambda g: (prv(g) // ppb, 0)),
                  pl.BlockSpec((None, 6, D_MODEL), lambda g: (mod_row(prv(g) // ppb), 0, 0))],
        out_specs=pl.BlockSpec((PEER_TOKENS, D_MODEL), lambda g: (prv(g) // ppb, 0)),
        out_shape=jax.ShapeDtypeStruct((t, D_MODEL), F32),
        scratch_shapes=[pltpu.VMEM((D_MODEL, PEER_TOKENS), F32),
                        pltpu.VMEM((PEER_EXPERTS, PEER_TOKENS), F32),
                        pltpu.VMEM((PEER_EXPERTS, PEER_TOKENS), F32),
                        pltpu.VMEM((PEER_EXPERTS, PEER_TOKENS), BF16),
                        pltpu.VMEM((PEER_EXPERTS, PEER_TOKENS), BF16)],
        compiler_params=_params("arbitrary"),
        name="peer_experts",
    )(u_bf, vt_bf, h2t, rank, p2, rank, p2, cnt, p1, cnt, p1, x1, mod3)


def _rope_tables(seq_len):
    rows = seq_len // GRID_W
    row = jnp.repeat(jnp.arange(rows, dtype=F32), GRID_W)
    col = jnp.tile(jnp.arange(GRID_W, dtype=F32), rows)
    inv = ROPE_THETA ** (-jnp.arange(AXIS_FREQS, dtype=F32) / AXIS_FREQS)
    ang_row = row[:, None] * inv
    ang_col = col[:, None] * inv
    zeros = jnp.zeros_like(ang_row)
    cos = jnp.concatenate([jnp.cos(ang_row)] * 2 + [jnp.cos(ang_col)] * 2, axis=-1)
    sin_lo = jnp.concatenate([-jnp.sin(ang_row), zeros, -jnp.sin(ang_col), zeros], axis=-1)
    sin_hi = jnp.concatenate([zeros, jnp.sin(ang_row), zeros, jnp.sin(ang_col)], axis=-1)
    return tuple(jnp.tile(t, (1, N_HEADS)) for t in (cos, sin_lo, sin_hi))


def _head_mean_matrix(width):
    idx = jnp.arange(width) // HEAD_DIM
    return jnp.where(idx[:, None] == idx[None, :], 1.0 / HEAD_DIM, 0.0).astype(BF16)


def _trunk(x3, mod3, mod_of_seq, cache, w, rope):
    b, seq, _ = x3.shape
    t = b * seq
    tiles_per_seq = seq // ROW_TILE
    q, k, v, xg = _inproj(x3.reshape(t, D_MODEL), mod3, lambda i: mod_of_seq(i // tiles_per_seq),
                          tiles_per_seq, w["norm1"], w["w_in"], w["qn"], w["kn"], w["gq"], w["gk"], rope)
    conv = _conv_module(xg, tiles_per_seq, w["conv_dw"], w["conv_dw_b"], w["conv_ln_g"], w["conv_ln_b"],
                        w["conv_pw"])
    x1, h2 = _mix(x3, mod3, mod_of_seq, q.reshape(b, seq, ATTN_WIDTH), k.reshape(b, seq, KV_WIDTH),
                  v.reshape(b, seq, KV_WIDTH), conv.reshape(b, seq, CONV_CH), cache, w["sink"],
                  w["w_out"], w["norm2"])
    h2t, rank, p2, cnt, p1 = _route(h2.reshape(t, D_MODEL), w["peer_wq"], w["peer_keys"])
    per_seq = seq // PEER_TOKENS
    if per_seq == 0:
        seqs_per_block = PEER_TOKENS // seq
        row_of_block = lambda i: mod_of_seq(i * seqs_per_block)
    else:
        row_of_block = lambda i: mod_of_seq(i // per_seq)
    y = _experts(w["peer_u"], w["peer_vt"], h2t, rank, p2, cnt, p1, x1.reshape(t, D_MODEL), mod3, row_of_block)
    return y.reshape(b, seq, D_MODEL), k, v


def kernel(x_prompt, x_sample, cache_k, cache_v, c, c_ctx, norm1, norm2, w_ada, b_ada, w_in, w_out,
           q_norm, k_norm, attn_sink, conv_dw, conv_dw_b, conv_ln_g, conv_ln_b, conv_pw,
           peer_wq, peer_keys, peer_u, peer_v):
    depth = norm1.shape[0]
    batch, seq, _ = x_prompt.shape
    dec_batch, dec_seq, _ = x_sample.shape
    past = cache_k.shape[2]
    assert dec_batch + 1 <= 8 and depth >= 1
    cond8 = jnp.concatenate([c_ctx[None, :], c, jnp.zeros((8 - 1 - dec_batch, D_MODEL), F32)], axis=0)
    rope = _rope_tables(dec_seq)
    xp, xs = x_prompt, x_sample
    new_ks, new_vs = [], []
    for l in range(depth):
        w = dict(
            norm1=norm1[l][None], norm2=norm2[l][None],
            w_in=w_in[l].astype(BF16), w_out=w_out[l].astype(BF16),
            qn=jnp.tile(q_norm[l], N_HEADS)[None], kn=jnp.tile(k_norm[l], N_KV_HEADS)[None],
            gq=_head_mean_matrix(ATTN_WIDTH), gk=_head_mean_matrix(KV_WIDTH),
            sink=attn_sink[l],
            conv_dw=conv_dw[l], conv_dw_b=conv_dw_b[l][None], conv_ln_g=conv_ln_g[l][None],
            conv_ln_b=conv_ln_b[l][None], conv_pw=conv_pw[l].astype(BF16),
            peer_wq=peer_wq[l].astype(BF16), peer_keys=peer_keys[l].astype(BF16),
            peer_u=_pack_table(peer_u[l], False), peer_vt=_pack_table(peer_v[l], True),
        )
        mod3 = _modulation(cond8, w_ada[l], b_ada[l]).reshape(8, 6, D_MODEL)
        xp, k_c, v_c = _trunk(xp, mod3, lambda s: 0, None, w, None)
        new_ks.append(k_c.reshape(batch, seq, N_KV_HEADS, HEAD_DIM))
        new_vs.append(v_c.reshape(batch, seq, N_KV_HEADS, HEAD_DIM))
        cache = (cache_k[:, l].reshape(dec_batch, past, KV_WIDTH), cache_v[:, l].reshape(dec_batch, past, KV_WIDTH))
        xs, _, _ = _trunk(xs, mod3, lambda s: s + 1, cache, w, rope)
    return (xp, xs, jnp.stack(new_ks, axis=1), jnp.stack(new_vs, axis=1))
```

```python
import functools
import math

import jax
import jax.numpy as jnp
from jax import lax
from jax.experimental import pallas as pl
from jax.experimental.pallas import tpu as pltpu

F32 = jnp.float32
BF16 = jnp.bfloat16

D_MODEL = 1024
GRID_W = 64
N_HEADS = 8
N_KV_HEADS = 2
GROUP = N_HEADS // N_KV_HEADS
HEAD_DIM = 64
ATTN_WIDTH = N_HEADS * HEAD_DIM
KV_WIDTH = N_KV_HEADS * HEAD_DIM
WINDOW = 128
BLOCK = 128
ROPE_THETA = 10000.0
AXIS_DIM = HEAD_DIM // 2
AXIS_FREQS = AXIS_DIM // 2
CONV_CH = D_MODEL - ATTN_WIDTH
CONV_K = 31
CONV_PAD = CONV_K // 2
IN_WIDTH = ATTN_WIDTH + 2 * KV_WIDTH + 2 * CONV_CH
N_KEYS = 128
N_EXPERTS = N_KEYS * N_KEYS
PEER_HEADS = 8
PEER_QDIM = 256
PEER_HALF = PEER_QDIM // 2
PEER_TOPK = 16
EPS = 1e-6

LANES = 128
BF16_SUBLANES = 16
ROW_TILE = 256
INPROJ_TILE = 512
MIX_TILE = 512
HALO = 16
ROUTE_TILE = 512
PEER_TOKENS = 512
PEER_EXPERTS = 1024
GATE_ROWS = 4
DOT_SPLIT = 1
VMEM_LIMIT = 48 * 1024 * 1024

NEG_INF = float("-inf")


def _params(*sem, flags=None):
    return pltpu.CompilerParams(dimension_semantics=sem, vmem_limit_bytes=VMEM_LIMIT, flags=flags)


def _full(shape):
    return pl.BlockSpec(shape, lambda *_: (0,) * len(shape))


def _mod_kernel(cond_ref, w_ref, b_ref, o_ref):
    cnd = cond_ref[...]
    act = cnd * jax.nn.sigmoid(cnd)
    o_ref[...] = jnp.dot(act, w_ref[...], precision=lax.Precision.HIGHEST,
                         preferred_element_type=F32) + b_ref[...]


def _modulation(cond8, w_ada, b_ada):
    n = w_ada.shape[1]
    nb = n // D_MODEL
    return pl.pallas_call(
        _mod_kernel,
        grid=(nb,),
        in_specs=[_full((8, D_MODEL)),
                  pl.BlockSpec((D_MODEL, D_MODEL), lambda j: (0, j)),
                  pl.BlockSpec((1, D_MODEL), lambda j: (0, j))],
        out_specs=pl.BlockSpec((8, D_MODEL), lambda j: (0, j)),
        out_shape=jax.ShapeDtypeStruct((8, n), F32),
        compiler_params=_params("arbitrary"),
        name="modulation",
    )(cond8, w_ada, b_ada.reshape(1, n))


def _group_rms(t, gmat, gain):
    ms = jnp.dot((t * t).astype(BF16), gmat, preferred_element_type=F32)
    return t * lax.rsqrt(ms + EPS) * gain


def _rope(t, cos, sin_lo, sin_hi):
    w = t.shape[1]
    return (t * cos + pltpu.roll(t, w - AXIS_FREQS, 1) * sin_lo
            + pltpu.roll(t, AXIS_FREQS, 1) * sin_hi)


def _inproj_kernel(latent, x_ref, mod_ref, n1_ref, win_ref, qn_ref, kn_ref, gq_ref, gk_ref, *rest):
    if latent:
        cos_ref, slo_ref, shi_ref, q_ref, k_ref, v_ref, xg_ref = rest
    else:
        q_ref, k_ref, v_ref, xg_ref = rest
    x = x_ref[...]
    sh1 = mod_ref[0:1, :]
    sc1 = mod_ref[1:2, :]
    h = x * lax.rsqrt(jnp.mean(x * x, axis=-1, keepdims=True) + EPS) * n1_ref[...]
    h = h * (1.0 + sc1) + sh1
    z = jnp.dot(h.astype(BF16), win_ref[...], preferred_element_type=F32)
    q = _group_rms(z[:, :ATTN_WIDTH], gq_ref[...], qn_ref[...])
    k = _group_rms(z[:, ATTN_WIDTH:ATTN_WIDTH + KV_WIDTH], gk_ref[...], kn_ref[...])
    if latent:
        q = _rope(q, cos_ref[...], slo_ref[...], shi_ref[...])
        k = _rope(k, cos_ref[:, :KV_WIDTH], slo_ref[:, :KV_WIDTH], shi_ref[:, :KV_WIDTH])
    q_ref[...] = (q * (1.0 / math.sqrt(HEAD_DIM))).astype(BF16)
    k_ref[...] = k
    v_ref[...] = z[:, ATTN_WIDTH + KV_WIDTH:ATTN_WIDTH + 2 * KV_WIDTH]
    u = z[:, ATTN_WIDTH + 2 * KV_WIDTH:]
    xg_ref[...] = u[:, :CONV_CH] * jax.nn.sigmoid(u[:, CONV_CH:])


def _inproj(x2d, mod3, mod_of_seq, seq, norm1, w_in, qn, kn, gq, gk, rope):
    t = x2d.shape[0]
    latent = rope is not None
    row = lambda i: (i, 0)
    tile = INPROJ_TILE
    if seq >= tile:
        assert seq % tile == 0
        tiles_per_seq = seq // tile
        mod_row = lambda i: mod_of_seq(i // tiles_per_seq)
    else:
        assert tile % seq == 0 and not latent
        tiles_per_seq = 1
        mod_row = lambda i: mod_of_seq(i * (tile // seq))
    in_specs = [pl.BlockSpec((tile, D_MODEL), row),
                pl.BlockSpec((None, 6, D_MODEL), lambda i: (mod_row(i), 0, 0)),
                _full((1, D_MODEL)), _full((D_MODEL, IN_WIDTH)),
                _full((1, ATTN_WIDTH)), _full((1, KV_WIDTH)),
                _full((ATTN_WIDTH, ATTN_WIDTH)), _full((KV_WIDTH, KV_WIDTH))]
    args = [x2d, mod3, norm1, w_in, qn, kn, gq, gk]
    if latent:
        pos = lambda i: (i % tiles_per_seq, 0)
        in_specs += [pl.BlockSpec((tile, ATTN_WIDTH), pos)] * 3
        args += list(rope)
    return pl.pallas_call(
        functools.partial(_inproj_kernel, latent),
        grid=(t // tile,),
        in_specs=in_specs,
        out_specs=[pl.BlockSpec((tile, ATTN_WIDTH), row),
                   pl.BlockSpec((tile, KV_WIDTH), row),
                   pl.BlockSpec((tile, KV_WIDTH), row),
                   pl.BlockSpec((tile, CONV_CH), row)],
        out_shape=[jax.ShapeDtypeStruct((t, ATTN_WIDTH), BF16),
                   jax.ShapeDtypeStruct((t, KV_WIDTH), F32),
                   jax.ShapeDtypeStruct((t, KV_WIDTH), F32),
                   jax.ShapeDtypeStruct((t, CONV_CH), F32)],
        compiler_params=_params("arbitrary"),
        name="inproj_latent" if latent else "inproj_context",
    )(*args)


def _conv_kernel(tiles_per_seq, prev_ref, cur_ref, nxt_ref, dw_ref, db_ref, lg_ref, lb_ref, pw_ref,
                 o_ref, xs_ref, act_ref):
    i = pl.program_id(0)
    pos = i % tiles_per_seq
    zero = jnp.zeros((HALO, CONV_CH), F32)
    xs_ref[0, 0:HALO, :] = jnp.where(pos > 0, prev_ref[...], zero)
    xs_ref[0, HALO:HALO + ROW_TILE, :] = cur_ref[...]
    xs_ref[0, HALO + ROW_TILE:2 * HALO + ROW_TILE, :] = jnp.where(pos < tiles_per_seq - 1, nxt_ref[...], zero)
    span = ROW_TILE + 2 * HALO - 8
    for r in range(1, 8):
        xs_ref[r, 0:span, :] = xs_ref[0, r:r + span, :]
    chunk = 32

    def body(c, carry):
        base = pl.multiple_of(c * chunk, chunk)
        acc = jnp.zeros((chunk, CONV_CH), F32) + db_ref[...]
        for k in range(CONV_K):
            kk = k + HALO - CONV_PAD
            start = pl.multiple_of(base + 8 * (kk // 8), 8)
            acc = acc + xs_ref[kk % 8, pl.ds(start, chunk), :] * dw_ref[k:k + 1, :]
        act_ref[pl.ds(base, chunk), :] = acc
        return carry

    lax.fori_loop(0, ROW_TILE // chunk, body, 0)
    acc = act_ref[...]
    mu = jnp.mean(acc, axis=-1, keepdims=True)
    cen = acc - mu
    var = jnp.mean(cen * cen, axis=-1, keepdims=True)
    y = cen * lax.rsqrt(var + EPS) * lg_ref[...] + lb_ref[...]
    y = y * jax.nn.sigmoid(y)
    o_ref[...] = jnp.dot(y.astype(BF16), pw_ref[...], preferred_element_type=F32).astype(BF16)


def _conv_module(xg, tiles_per_seq, dw, db, lg, lb, pw):
    t = xg.shape[0]
    nt = t // ROW_TILE
    per = ROW_TILE // HALO
    last = t // HALO - 1
    return pl.pallas_call(
        functools.partial(_conv_kernel, tiles_per_seq),
        grid=(nt,),
        in_specs=[pl.BlockSpec((HALO, CONV_CH), lambda i: (jnp.maximum(i * per - 1, 0), 0)),
                  pl.BlockSpec((ROW_TILE, CONV_CH), lambda i: (i, 0)),
                  pl.BlockSpec((HALO, CONV_CH), lambda i: (jnp.minimum((i + 1) * per, last), 0)),
                  _full((CONV_K, CONV_CH)), _full((1, CONV_CH)), _full((1, CONV_CH)), _full((1, CONV_CH)),
                  _full((CONV_CH, CONV_CH))],
        out_specs=pl.BlockSpec((ROW_TILE, CONV_CH), lambda i: (i, 0)),
        out_shape=jax.ShapeDtypeStruct((t, CONV_CH), BF16),
        scratch_shapes=[pltpu.VMEM((8, ROW_TILE + 2 * HALO, CONV_CH), F32),
                        pltpu.VMEM((ROW_TILE, CONV_CH), F32)],
        compiler_params=_params("arbitrary"),
        name="conv_module",
    )(xg, xg, xg, dw, db, lg, lb, pw)


def _attend(qg, sink_col, parts):
    m = sink_col
    scores = []
    for keys, _, mask in parts:
        s = lax.dot_general(qg, keys, (((1,), (1,)), ((), ())), preferred_element_type=F32)
        if mask is not None:
            s = jnp.where(mask, s, NEG_INF)
        m = jnp.maximum(m, jnp.max(s, axis=1, keepdims=True))
        scores.append(s)
    denom = jnp.exp(sink_col - m)
    out = None
    for s, (_, vals, _) in zip(scores, parts):
        p = jnp.exp(s - m)
        denom = denom + jnp.sum(p, axis=1, keepdims=True)
        o = jnp.dot(p.astype(BF16), vals, preferred_element_type=F32)
        out = o if out is None else out + o
    return out / denom


def _mix_kernel(latent, seq, x_ref, mod_ref, q_ref, k_ref, v_ref, cv_ref, *rest):
    if latent:
        ck_ref, cvv_ref, sink_ref, wout_ref, n2_ref, x1_ref, h2_ref, attn_ref = rest
    else:
        sink_ref, wout_ref, n2_ref, x1_ref, h2_ref, attn_ref = rest
    prob = ROW_TILE
    width = min(seq, prob + 2 * WINDOW)
    mix_tile = x_ref.shape[0]
    steps_per_seq = max(seq // mix_tile, 1)
    for sub in range(mix_tile // prob):
        rows = slice(sub * prob, (sub + 1) * prob)
        if latent:
            q0 = (pl.program_id(0) % steps_per_seq) * mix_tile + sub * prob
            start = pl.multiple_of(jnp.clip(q0 - WINDOW, 0, seq - width), BLOCK)
            qpos = q0 + (lax.broadcasted_iota(jnp.int32, (GROUP * prob, width), 0) & (prob - 1))
            kpos = start + lax.broadcasted_iota(jnp.int32, (GROUP * prob, width), 1)
            mask = jnp.abs(kpos - qpos) <= WINDOW
        else:
            start = sub * prob
            mask = None
        for kv in range(N_KV_HEADS):
            lanes = slice(kv * HEAD_DIM, (kv + 1) * HEAD_DIM)
            parts = [(k_ref[pl.ds(start, width), lanes].astype(BF16),
                      v_ref[pl.ds(start, width), lanes].astype(BF16), mask)]
            if latent:
                parts.append((ck_ref[:, lanes].astype(BF16), cvv_ref[:, lanes].astype(BF16), None))
            sink_col = jnp.concatenate(
                [jnp.full((prob, 1), sink_ref[kv * GROUP + g], F32) for g in range(GROUP)], axis=0)
            qg = jnp.concatenate(
                [q_ref[rows, (kv * GROUP + g) * HEAD_DIM:(kv * GROUP + g + 1) * HEAD_DIM]
                 for g in range(GROUP)], axis=0)
            o = _attend(qg, sink_col, parts)
            for g in range(GROUP):
                h = kv * GROUP + g
                attn_ref[rows, h * HEAD_DIM:(h + 1) * HEAD_DIM] = o[g * prob:(g + 1) * prob].astype(BF16)
    g1 = mod_ref[2:3, :]
    sh2 = mod_ref[3:4, :]
    sc2 = mod_ref[4:5, :]
    mixed = (jnp.dot(attn_ref[...], wout_ref[0:ATTN_WIDTH, :], preferred_element_type=F32)
             + jnp.dot(cv_ref[...], wout_ref[ATTN_WIDTH:, :], preferred_element_type=F32))
    x1 = x_ref[...] + g1 * mixed
    x1_ref[...] = x1
    h2 = x1 * lax.rsqrt(jnp.mean(x1 * x1, axis=-1, keepdims=True) + EPS) * n2_ref[...]
    h2_ref[...] = (h2 * (1.0 + sc2) + sh2).astype(BF16)


def _mix(x3, mod3, mod_row, q3, k3, v3, conv3, cache, sink, w_out, norm2):
    b, seq, _ = x3.shape
    latent = cache is not None
    t = b * seq
    if latent:
        mix_tile = ROW_TILE
        assert seq % mix_tile == 0
        steps_per_seq = seq // mix_tile
        seq_of = lambda i: i // steps_per_seq
        kv_spec = pl.BlockSpec((seq, KV_WIDTH), lambda i: (seq_of(i), 0))
    else:
        mix_tile = MIX_TILE
        assert seq == ROW_TILE and t % mix_tile == 0
        seqs_per_step = mix_tile // seq
        seq_of = lambda i: i * seqs_per_step
        kv_spec = pl.BlockSpec((mix_tile, KV_WIDTH), lambda i: (i, 0))
    tile = lambda w: pl.BlockSpec((mix_tile, w), lambda i: (i, 0))
    flat = lambda a: a.reshape(t, a.shape[-1])
    in_specs = [tile(D_MODEL),
                pl.BlockSpec((None, 6, D_MODEL), lambda i: (mod_row(seq_of(i)), 0, 0)),
                tile(ATTN_WIDTH), kv_spec, kv_spec, tile(CONV_CH)]
    args = [flat(x3), mod3, flat(q3), flat(k3), flat(v3), flat(conv3)]
    if latent:
        past = cache[0].shape[1]
        in_specs += [pl.BlockSpec((None, past, KV_WIDTH), lambda i: (seq_of(i), 0, 0))] * 2
        args += list(cache)
    in_specs += [pl.BlockSpec(memory_space=pltpu.SMEM), _full((D_MODEL, D_MODEL)), _full((1, D_MODEL))]
    args += [sink, w_out, norm2]
    x1, h2 = pl.pallas_call(
        functools.partial(_mix_kernel, latent, seq),
        grid=(t // mix_tile,),
        in_specs=in_specs,
        out_specs=[tile(D_MODEL), tile(D_MODEL)],
        out_shape=[jax.ShapeDtypeStruct((t, D_MODEL), F32),
                   jax.ShapeDtypeStruct((t, D_MODEL), BF16)],
        scratch_shapes=[pltpu.VMEM((mix_tile, ATTN_WIDTH), BF16)],
        compiler_params=_params("arbitrary"),
        name="mix_latent" if latent else "mix_context",
    )(*args)
    return x1.reshape(b, seq, D_MODEL), h2.reshape(b, seq, D_MODEL)


def _merge_exchange_pairs(n):
    pairs, p = [], 1
    while p < n:
        k = p
        while k >= 1:
            for j in range(k % p, n - k, 2 * k):
                for i in range(min(k, n - j - k)):
                    if (i + j) // (2 * p) == (i + j + k) // (2 * p):
                        pairs.append((i + j, i + j + k))
            k //= 2
        p *= 2
    return pairs


def _top_desc_sorted(x, count):
    lists = [x[8 * g:8 * (g + 1)] for g in range(count)]
    for a, b in _merge_exchange_pairs(count):
        lists[a], lists[b] = jnp.maximum(lists[a], lists[b]), jnp.minimum(lists[a], lists[b])
    slot = lax.broadcasted_iota(jnp.int32, (count, x.shape[1]), 0)
    out = jnp.zeros((count, x.shape[1]), F32)
    for k in range(count):
        m = jnp.max(lists[0], axis=0, keepdims=True)
        out = jnp.where(slot == k, m, out)
        hit = lists[0] == m
        for lvl in range(count - 1 - k):
            lists[lvl] = jnp.where(hit, lists[lvl + 1], lists[lvl])
    return out


def _top_pair_sums(v1, v2, count):
    cols = v1.shape[1]
    row = lax.broadcasted_iota(jnp.int32, (8, cols), 0)
    lists = [jnp.where(row < count // (k + 1), v1[0:8] + v2[k:k + 1], NEG_INF) for k in range(count)]
    tail = v1[8:count] + v2[0:1]
    slot = lax.broadcasted_iota(jnp.int32, (count, cols), 0)
    out = jnp.zeros((count, cols), F32)
    for n in range(count):
        m = jnp.max(jnp.maximum(lists[0], tail), axis=0, keepdims=True)
        out = jnp.where(slot == n, m, out)
        hit = lists[0] == m
        for lvl in range(count - 1 - n):
            lists[lvl] = jnp.where(hit, lists[lvl + 1], lists[lvl])
        tail = jnp.where(tail == m, NEG_INF, tail)
    return out


def _route_kernel(h2_ref, wq_ref, keys_ref, h2t_ref, rank_ref, p2_ref, cnt_ref, p1_ref, qp_ref):
    h2 = h2_ref[...]
    h2t_ref[...] = pltpu.bitcast(h2.astype(F32).T.astype(BF16), jnp.uint32)
    qp_ref[...] = jnp.dot(h2, wq_ref[...], preferred_element_type=F32).astype(BF16)
    nt = (((1,), (1,)), ((), ()))

    def head(h, carry):
        col = pl.multiple_of(h * PEER_QDIM, PEER_QDIM)
        s1 = lax.dot_general(keys_ref[h, 0], qp_ref[:, pl.ds(col, PEER_HALF)], nt,
                             preferred_element_type=F32)
        s2 = lax.dot_general(keys_ref[h, 1], qp_ref[:, pl.ds(col + PEER_HALF, PEER_HALF)], nt,
                             preferred_element_type=F32)
        v1 = _top_desc_sorted(s1, PEER_TOPK)
        v2 = _top_desc_sorted(s2, PEER_TOPK)
        best = _top_pair_sums(v1, v2, PEER_TOPK)
        tau = best[PEER_TOPK - 1:PEER_TOPK]
        z = jnp.sum(jnp.exp(best - best[0:1]), axis=0, keepdims=True)
        rank = jnp.zeros_like(s2)
        cnt_top = jnp.zeros_like(v1)
        for k in range(PEER_TOPK):
            vk = v2[k:k + 1]
            rank = jnp.where(vk > s2, k + 1.0, rank)
            cnt_top = jnp.where(v1 + vk >= tau, k + 1.0, cnt_top)
        cnt = jnp.zeros_like(s1)
        for j in range(PEER_TOPK):
            cnt = jnp.where(s1 == v1[j:j + 1], cnt_top[j:j + 1], cnt)
        steps = N_EXPERTS // PEER_EXPERTS
        cnt_ref[:, h] = cnt.reshape(steps, N_KEYS // steps, ROUTE_TILE)
        p1_ref[:, h] = jnp.exp(s1 - v1[0:1]).reshape(steps, N_KEYS // steps, ROUTE_TILE)
        rank_ref[h] = pltpu.bitcast(rank.astype(BF16), jnp.uint32)
        p2_ref[h] = pltpu.bitcast((jnp.exp(s2 - v2[0:1]) * (0.5 / z)).astype(BF16), jnp.uint32)
        return carry

    lax.fori_loop(0, PEER_HEADS, head, 0, unroll=2)


def _route(h2, wq, keys):
    t = h2.shape[0]
    tok4 = pl.BlockSpec((PEER_HEADS, N_KEYS // 2, ROUTE_TILE), lambda i: (0, 0, i))
    steps = N_EXPERTS // PEER_EXPERTS
    tok3 = pl.BlockSpec((steps, PEER_HEADS, N_KEYS // steps, ROUTE_TILE), lambda i: (0, 0, 0, i))
    return pl.pallas_call(
        _route_kernel,
        grid=(t // ROUTE_TILE,),
        in_specs=[pl.BlockSpec((ROUTE_TILE, D_MODEL), lambda i: (i, 0)),
                  _full((D_MODEL, PEER_HEADS * PEER_QDIM)),
                  _full((PEER_HEADS, 2, N_KEYS, PEER_HALF))],
        out_specs=[pl.BlockSpec((D_MODEL // 2, ROUTE_TILE), lambda i: (0, i)), tok4, tok4, tok3, tok3],
        out_shape=[jax.ShapeDtypeStruct((D_MODEL // 2, t), jnp.uint32),
                   jax.ShapeDtypeStruct((PEER_HEADS, N_KEYS // 2, t), jnp.uint32),
                   jax.ShapeDtypeStruct((PEER_HEADS, N_KEYS // 2, t), jnp.uint32),
                   jax.ShapeDtypeStruct((steps, PEER_HEADS, N_KEYS // steps, t), F32),
                   jax.ShapeDtypeStruct((steps, PEER_HEADS, N_KEYS // steps, t), F32)],
        scratch_shapes=[pltpu.VMEM((ROUTE_TILE, PEER_HEADS * PEER_QDIM), BF16)],
        compiler_params=_params("arbitrary"),
        name="peer_route",
    )(h2, wq, keys)


def _pack_kernel(transpose, x_ref, o_ref):
    x = x_ref[...]
    if transpose:
        x = x.T
    o_ref[...] = pltpu.bitcast(x.astype(BF16), jnp.uint32)


def _pack_table(table, transpose):
    rows, cols = table.shape
    blk = 512
    if transpose:
        out_spec = pl.BlockSpec((cols // 2, blk), lambda i: (0, i))
        out_shape = jax.ShapeDtypeStruct((cols // 2, rows), jnp.uint32)
    else:
        out_spec = pl.BlockSpec((blk // 2, cols), lambda i: (i, 0))
        out_shape = jax.ShapeDtypeStruct((rows // 2, cols), jnp.uint32)
    return pl.pallas_call(
        functools.partial(_pack_kernel, transpose),
        grid=(rows // blk,),
        in_specs=[pl.BlockSpec((blk, cols), lambda i: (i, 0))],
        out_specs=out_spec,
        out_shape=out_shape,
        compiler_params=_params("arbitrary"),
        name="pack_table_t" if transpose else "pack_table",
    )(table)


def _gate_rows(cs, act_ref, wt_ref, rank_ref, p2_ref, cnt_ref, p1_ref):
    tiles = (N_KEYS // BF16_SUBLANES, BF16_SUBLANES, LANES)
    for lt in range(PEER_TOKENS // LANES):
        lanes = slice(lt * LANES, (lt + 1) * LANES)
        gates = [jnp.zeros(tiles, BF16) for _ in cs]
        for h in range(PEER_HEADS):
            rank = pltpu.bitcast(rank_ref[h, :, lanes], BF16).reshape(tiles)
            p2 = pltpu.bitcast(p2_ref[h, :, lanes], BF16).reshape(tiles)
            for n, c in enumerate(cs):
                cnt = jnp.broadcast_to(cnt_ref[h, c:c + 1, lanes], (BF16_SUBLANES, LANES)).astype(BF16)
                p1 = jnp.broadcast_to(p1_ref[h, c:c + 1, lanes], (BF16_SUBLANES, LANES)).astype(BF16)
                gates[n] = gates[n] + jnp.where(rank < cnt[None], p2, jnp.zeros((), BF16)) * p1[None]
        for n, c in enumerate(cs):
            rows = slice(c * N_KEYS, (c + 1) * N_KEYS)
            a = act_ref[rows, lanes]
            gl = a + a * lax.erf(a * (1.0 / math.sqrt(2.0)))
            wt_ref[rows, lanes] = gl.astype(BF16) * gates[n].reshape(N_KEYS, LANES)


def _expert_kernel(pairs_per_block, u_ref, vt_ref, h2t_ref, rank_a, p2_a, rank_b, p2_b, cnt_a, p1_a,
                   cnt_b, p1_b, x1_ref, mod_ref, y_ref, acc_ref, act_a, act_b, wt_a, wt_b):
    g = pl.program_id(0)

    @pl.when(g == 0)
    def _():
        acc_ref[...] = jnp.zeros_like(acc_ref)
        act_b[...] = jnp.zeros_like(act_b)
        wt_a[...] = jnp.zeros_like(wt_a)

    key_rows = PEER_EXPERTS // N_KEYS

    def half_step(half, wt_old, act_old, wt_new, act_new, rank_ref, p2_ref, cnt_ref, p1_ref):
        regions = 2
        gate_rows = key_rows // regions
        for idx in range(regions):
            @pl.when(g >= -(half * regions + idx))
            def _(idx=idx):
                for c in range(idx * gate_rows, (idx + 1) * gate_rows, GATE_ROWS):
                    _gate_rows(range(c, c + GATE_ROWS), act_old, wt_new, rank_ref, p2_ref, cnt_ref, p1_ref)
                for part in range(DOT_SPLIT):
                    if idx == 0:
                        m = D_MODEL // DOT_SPLIT
                        vt = pltpu.bitcast(vt_ref[part * m // 2:(part + 1) * m // 2,
                                                  half * PEER_EXPERTS:(half + 1) * PEER_EXPERTS], BF16)
                        acc_ref[part * m:(part + 1) * m, :] += jnp.dot(vt, wt_old[...],
                                                                       preferred_element_type=F32)
                    else:
                        m = PEER_EXPERTS // DOT_SPLIT
                        lo = (half * PEER_EXPERTS + part * m) // 2
                        u = pltpu.bitcast(u_ref[lo:lo + m // 2, :], BF16)
                        act_new[part * m:(part + 1) * m, :] = jnp.dot(u, pltpu.bitcast(h2t_ref[...], BF16),
                                                                      preferred_element_type=F32)

    half_step(0, wt_a, act_b, wt_b, act_a, rank_a, p2_a, cnt_a, p1_a)
    half_step(1, wt_b, act_a, wt_a, act_b, rank_b, p2_b, cnt_b, p1_b)

    @pl.when((g > 0) & (g % pairs_per_block == 0))
    def _():
        g2 = mod_ref[5:6, :]
        y_ref[...] = x1_ref[...] + g2 * acc_ref[...].T
        acc_ref[...] = jnp.zeros_like(acc_ref)


def _experts(u_bf, vt_bf, h2t, rank, p2, cnt, p1, x1, mod3, mod_row):
    t = x1.shape[0]
    ppb = N_EXPERTS // (2 * PEER_EXPERTS)
    steps = (t // PEER_TOKENS) * ppb + 1
    cur = lambda g: jnp.minimum(g, steps - 2)
    prv = lambda g: jnp.maximum(g - 1, 0)
    packed = lambda f: pl.BlockSpec((PEER_HEADS, N_KEYS // 2, PEER_TOKENS), lambda g: (0, 0, f(g) // ppb))
    rows = lambda f, half: pl.BlockSpec((None, PEER_HEADS, PEER_EXPERTS // N_KEYS, PEER_TOKENS),
                                        lambda g: (2 * (f(g) % ppb) + half, 0, 0, f(g) // ppb))
    return pl.pallas_call(
        functools.partial(_expert_kernel, ppb),
        grid=(steps,),
        in_specs=[pl.BlockSpec((PEER_EXPERTS, D_MODEL), lambda g: (cur(g) % ppb, 0)),
                  pl.BlockSpec((D_MODEL // 2, 2 * PEER_EXPERTS), lambda g: (0, prv(g) % ppb)),
                  pl.BlockSpec((D_MODEL // 2, PEER_TOKENS), lambda g: (0, cur(g) // ppb)),
                  packed(prv), packed(prv), packed(cur), packed(cur),
                  rows(prv, 1), rows(prv, 1), rows(cur, 0), rows(cur, 0),
                  pl.BlockSpec((PEER_TOKENS, D_MODEL), lambda g: (prv(g) // ppb, 0)),
                  pl.BlockSpec((None, 6, D_MODEL), lambda g: (mod_row(prv(g) // ppb), 0, 0))],
        out_specs=pl.BlockSpec((PEER_TOKENS, D_MODEL), lambda g: (prv(g) // ppb, 0)),
        out_shape=jax.ShapeDtypeStruct((t, D_MODEL), F32),
        scratch_shapes=[pltpu.VMEM((D_MODEL, PEER_TOKENS), F32),
                        pltpu.VMEM((PEER_EXPERTS, PEER_TOKENS), F32),
                        pltpu.VMEM((PEER_EXPERTS, PEER_TOKENS), F32),
                        pltpu.VMEM((PEER_EXPERTS, PEER_TOKENS), BF16),
                        pltpu.VMEM((PEER_EXPERTS, PEER_TOKENS), BF16)],
        compiler_params=_params("arbitrary"),
        name="peer_experts",
    )(u_bf, vt_bf, h2t, rank, p2, rank, p2, cnt, p1, cnt, p1, x1, mod3)


def _rope_tables(seq_len):
    rows = seq_len // GRID_W
    row = jnp.repeat(jnp.arange(rows, dtype=F32), GRID_W)
    col = jnp.tile(jnp.arange(GRID_W, dtype=F32), rows)
    inv = ROPE_THETA ** (-jnp.arange(AXIS_FREQS, dtype=F32) / AXIS_FREQS)
    ang_row = row[:, None] * inv
    ang_col = col[:, None] * inv
    zeros = jnp.zeros_like(ang_row)
    cos = jnp.concatenate([jnp.cos(ang_row)] * 2 + [jnp.cos(ang_col)] * 2, axis=-1)
    sin_lo = jnp.concatenate([-jnp.sin(ang_row), zeros, -jnp.sin(ang_col), zeros], axis=-1)
    sin_hi = jnp.concatenate([zeros, jnp.sin(ang_row), zeros, jnp.sin(ang_col)], axis=-1)
    return tuple(jnp.tile(t, (1, N_HEADS)) for t in (cos, sin_lo, sin_hi))


def _head_mean_matrix(width):
    idx = jnp.arange(width) // HEAD_DIM
    return jnp.where(idx[:, None] == idx[None, :], 1.0 / HEAD_DIM, 0.0).astype(BF16)


def _trunk(x3, mod3, mod_of_seq, cache, w, rope):
    b, seq, _ = x3.shape
    t = b * seq
    tiles_per_seq = seq // ROW_TILE
    q, k, v, xg = _inproj(x3.reshape(t, D_MODEL), mod3, mod_of_seq, seq, w["norm1"], w["w_in"], w["qn"], w["kn"], w["gq"], w["gk"], rope)
    conv = _conv_module(xg, tiles_per_seq, w["conv_dw"], w["conv_dw_b"], w["conv_ln_g"], w["conv_ln_b"],
                        w["conv_pw"])
    x1, h2 = _mix(x3, mod3, mod_of_seq, q.reshape(b, seq, ATTN_WIDTH), k.reshape(b, seq, KV_WIDTH),
                  v.reshape(b, seq, KV_WIDTH), conv.reshape(b, seq, CONV_CH), cache, w["sink"],
                  w["w_out"], w["norm2"])
    h2t, rank, p2, cnt, p1 = _route(h2.reshape(t, D_MODEL), w["peer_wq"], w["peer_keys"])
    per_seq = seq // PEER_TOKENS
    if per_seq == 0:
        seqs_per_block = PEER_TOKENS // seq
        row_of_block = lambda i: mod_of_seq(i * seqs_per_block)
    else:
        row_of_block = lambda i: mod_of_seq(i // per_seq)
    y = _experts(w["peer_u"], w["peer_vt"], h2t, rank, p2, cnt, p1, x1.reshape(t, D_MODEL), mod3, row_of_block)
    return y.reshape(b, seq, D_MODEL), k, v


def kernel(x_prompt, x_sample, cache_k, cache_v, c, c_ctx, norm1, norm2, w_ada, b_ada, w_in, w_out,
           q_norm, k_norm, attn_sink, conv_dw, conv_dw_b, conv_ln_g, conv_ln_b, conv_pw,
           peer_wq, peer_keys, peer_u, peer_v):
    depth = norm1.shape[0]
    batch, seq, _ = x_prompt.shape
    dec_batch, dec_seq, _ = x_sample.shape
    past = cache_k.shape[2]
    assert dec_batch + 1 <= 8 and depth >= 1
    cond8 = jnp.concatenate([c_ctx[None, :], c, jnp.zeros((8 - 1 - dec_batch, D_MODEL), F32)], axis=0)
    rope = _rope_tables(dec_seq)
    xp, xs = x_prompt, x_sample
    new_ks, new_vs = [], []
    for l in range(depth):
        w = dict(
            norm1=norm1[l][None], norm2=norm2[l][None],
            w_in=w_in[l].astype(BF16), w_out=w_out[l].astype(BF16),
            qn=jnp.tile(q_norm[l], N_HEADS)[None], kn=jnp.tile(k_norm[l], N_KV_HEADS)[None],
            gq=_head_mean_matrix(ATTN_WIDTH), gk=_head_mean_matrix(KV_WIDTH),
            sink=attn_sink[l],
            conv_dw=conv_dw[l], conv_dw_b=conv_dw_b[l][None], conv_ln_g=conv_ln_g[l][None],
            conv_ln_b=conv_ln_b[l][None], conv_pw=conv_pw[l].astype(BF16),
            peer_wq=peer_wq[l].astype(BF16), peer_keys=peer_keys[l].astype(BF16),
            peer_u=_pack_table(peer_u[l], False), peer_vt=_pack_table(peer_v[l], True),
        )
        mod3 = _modulation(cond8, w_ada[l], b_ada[l]).reshape(8, 6, D_MODEL)
        xp, k_c, v_c = _trunk(xp, mod3, lambda s: 0, None, w, None)
        new_ks.append(k_c.reshape(batch, seq, N_KV_HEADS, HEAD_DIM))
        new_vs.append(v_c.reshape(batch, seq, N_KV_HEADS, HEAD_DIM))
        cache = (cache_k[:, l].reshape(dec_batch, past, KV_WIDTH), cache_v[:, l].reshape(dec_batch, past, KV_WIDTH))
        xs, _, _ = _trunk(xs, mod3, lambda s: s + 1, cache, w, rope)
    join = (lambda parts: parts[0][:, None]) if depth == 1 else (lambda parts: jnp.stack(parts, axis=1))
    return (xp, xs, join(new_ks), join(new_vs))
```

```python
import functools
import math

import jax
import jax.numpy as jnp
from jax import lax
from jax.experimental import pallas as pl
from jax.experimental.pallas import tpu as pltpu

F32 = jnp.float32
BF16 = jnp.bfloat16

D_MODEL = 1024
GRID_W = 64
N_HEADS = 8
N_KV_HEADS = 2
GROUP = N_HEADS // N_KV_HEADS
HEAD_DIM = 64
ATTN_WIDTH = N_HEADS * HEAD_DIM
KV_WIDTH = N_KV_HEADS * HEAD_DIM
WINDOW = 128
BLOCK = 128
ROPE_THETA = 10000.0
AXIS_DIM = HEAD_DIM // 2
AXIS_FREQS = AXIS_DIM // 2
CONV_CH = D_MODEL - ATTN_WIDTH
CONV_K = 31
CONV_PAD = CONV_K // 2
IN_WIDTH = ATTN_WIDTH + 2 * KV_WIDTH + 2 * CONV_CH
N_KEYS = 128
N_EXPERTS = N_KEYS * N_KEYS
PEER_HEADS = 8
PEER_QDIM = 256
PEER_HALF = PEER_QDIM // 2
PEER_TOPK = 16
EPS = 1e-6

LANES = 128
BF16_SUBLANES = 16
ROW_TILE = 256
INPROJ_TILE = 512
MIX_TILE = 512
HALO = 16
ROUTE_TILE = 512
PEER_TOKENS = 512
PEER_EXPERTS = 1024
GATE_ROWS = 4
DOT_SPLIT = 1
VMEM_LIMIT = 48 * 1024 * 1024

NEG_INF = float("-inf")


def _params(*sem, flags=None):
    return pltpu.CompilerParams(dimension_semantics=sem, vmem_limit_bytes=VMEM_LIMIT, flags=flags)


def _full(shape):
    return pl.BlockSpec(shape, lambda *_: (0,) * len(shape))


def _mod_kernel(cond_ref, w_ref, b_ref, o_ref):
    cnd = cond_ref[...]
    act = cnd * jax.nn.sigmoid(cnd)
    o_ref[...] = jnp.dot(act, w_ref[...], precision=lax.Precision.HIGHEST,
                         preferred_element_type=F32) + b_ref[...]


def _modulation(cond8, w_ada, b_ada):
    n = w_ada.shape[1]
    nb = n // D_MODEL
    return pl.pallas_call(
        _mod_kernel,
        grid=(nb,),
        in_specs=[_full((8, D_MODEL)),
                  pl.BlockSpec((D_MODEL, D_MODEL), lambda j: (0, j)),
                  pl.BlockSpec((1, D_MODEL), lambda j: (0, j))],
        out_specs=pl.BlockSpec((8, D_MODEL), lambda j: (0, j)),
        out_shape=jax.ShapeDtypeStruct((8, n), F32),
        compiler_params=_params("arbitrary"),
        name="modulation",
    )(cond8, w_ada, b_ada.reshape(1, n))


def _group_rms(t, gmat, gain):
    ms = jnp.dot((t * t).astype(BF16), gmat, preferred_element_type=F32)
    return t * lax.rsqrt(ms + EPS) * gain


def _rope(t, cos, sin_lo, sin_hi):
    w = t.shape[1]
    return (t * cos + pltpu.roll(t, w - AXIS_FREQS, 1) * sin_lo
            + pltpu.roll(t, AXIS_FREQS, 1) * sin_hi)


def _inproj_kernel(latent, x_ref, mod_ref, n1_ref, win_ref, qn_ref, kn_ref, gq_ref, gk_ref, *rest):
    if latent:
        cos_ref, slo_ref, shi_ref, q_ref, k_ref, v_ref, xg_ref = rest
    else:
        q_ref, k_ref, v_ref, xg_ref, kt_ref, vt_ref = rest
    x = x_ref[...]
    sh1 = mod_ref[0:1, :]
    sc1 = mod_ref[1:2, :]
    h = x * lax.rsqrt(jnp.mean(x * x, axis=-1, keepdims=True) + EPS) * n1_ref[...]
    h = h * (1.0 + sc1) + sh1
    z = jnp.dot(h.astype(BF16), win_ref[...], preferred_element_type=F32)
    q = _group_rms(z[:, :ATTN_WIDTH], gq_ref[...], qn_ref[...])
    k = _group_rms(z[:, ATTN_WIDTH:ATTN_WIDTH + KV_WIDTH], gk_ref[...], kn_ref[...])
    if latent:
        q = _rope(q, cos_ref[...], slo_ref[...], shi_ref[...])
        k = _rope(k, cos_ref[:, :KV_WIDTH], slo_ref[:, :KV_WIDTH], shi_ref[:, :KV_WIDTH])
    q_ref[...] = (q * (1.0 / math.sqrt(HEAD_DIM))).astype(BF16)
    v = z[:, ATTN_WIDTH + KV_WIDTH:ATTN_WIDTH + 2 * KV_WIDTH]
    k_ref[...] = k
    v_ref[...] = v
    if not latent:
        seq = kt_ref.shape[2]
        for s in range(kt_ref.shape[0]):
            kt_ref[s] = k[s * seq:(s + 1) * seq, :].T
            vt_ref[s] = v[s * seq:(s + 1) * seq, :].T
    u = z[:, ATTN_WIDTH + 2 * KV_WIDTH:]
    xg_ref[...] = u[:, :CONV_CH] * jax.nn.sigmoid(u[:, CONV_CH:])


def _inproj(x2d, mod3, mod_of_seq, seq, norm1, w_in, qn, kn, gq, gk, rope):
    t = x2d.shape[0]
    latent = rope is not None
    row = lambda i: (i, 0)
    tile = INPROJ_TILE
    if seq >= tile:
        assert seq % tile == 0
        tiles_per_seq = seq // tile
        mod_row = lambda i: mod_of_seq(i // tiles_per_seq)
    else:
        assert tile % seq == 0 and not latent
        tiles_per_seq = 1
        mod_row = lambda i: mod_of_seq(i * (tile // seq))
    in_specs = [pl.BlockSpec((tile, D_MODEL), row),
                pl.BlockSpec((None, 6, D_MODEL), lambda i: (mod_row(i), 0, 0)),
                _full((1, D_MODEL)), _full((D_MODEL, IN_WIDTH)),
                _full((1, ATTN_WIDTH)), _full((1, KV_WIDTH)),
                _full((ATTN_WIDTH, ATTN_WIDTH)), _full((KV_WIDTH, KV_WIDTH))]
    args = [x2d, mod3, norm1, w_in, qn, kn, gq, gk]
    if latent:
        pos = lambda i: (i % tiles_per_seq, 0)
        in_specs += [pl.BlockSpec((tile, ATTN_WIDTH), pos)] * 3
        args += list(rope)
    out_specs = [pl.BlockSpec((tile, ATTN_WIDTH), row),
                 pl.BlockSpec((tile, KV_WIDTH), row),
                 pl.BlockSpec((tile, KV_WIDTH), row),
                 pl.BlockSpec((tile, CONV_CH), row)]
    out_shape = [jax.ShapeDtypeStruct((t, ATTN_WIDTH), BF16),
                 jax.ShapeDtypeStruct((t, KV_WIDTH), F32),
                 jax.ShapeDtypeStruct((t, KV_WIDTH), F32),
                 jax.ShapeDtypeStruct((t, CONV_CH), F32)]
    if not latent:
        out_specs += [pl.BlockSpec((tile // seq, KV_WIDTH, seq), lambda i: (i, 0, 0))] * 2
        out_shape += [jax.ShapeDtypeStruct((t // seq, KV_WIDTH, seq), F32)] * 2
    return pl.pallas_call(
        functools.partial(_inproj_kernel, latent),
        grid=(t // tile,),
        in_specs=in_specs,
        out_specs=out_specs,
        out_shape=out_shape,
        compiler_params=_params("arbitrary"),
        name="inproj_latent" if latent else "inproj_context",
    )(*args)


def _conv_kernel(tiles_per_seq, prev_ref, cur_ref, nxt_ref, dw_ref, db_ref, lg_ref, lb_ref, pw_ref,
                 o_ref, xs_ref, act_ref):
    i = pl.program_id(0)
    pos = i % tiles_per_seq
    zero = jnp.zeros((HALO, CONV_CH), F32)
    xs_ref[0, 0:HALO, :] = jnp.where(pos > 0, prev_ref[...], zero)
    xs_ref[0, HALO:HALO + ROW_TILE, :] = cur_ref[...]
    xs_ref[0, HALO + ROW_TILE:2 * HALO + ROW_TILE, :] = jnp.where(pos < tiles_per_seq - 1, nxt_ref[...], zero)
    span = ROW_TILE + 2 * HALO - 8
    for r in range(1, 8):
        xs_ref[r, 0:span, :] = xs_ref[0, r:r + span, :]
    chunk = 32

    def body(c, carry):
        base = pl.multiple_of(c * chunk, chunk)
        acc = jnp.zeros((chunk, CONV_CH), F32) + db_ref[...]
        for k in range(CONV_K):
            kk = k + HALO - CONV_PAD
            start = pl.multiple_of(base + 8 * (kk // 8), 8)
            acc = acc + xs_ref[kk % 8, pl.ds(start, chunk), :] * dw_ref[k:k + 1, :]
        act_ref[pl.ds(base, chunk), :] = acc
        return carry

    lax.fori_loop(0, ROW_TILE // chunk, body, 0)
    acc = act_ref[...]
    mu = jnp.mean(acc, axis=-1, keepdims=True)
    cen = acc - mu
    var = jnp.mean(cen * cen, axis=-1, keepdims=True)
    y = cen * lax.rsqrt(var + EPS) * lg_ref[...] + lb_ref[...]
    y = y * jax.nn.sigmoid(y)
    o_ref[...] = jnp.dot(y.astype(BF16), pw_ref[...], preferred_element_type=F32).astype(BF16)


def _conv_module(xg, tiles_per_seq, dw, db, lg, lb, pw):
    t = xg.shape[0]
    nt = t // ROW_TILE
    per = ROW_TILE // HALO
    last = t // HALO - 1
    return pl.pallas_call(
        functools.partial(_conv_kernel, tiles_per_seq),
        grid=(nt,),
        in_specs=[pl.BlockSpec((HALO, CONV_CH), lambda i: (jnp.maximum(i * per - 1, 0), 0)),
                  pl.BlockSpec((ROW_TILE, CONV_CH), lambda i: (i, 0)),
                  pl.BlockSpec((HALO, CONV_CH), lambda i: (jnp.minimum((i + 1) * per, last), 0)),
                  _full((CONV_K, CONV_CH)), _full((1, CONV_CH)), _full((1, CONV_CH)), _full((1, CONV_CH)),
                  _full((CONV_CH, CONV_CH))],
        out_specs=pl.BlockSpec((ROW_TILE, CONV_CH), lambda i: (i, 0)),
        out_shape=jax.ShapeDtypeStruct((t, CONV_CH), BF16),
        scratch_shapes=[pltpu.VMEM((8, ROW_TILE + 2 * HALO, CONV_CH), F32),
                        pltpu.VMEM((ROW_TILE, CONV_CH), F32)],
        compiler_params=_params("arbitrary"),
        name="conv_module",
    )(xg, xg, xg, dw, db, lg, lb, pw)


def _attend(qg, sink_col, parts):
    m = sink_col
    scores = []
    for keys, _, mask in parts:
        s = lax.dot_general(qg, keys, (((1,), (1,)), ((), ())), preferred_element_type=F32)
        if mask is not None:
            s = jnp.where(mask, s, NEG_INF)
        m = jnp.maximum(m, jnp.max(s, axis=1, keepdims=True))
        scores.append(s)
    denom = jnp.exp(sink_col - m)
    out = None
    for s, (_, vals, _) in zip(scores, parts):
        p = jnp.exp(s - m)
        denom = denom + jnp.sum(p, axis=1, keepdims=True)
        o = jnp.dot(p.astype(BF16), vals, preferred_element_type=F32)
        out = o if out is None else out + o
    return out / denom


def _mix_kernel(latent, seq, x_ref, mod_ref, q_ref, k_ref, v_ref, cv_ref, *rest):
    if latent:
        ck_ref, cvv_ref, sink_ref, wout_ref, n2_ref, x1_ref, h2_ref, attn_ref = rest
    else:
        sink_ref, wout_ref, n2_ref, x1_ref, h2_ref, attn_ref = rest
    prob = ROW_TILE
    width = min(seq, prob + 2 * WINDOW)
    mix_tile = x_ref.shape[0]
    steps_per_seq = max(seq // mix_tile, 1)
    for sub in range(mix_tile // prob):
        rows = slice(sub * prob, (sub + 1) * prob)
        if latent:
            q0 = (pl.program_id(0) % steps_per_seq) * mix_tile + sub * prob
            start = pl.multiple_of(jnp.clip(q0 - WINDOW, 0, seq - width), BLOCK)
            qpos = q0 + (lax.broadcasted_iota(jnp.int32, (GROUP * prob, width), 0) & (prob - 1))
            kpos = start + lax.broadcasted_iota(jnp.int32, (GROUP * prob, width), 1)
            mask = jnp.abs(kpos - qpos) <= WINDOW
        else:
            start = sub * prob
            mask = None
        for kv in range(N_KV_HEADS):
            lanes = slice(kv * HEAD_DIM, (kv + 1) * HEAD_DIM)
            parts = [(k_ref[pl.ds(start, width), lanes].astype(BF16),
                      v_ref[pl.ds(start, width), lanes].astype(BF16), mask)]
            if latent:
                parts.append((ck_ref[:, lanes].astype(BF16), cvv_ref[:, lanes].astype(BF16), None))
            sink_col = jnp.concatenate(
                [jnp.full((prob, 1), sink_ref[kv * GROUP + g], F32) for g in range(GROUP)], axis=0)
            qg = jnp.concatenate(
                [q_ref[rows, (kv * GROUP + g) * HEAD_DIM:(kv * GROUP + g + 1) * HEAD_DIM]
                 for g in range(GROUP)], axis=0)
            o = _attend(qg, sink_col, parts)
            for g in range(GROUP):
                h = kv * GROUP + g
                attn_ref[rows, h * HEAD_DIM:(h + 1) * HEAD_DIM] = o[g * prob:(g + 1) * prob].astype(BF16)
    g1 = mod_ref[2:3, :]
    sh2 = mod_ref[3:4, :]
    sc2 = mod_ref[4:5, :]
    mixed = (jnp.dot(attn_ref[...], wout_ref[0:ATTN_WIDTH, :], preferred_element_type=F32)
             + jnp.dot(cv_ref[...], wout_ref[ATTN_WIDTH:, :], preferred_element_type=F32))
    x1 = x_ref[...] + g1 * mixed
    x1_ref[...] = x1
    h2 = x1 * lax.rsqrt(jnp.mean(x1 * x1, axis=-1, keepdims=True) + EPS) * n2_ref[...]
    h2_ref[...] = (h2 * (1.0 + sc2) + sh2).astype(BF16)


def _mix(x3, mod3, mod_row, q3, k3, v3, conv3, cache, sink, w_out, norm2):
    b, seq, _ = x3.shape
    latent = cache is not None
    t = b * seq
    if latent:
        mix_tile = ROW_TILE
        assert seq % mix_tile == 0
        steps_per_seq = seq // mix_tile
        seq_of = lambda i: i // steps_per_seq
        kv_spec = pl.BlockSpec((seq, KV_WIDTH), lambda i: (seq_of(i), 0))
    else:
        mix_tile = MIX_TILE
        assert seq == ROW_TILE and t % mix_tile == 0
        seqs_per_step = mix_tile // seq
        seq_of = lambda i: i * seqs_per_step
        kv_spec = pl.BlockSpec((mix_tile, KV_WIDTH), lambda i: (i, 0))
    tile = lambda w: pl.BlockSpec((mix_tile, w), lambda i: (i, 0))
    flat = lambda a: a.reshape(t, a.shape[-1])
    in_specs = [tile(D_MODEL),
                pl.BlockSpec((None, 6, D_MODEL), lambda i: (mod_row(seq_of(i)), 0, 0)),
                tile(ATTN_WIDTH), kv_spec, kv_spec, tile(CONV_CH)]
    args = [flat(x3), mod3, flat(q3), flat(k3), flat(v3), flat(conv3)]
    if latent:
        past = cache[0].shape[1]
        in_specs += [pl.BlockSpec((None, past, KV_WIDTH), lambda i: (seq_of(i), 0, 0))] * 2
        args += list(cache)
    in_specs += [pl.BlockSpec(memory_space=pltpu.SMEM), _full((D_MODEL, D_MODEL)), _full((1, D_MODEL))]
    args += [sink, w_out, norm2]
    x1, h2 = pl.pallas_call(
        functools.partial(_mix_kernel, latent, seq),
        grid=(t // mix_tile,),
        in_specs=in_specs,
        out_specs=[tile(D_MODEL), tile(D_MODEL)],
        out_shape=[jax.ShapeDtypeStruct((t, D_MODEL), F32),
                   jax.ShapeDtypeStruct((t, D_MODEL), BF16)],
        scratch_shapes=[pltpu.VMEM((mix_tile, ATTN_WIDTH), BF16)],
        compiler_params=_params("arbitrary"),
        name="mix_latent" if latent else "mix_context",
    )(*args)
    return x1.reshape(b, seq, D_MODEL), h2.reshape(b, seq, D_MODEL)


def _merge_exchange_pairs(n):
    pairs, p = [], 1
    while p < n:
        k = p
        while k >= 1:
            for j in range(k % p, n - k, 2 * k):
                for i in range(min(k, n - j - k)):
                    if (i + j) // (2 * p) == (i + j + k) // (2 * p):
                        pairs.append((i + j, i + j + k))
            k //= 2
        p *= 2
    return pairs


def _top_desc_sorted(x, count):
    lists = [x[8 * g:8 * (g + 1)] for g in range(count)]
    for a, b in _merge_exchange_pairs(count):
        lists[a], lists[b] = jnp.maximum(lists[a], lists[b]), jnp.minimum(lists[a], lists[b])
    slot = lax.broadcasted_iota(jnp.int32, (count, x.shape[1]), 0)
    out = jnp.zeros((count, x.shape[1]), F32)
    for k in range(count):
        m = jnp.max(lists[0], axis=0, keepdims=True)
        out = jnp.where(slot == k, m, out)
        hit = lists[0] == m
        for lvl in range(count - 1 - k):
            lists[lvl] = jnp.where(hit, lists[lvl + 1], lists[lvl])
    return out


def _top_pair_sums(v1, v2, count):
    cols = v1.shape[1]
    row = lax.broadcasted_iota(jnp.int32, (8, cols), 0)
    lists = [jnp.where(row < count // (k + 1), v1[0:8] + v2[k:k + 1], NEG_INF) for k in range(count)]
    tail = v1[8:count] + v2[0:1]
    slot = lax.broadcasted_iota(jnp.int32, (count, cols), 0)
    out = jnp.zeros((count, cols), F32)
    for n in range(count):
        m = jnp.max(jnp.maximum(lists[0], tail), axis=0, keepdims=True)
        out = jnp.where(slot == n, m, out)
        hit = lists[0] == m
        for lvl in range(count - 1 - n):
            lists[lvl] = jnp.where(hit, lists[lvl + 1], lists[lvl])
        tail = jnp.where(tail == m, NEG_INF, tail)
    return out


def _route_kernel(h2_ref, wq_ref, keys_ref, h2t_ref, rank_ref, p2_ref, cnt_ref, p1_ref, qp_ref):
    h2 = h2_ref[...]
    h2t_ref[...] = pltpu.bitcast(h2.astype(F32).T.astype(BF16), jnp.uint32)
    qp_ref[...] = jnp.dot(h2, wq_ref[...], preferred_element_type=F32).astype(BF16)
    nt = (((1,), (1,)), ((), ()))

    def head(h, carry):
        col = pl.multiple_of(h * PEER_QDIM, PEER_QDIM)
        s1 = lax.dot_general(keys_ref[h, 0], qp_ref[:, pl.ds(col, PEER_HALF)], nt,
                             preferred_element_type=F32)
        s2 = lax.dot_general(keys_ref[h, 1], qp_ref[:, pl.ds(col + PEER_HALF, PEER_HALF)], nt,
                             preferred_element_type=F32)
        v1 = _top_desc_sorted(s1, PEER_TOPK)
        v2 = _top_desc_sorted(s2, PEER_TOPK)
        best = _top_pair_sums(v1, v2, PEER_TOPK)
        tau = best[PEER_TOPK - 1:PEER_TOPK]
        z = jnp.sum(jnp.exp(best - best[0:1]), axis=0, keepdims=True)
        rank = jnp.zeros_like(s2)
        cnt_top = jnp.zeros_like(v1)
        for k in range(PEER_TOPK):
            vk = v2[k:k + 1]
            rank = jnp.where(vk > s2, k + 1.0, rank)
            cnt_top = jnp.where(v1 + vk >= tau, k + 1.0, cnt_top)
        cnt = jnp.zeros_like(s1)
        for j in range(PEER_TOPK):
            cnt = jnp.where(s1 == v1[j:j + 1], cnt_top[j:j + 1], cnt)
        steps = N_EXPERTS // PEER_EXPERTS
        cnt_ref[:, h] = cnt.reshape(steps, N_KEYS // steps, ROUTE_TILE)
        p1_ref[:, h] = jnp.exp(s1 - v1[0:1]).reshape(steps, N_KEYS // steps, ROUTE_TILE)
        rank_ref[h] = pltpu.bitcast(rank.astype(BF16), jnp.uint32)
        p2_ref[h] = pltpu.bitcast((jnp.exp(s2 - v2[0:1]) * (0.5 / z)).astype(BF16), jnp.uint32)
        return carry

    lax.fori_loop(0, PEER_HEADS, head, 0, unroll=2)


def _route(h2, wq, keys):
    t = h2.shape[0]
    tok4 = pl.BlockSpec((PEER_HEADS, N_KEYS // 2, ROUTE_TILE), lambda i: (0, 0, i))
    steps = N_EXPERTS // PEER_EXPERTS
    tok3 = pl.BlockSpec((steps, PEER_HEADS, N_KEYS // steps, ROUTE_TILE), lambda i: (0, 0, 0, i))
    return pl.pallas_call(
        _route_kernel,
        grid=(t // ROUTE_TILE,),
        in_specs=[pl.BlockSpec((ROUTE_TILE, D_MODEL), lambda i: (i, 0)),
                  _full((D_MODEL, PEER_HEADS * PEER_QDIM)),
                  _full((PEER_HEADS, 2, N_KEYS, PEER_HALF))],
        out_specs=[pl.BlockSpec((D_MODEL // 2, ROUTE_TILE), lambda i: (0, i)), tok4, tok4, tok3, tok3],
        out_shape=[jax.ShapeDtypeStruct((D_MODEL // 2, t), jnp.uint32),
                   jax.ShapeDtypeStruct((PEER_HEADS, N_KEYS // 2, t), jnp.uint32),
                   jax.ShapeDtypeStruct((PEER_HEADS, N_KEYS // 2, t), jnp.uint32),
                   jax.ShapeDtypeStruct((steps, PEER_HEADS, N_KEYS // steps, t), F32),
                   jax.ShapeDtypeStruct((steps, PEER_HEADS, N_KEYS // steps, t), F32)],
        scratch_shapes=[pltpu.VMEM((ROUTE_TILE, PEER_HEADS * PEER_QDIM), BF16)],
        compiler_params=_params("arbitrary"),
        name="peer_route",
    )(h2, wq, keys)


def _pack_kernel(transpose, x_ref, o_ref):
    x = x_ref[...]
    if transpose:
        x = x.T
    o_ref[...] = pltpu.bitcast(x.astype(BF16), jnp.uint32)


def _pack_table(table, transpose):
    rows, cols = table.shape
    blk = 512
    if transpose:
        out_spec = pl.BlockSpec((cols // 2, blk), lambda i: (0, i))
        out_shape = jax.ShapeDtypeStruct((cols // 2, rows), jnp.uint32)
    else:
        out_spec = pl.BlockSpec((blk // 2, cols), lambda i: (i, 0))
        out_shape = jax.ShapeDtypeStruct((rows // 2, cols), jnp.uint32)
    return pl.pallas_call(
        functools.partial(_pack_kernel, transpose),
        grid=(rows // blk,),
        in_specs=[pl.BlockSpec((blk, cols), lambda i: (i, 0))],
        out_specs=out_spec,
        out_shape=out_shape,
        compiler_params=_params("arbitrary"),
        name="pack_table_t" if transpose else "pack_table",
    )(table)


def _gate_rows(cs, act_ref, wt_ref, rank_ref, p2_ref, cnt_ref, p1_ref):
    tiles = (N_KEYS // BF16_SUBLANES, BF16_SUBLANES, LANES)
    for lt in range(PEER_TOKENS // LANES):
        lanes = slice(lt * LANES, (lt + 1) * LANES)
        gates = [jnp.zeros(tiles, BF16) for _ in cs]
        for h in range(PEER_HEADS):
            rank = pltpu.bitcast(rank_ref[h, :, lanes], BF16).reshape(tiles)
            p2 = pltpu.bitcast(p2_ref[h, :, lanes], BF16).reshape(tiles)
            for n, c in enumerate(cs):
                cnt = jnp.broadcast_to(cnt_ref[h, c:c + 1, lanes], (BF16_SUBLANES, LANES)).astype(BF16)
                p1 = jnp.broadcast_to(p1_ref[h, c:c + 1, lanes], (BF16_SUBLANES, LANES)).astype(BF16)
                gates[n] = gates[n] + jnp.where(rank < cnt[None], p2, jnp.zeros((), BF16)) * p1[None]
        for n, c in enumerate(cs):
            rows = slice(c * N_KEYS, (c + 1) * N_KEYS)
            a = act_ref[rows, lanes]
            gl = a + a * lax.erf(a * (1.0 / math.sqrt(2.0)))
            wt_ref[rows, lanes] = gl.astype(BF16) * gates[n].reshape(N_KEYS, LANES)


def _expert_kernel(pairs_per_block, u_ref, vt_ref, h2t_ref, rank_a, p2_a, rank_b, p2_b, cnt_a, p1_a,
                   cnt_b, p1_b, x1_ref, mod_ref, y_ref, acc_ref, act_a, act_b, wt_a, wt_b):
    g = pl.program_id(0)

    @pl.when(g == 0)
    def _():
        acc_ref[...] = jnp.zeros_like(acc_ref)
        act_b[...] = jnp.zeros_like(act_b)
        wt_a[...] = jnp.zeros_like(wt_a)

    key_rows = PEER_EXPERTS // N_KEYS

    def half_step(half, wt_old, act_old, wt_new, act_new, rank_ref, p2_ref, cnt_ref, p1_ref):
        regions = 2
        gate_rows = key_rows // regions
        for idx in range(regions):
            @pl.when(g >= -(half * regions + idx))
            def _(idx=idx):
                for c in range(idx * gate_rows, (idx + 1) * gate_rows, GATE_ROWS):
                    _gate_rows(range(c, c + GATE_ROWS), act_old, wt_new, rank_ref, p2_ref, cnt_ref, p1_ref)
                for part in range(DOT_SPLIT):
                    if idx == 0:
                        m = D_MODEL // DOT_SPLIT
                        vt = pltpu.bitcast(vt_ref[part * m // 2:(part + 1) * m // 2,
                                                  half * PEER_EXPERTS:(half + 1) * PEER_EXPERTS], BF16)
                        acc_ref[part * m:(part + 1) * m, :] += jnp.dot(vt, wt_old[...],
                                                                       preferred_element_type=F32)
                    else:
                        m = PEER_EXPERTS // DOT_SPLIT
                        lo = (half * PEER_EXPERTS + part * m) // 2
                        u = pltpu.bitcast(u_ref[lo:lo + m // 2, :], BF16)
                        act_new[part * m:(part + 1) * m, :] = jnp.dot(u, pltpu.bitcast(h2t_ref[...], BF16),
                                                                      preferred_element_type=F32)

    half_step(0, wt_a, act_b, wt_b, act_a, rank_a, p2_a, cnt_a, p1_a)
    half_step(1, wt_b, act_a, wt_a, act_b, rank_b, p2_b, cnt_b, p1_b)

    @pl.when((g > 0) & (g % pairs_per_block == 0))
    def _():
        g2 = mod_ref[5:6, :]
        y_ref[...] = x1_ref[...] + g2 * acc_ref[...].T
        acc_ref[...] = jnp.zeros_like(acc_ref)


def _experts(u_bf, vt_bf, h2t, rank, p2, cnt, p1, x1, mod3, mod_row):
    t = x1.shape[0]
    ppb = N_EXPERTS // (2 * PEER_EXPERTS)
    steps = (t // PEER_TOKENS) * ppb + 1
    cur = lambda g: jnp.minimum(g, steps - 2)
    prv = lambda g: jnp.maximum(g - 1, 0)
    packed = lambda f: pl.BlockSpec((PEER_HEADS, N_KEYS // 2, PEER_TOKENS), lambda g: (0, 0, f(g) // ppb))
    rows = lambda f, half: pl.BlockSpec((None, PEER_HEADS, PEER_EXPERTS // N_KEYS, PEER_TOKENS),
                                        lambda g: (2 * (f(g) % ppb) + half, 0, 0, f(g) // ppb))
    return pl.pallas_call(
        functools.partial(_expert_kernel, ppb),
        grid=(steps,),
        in_specs=[pl.BlockSpec((PEER_EXPERTS, D_MODEL), lambda g: (cur(g) % ppb, 0)),
                  pl.BlockSpec((D_MODEL // 2, 2 * PEER_EXPERTS), lambda g: (0, prv(g) % ppb)),
                  pl.BlockSpec((D_MODEL // 2, PEER_TOKENS), lambda g: (0, cur(g) // ppb)),
                  packed(prv), packed(prv), packed(cur), packed(cur),
                  rows(prv, 1), rows(prv, 1), rows(cur, 0), rows(cur, 0),
                  pl.BlockSpec((PEER_TOKENS, D_MODEL), lambda g: (prv(g) // ppb, 0)),
                  pl.BlockSpec((None, 6, D_MODEL), lambda g: (mod_row(prv(g) // ppb), 0, 0))],
        out_specs=pl.BlockSpec((PEER_TOKENS, D_MODEL), lambda g: (prv(g) // ppb, 0)),
        out_shape=jax.ShapeDtypeStruct((t, D_MODEL), F32),
        scratch_shapes=[pltpu.VMEM((D_MODEL, PEER_TOKENS), F32),
                        pltpu.VMEM((PEER_EXPERTS, PEER_TOKENS), F32),
                        pltpu.VMEM((PEER_EXPERTS, PEER_TOKENS), F32),
                        pltpu.VMEM((PEER_EXPERTS, PEER_TOKENS), BF16),
                        pltpu.VMEM((PEER_EXPERTS, PEER_TOKENS), BF16)],
        compiler_params=_params("arbitrary"),
        name="peer_experts",
    )(u_bf, vt_bf, h2t, rank, p2, rank, p2, cnt, p1, cnt, p1, x1, mod3)


def _rope_tables(seq_len):
    rows = seq_len // GRID_W
    row = jnp.repeat(jnp.arange(rows, dtype=F32), GRID_W)
    col = jnp.tile(jnp.arange(GRID_W, dtype=F32), rows)
    inv = ROPE_THETA ** (-jnp.arange(AXIS_FREQS, dtype=F32) / AXIS_FREQS)
    ang_row = row[:, None] * inv
    ang_col = col[:, None] * inv
    zeros = jnp.zeros_like(ang_row)
    cos = jnp.concatenate([jnp.cos(ang_row)] * 2 + [jnp.cos(ang_col)] * 2, axis=-1)
    sin_lo = jnp.concatenate([-jnp.sin(ang_row), zeros, -jnp.sin(ang_col), zeros], axis=-1)
    sin_hi = jnp.concatenate([zeros, jnp.sin(ang_row), zeros, jnp.sin(ang_col)], axis=-1)
    return tuple(jnp.tile(t, (1, N_HEADS)) for t in (cos, sin_lo, sin_hi))


def _head_mean_matrix(width):
    idx = jnp.arange(width) // HEAD_DIM
    return jnp.where(idx[:, None] == idx[None, :], 1.0 / HEAD_DIM, 0.0).astype(BF16)


def _trunk(x3, mod3, mod_of_seq, cache, w, rope):
    b, seq, _ = x3.shape
    t = b * seq
    tiles_per_seq = seq // ROW_TILE
    q, k, v, xg, *cache_out = _inproj(x3.reshape(t, D_MODEL), mod3, mod_of_seq, seq, w["norm1"], w["w_in"],
                                      w["qn"], w["kn"], w["gq"], w["gk"], rope)
    conv = _conv_module(xg, tiles_per_seq, w["conv_dw"], w["conv_dw_b"], w["conv_ln_g"], w["conv_ln_b"],
                        w["conv_pw"])
    x1, h2 = _mix(x3, mod3, mod_of_seq, q.reshape(b, seq, ATTN_WIDTH), k.reshape(b, seq, KV_WIDTH),
                  v.reshape(b, seq, KV_WIDTH), conv.reshape(b, seq, CONV_CH), cache, w["sink"],
                  w["w_out"], w["norm2"])
    h2t, rank, p2, cnt, p1 = _route(h2.reshape(t, D_MODEL), w["peer_wq"], w["peer_keys"])
    per_seq = seq // PEER_TOKENS
    if per_seq == 0:
        seqs_per_block = PEER_TOKENS // seq
        row_of_block = lambda i: mod_of_seq(i * seqs_per_block)
    else:
        row_of_block = lambda i: mod_of_seq(i // per_seq)
    y = _experts(w["peer_u"], w["peer_vt"], h2t, rank, p2, cnt, p1, x1.reshape(t, D_MODEL), mod3, row_of_block)
    return y.reshape(b, seq, D_MODEL), cache_out


def kernel(x_prompt, x_sample, cache_k, cache_v, c, c_ctx, norm1, norm2, w_ada, b_ada, w_in, w_out,
           q_norm, k_norm, attn_sink, conv_dw, conv_dw_b, conv_ln_g, conv_ln_b, conv_pw,
           peer_wq, peer_keys, peer_u, peer_v):
    depth = norm1.shape[0]
    batch, seq, _ = x_prompt.shape
    dec_batch, dec_seq, _ = x_sample.shape
    past = cache_k.shape[2]
    assert dec_batch + 1 <= 8 and depth >= 1
    cond8 = jnp.concatenate([c_ctx[None, :], c, jnp.zeros((8 - 1 - dec_batch, D_MODEL), F32)], axis=0)
    rope = _rope_tables(dec_seq)
    xp, xs = x_prompt, x_sample
    new_ks, new_vs = [], []
    for l in range(depth):
        w = dict(
            norm1=norm1[l][None], norm2=norm2[l][None],
            w_in=w_in[l].astype(BF16), w_out=w_out[l].astype(BF16),
            qn=jnp.tile(q_norm[l], N_HEADS)[None], kn=jnp.tile(k_norm[l], N_KV_HEADS)[None],
            gq=_head_mean_matrix(ATTN_WIDTH), gk=_head_mean_matrix(KV_WIDTH),
            sink=attn_sink[l],
            conv_dw=conv_dw[l], conv_dw_b=conv_dw_b[l][None], conv_ln_g=conv_ln_g[l][None],
            conv_ln_b=conv_ln_b[l][None], conv_pw=conv_pw[l].astype(BF16),
            peer_wq=peer_wq[l].astype(BF16), peer_keys=peer_keys[l].astype(BF16),
            peer_u=_pack_table(peer_u[l], False), peer_vt=_pack_table(peer_v[l], True),
        )
        mod3 = _modulation(cond8, w_ada[l], b_ada[l]).reshape(8, 6, D_MODEL)
        xp, (kt_c, vt_c) = _trunk(xp, mod3, lambda s: 0, None, w, None)
        new_ks.append(kt_c.reshape(batch, N_KV_HEADS, HEAD_DIM, seq).transpose(0, 3, 1, 2))
        new_vs.append(vt_c.reshape(batch, N_KV_HEADS, HEAD_DIM, seq).transpose(0, 3, 1, 2))
        cache = (cache_k[:, l].reshape(dec_batch, past, KV_WIDTH), cache_v[:, l].reshape(dec_batch, past, KV_WIDTH))
        xs, _ = _trunk(xs, mod3, lambda s: s + 1, cache, w, rope)
    join = (lambda parts: parts[0][:, None]) if depth == 1 else (lambda parts: jnp.stack(parts, axis=1))
    return (xp, xs, join(new_ks), join(new_vs))
```

```python
import functools
import math

import jax
import jax.numpy as jnp
from jax import lax
from jax.experimental import pallas as pl
from jax.experimental.pallas import tpu as pltpu

F32 = jnp.float32
BF16 = jnp.bfloat16

D_MODEL = 1024
GRID_W = 64
N_HEADS = 8
N_KV_HEADS = 2
GROUP = N_HEADS // N_KV_HEADS
HEAD_DIM = 64
ATTN_WIDTH = N_HEADS * HEAD_DIM
KV_WIDTH = N_KV_HEADS * HEAD_DIM
WINDOW = 128
BLOCK = 128
ROPE_THETA = 10000.0
AXIS_DIM = HEAD_DIM // 2
AXIS_FREQS = AXIS_DIM // 2
CONV_CH = D_MODEL - ATTN_WIDTH
CONV_K = 31
CONV_PAD = CONV_K // 2
IN_WIDTH = ATTN_WIDTH + 2 * KV_WIDTH + 2 * CONV_CH
N_KEYS = 128
N_EXPERTS = N_KEYS * N_KEYS
PEER_HEADS = 8
PEER_QDIM = 256
PEER_HALF = PEER_QDIM // 2
PEER_TOPK = 16
EPS = 1e-6

LANES = 128
BF16_SUBLANES = 16
ROW_TILE = 256
INPROJ_TILE = 512
MIX_TILE = 512
HALO = 16
ROUTE_TILE = 512
PEER_TOKENS = 512
PEER_EXPERTS = 1024
GATE_ROWS = 4
DOT_SPLIT = 1
VMEM_LIMIT = 48 * 1024 * 1024

NEG_INF = float("-inf")


def _params(*sem, flags=None):
    return pltpu.CompilerParams(dimension_semantics=sem, vmem_limit_bytes=VMEM_LIMIT, flags=flags)


def _full(shape):
    return pl.BlockSpec(shape, lambda *_: (0,) * len(shape))


def _mod_kernel(cond_ref, w_ref, b_ref, o_ref):
    cnd = cond_ref[...]
    act = cnd * jax.nn.sigmoid(cnd)
    o_ref[...] = jnp.dot(act, w_ref[...], precision=lax.Precision.HIGHEST,
                         preferred_element_type=F32) + b_ref[...]


def _modulation(cond8, w_ada, b_ada):
    n = w_ada.shape[1]
    nb = n // D_MODEL
    return pl.pallas_call(
        _mod_kernel,
        grid=(nb,),
        in_specs=[_full((8, D_MODEL)),
                  pl.BlockSpec((D_MODEL, D_MODEL), lambda j: (0, j)),
                  pl.BlockSpec((1, D_MODEL), lambda j: (0, j))],
        out_specs=pl.BlockSpec((8, D_MODEL), lambda j: (0, j)),
        out_shape=jax.ShapeDtypeStruct((8, n), F32),
        compiler_params=_params("arbitrary"),
        name="modulation",
    )(cond8, w_ada, b_ada.reshape(1, n))


def _group_rms(t, gmat, gain):
    ms = jnp.dot((t * t).astype(BF16), gmat, preferred_element_type=F32)
    return t * lax.rsqrt(ms + EPS) * gain


def _rope(t, cos, sin_lo, sin_hi):
    w = t.shape[1]
    return (t * cos + pltpu.roll(t, w - AXIS_FREQS, 1) * sin_lo
            + pltpu.roll(t, AXIS_FREQS, 1) * sin_hi)


def _inproj_kernel(latent, x_ref, mod_ref, n1_ref, win_ref, qn_ref, kn_ref, gq_ref, gk_ref, *rest):
    if latent:
        cos_ref, slo_ref, shi_ref, q_ref, k_ref, v_ref, xg_ref = rest
    else:
        q_ref, k_ref, v_ref, xg_ref, kt_ref, vt_ref = rest
    x = x_ref[...]
    sh1 = mod_ref[0:1, :]
    sc1 = mod_ref[1:2, :]
    h = x * lax.rsqrt(jnp.mean(x * x, axis=-1, keepdims=True) + EPS) * n1_ref[...]
    h = h * (1.0 + sc1) + sh1
    z = jnp.dot(h.astype(BF16), win_ref[...], preferred_element_type=F32)
    q = _group_rms(z[:, :ATTN_WIDTH], gq_ref[...], qn_ref[...])
    k = _group_rms(z[:, ATTN_WIDTH:ATTN_WIDTH + KV_WIDTH], gk_ref[...], kn_ref[...])
    if latent:
        q = _rope(q, cos_ref[...], slo_ref[...], shi_ref[...])
        k = _rope(k, cos_ref[:, :KV_WIDTH], slo_ref[:, :KV_WIDTH], shi_ref[:, :KV_WIDTH])
    q_ref[...] = (q * (1.0 / math.sqrt(HEAD_DIM))).astype(BF16)
    v = z[:, ATTN_WIDTH + KV_WIDTH:ATTN_WIDTH + 2 * KV_WIDTH]
    k_ref[...] = k
    v_ref[...] = v
    if not latent:
        seq = kt_ref.shape[2]
        for s in range(kt_ref.shape[0]):
            kt_ref[s] = k[s * seq:(s + 1) * seq, :].T
            vt_ref[s] = v[s * seq:(s + 1) * seq, :].T
    u = z[:, ATTN_WIDTH + 2 * KV_WIDTH:]
    xg_ref[...] = u[:, :CONV_CH] * jax.nn.sigmoid(u[:, CONV_CH:])


def _inproj(x2d, mod3, mod_of_seq, seq, norm1, w_in, qn, kn, gq, gk, rope):
    t = x2d.shape[0]
    latent = rope is not None
    row = lambda i: (i, 0)
    tile = INPROJ_TILE
    if seq >= tile:
        assert seq % tile == 0
        tiles_per_seq = seq // tile
        mod_row = lambda i: mod_of_seq(i // tiles_per_seq)
    else:
        assert tile % seq == 0 and not latent
        tiles_per_seq = 1
        mod_row = lambda i: mod_of_seq(i * (tile // seq))
    in_specs = [pl.BlockSpec((tile, D_MODEL), row),
                pl.BlockSpec((None, 6, D_MODEL), lambda i: (mod_row(i), 0, 0)),
                _full((1, D_MODEL)), _full((D_MODEL, IN_WIDTH)),
                _full((1, ATTN_WIDTH)), _full((1, KV_WIDTH)),
                _full((ATTN_WIDTH, ATTN_WIDTH)), _full((KV_WIDTH, KV_WIDTH))]
    args = [x2d, mod3, norm1, w_in, qn, kn, gq, gk]
    if latent:
        pos = lambda i: (i % tiles_per_seq, 0)
        in_specs += [pl.BlockSpec((tile, ATTN_WIDTH), pos)] * 3
        args += list(rope)
    out_specs = [pl.BlockSpec((tile, ATTN_WIDTH), row),
                 pl.BlockSpec((tile, KV_WIDTH), row),
                 pl.BlockSpec((tile, KV_WIDTH), row),
                 pl.BlockSpec((tile, CONV_CH), row)]
    out_shape = [jax.ShapeDtypeStruct((t, ATTN_WIDTH), BF16),
                 jax.ShapeDtypeStruct((t, KV_WIDTH), F32),
                 jax.ShapeDtypeStruct((t, KV_WIDTH), F32),
                 jax.ShapeDtypeStruct((t, CONV_CH), F32)]
    if not latent:
        out_specs += [pl.BlockSpec((tile // seq, KV_WIDTH, seq), lambda i: (i, 0, 0))] * 2
        out_shape += [jax.ShapeDtypeStruct((t // seq, KV_WIDTH, seq), F32)] * 2
    return pl.pallas_call(
        functools.partial(_inproj_kernel, latent),
        grid=(t // tile,),
        in_specs=in_specs,
        out_specs=out_specs,
        out_shape=out_shape,
        compiler_params=_params("arbitrary"),
        name="inproj_latent" if latent else "inproj_context",
    )(*args)


def _conv_kernel(tiles_per_seq, prev_ref, cur_ref, nxt_ref, dw_ref, db_ref, lg_ref, lb_ref, pw_ref,
                 o_ref, xs_ref, act_ref):
    i = pl.program_id(0)
    pos = i % tiles_per_seq
    zero = jnp.zeros((HALO, CONV_CH), F32)
    xs_ref[0, 0:HALO, :] = jnp.where(pos > 0, prev_ref[...], zero)
    xs_ref[0, HALO:HALO + ROW_TILE, :] = cur_ref[...]
    xs_ref[0, HALO + ROW_TILE:2 * HALO + ROW_TILE, :] = jnp.where(pos < tiles_per_seq - 1, nxt_ref[...], zero)
    span = ROW_TILE + 2 * HALO - 8
    for r in range(1, 8):
        xs_ref[r, 0:span, :] = xs_ref[0, r:r + span, :]
    chunk = 32

    def body(c, carry):
        base = pl.multiple_of(c * chunk, chunk)
        acc = jnp.zeros((chunk, CONV_CH), F32) + db_ref[...]
        for k in range(CONV_K):
            kk = k + HALO - CONV_PAD
            start = pl.multiple_of(base + 8 * (kk // 8), 8)
            acc = acc + xs_ref[kk % 8, pl.ds(start, chunk), :] * dw_ref[k:k + 1, :]
        act_ref[pl.ds(base, chunk), :] = acc
        return carry

    lax.fori_loop(0, ROW_TILE // chunk, body, 0)
    acc = act_ref[...]
    mu = jnp.mean(acc, axis=-1, keepdims=True)
    cen = acc - mu
    var = jnp.mean(cen * cen, axis=-1, keepdims=True)
    y = cen * lax.rsqrt(var + EPS) * lg_ref[...] + lb_ref[...]
    y = y * jax.nn.sigmoid(y)
    o_ref[...] = jnp.dot(y.astype(BF16), pw_ref[...], preferred_element_type=F32).astype(BF16)


def _conv_module(xg, tiles_per_seq, dw, db, lg, lb, pw):
    t = xg.shape[0]
    nt = t // ROW_TILE
    per = ROW_TILE // HALO
    last = t // HALO - 1
    return pl.pallas_call(
        functools.partial(_conv_kernel, tiles_per_seq),
        grid=(nt,),
        in_specs=[pl.BlockSpec((HALO, CONV_CH), lambda i: (jnp.maximum(i * per - 1, 0), 0)),
                  pl.BlockSpec((ROW_TILE, CONV_CH), lambda i: (i, 0)),
                  pl.BlockSpec((HALO, CONV_CH), lambda i: (jnp.minimum((i + 1) * per, last), 0)),
                  _full((CONV_K, CONV_CH)), _full((1, CONV_CH)), _full((1, CONV_CH)), _full((1, CONV_CH)),
                  _full((CONV_CH, CONV_CH))],
        out_specs=pl.BlockSpec((ROW_TILE, CONV_CH), lambda i: (i, 0)),
        out_shape=jax.ShapeDtypeStruct((t, CONV_CH), BF16),
        scratch_shapes=[pltpu.VMEM((8, ROW_TILE + 2 * HALO, CONV_CH), F32),
                        pltpu.VMEM((ROW_TILE, CONV_CH), F32)],
        compiler_params=_params("arbitrary"),
        name="conv_module",
    )(xg, xg, xg, dw, db, lg, lb, pw)


def _attend(qg, sink_col, parts):
    m = sink_col
    scores = []
    for keys, _, mask in parts:
        s = lax.dot_general(qg, keys, (((1,), (1,)), ((), ())), preferred_element_type=F32)
        if mask is not None:
            s = jnp.where(mask, s, NEG_INF)
        m = jnp.maximum(m, jnp.max(s, axis=1, keepdims=True))
        scores.append(s)
    out = None
    for s, (_, vals, _) in zip(scores, parts):
        p = jnp.exp(s - m)
        ext = jnp.concatenate([vals, jnp.ones_like(vals)], axis=1)
        o = jnp.dot(p.astype(BF16), ext, preferred_element_type=F32)
        out = o if out is None else out + o
    denom = jnp.exp(sink_col - m) + out[:, HEAD_DIM:HEAD_DIM + 1]
    return out[:, :HEAD_DIM] / denom


def _mix_kernel(latent, seq, x_ref, mod_ref, q_ref, k_ref, v_ref, cv_ref, *rest):
    if latent:
        ck_ref, cvv_ref, sink_ref, wout_ref, n2_ref, x1_ref, h2_ref, attn_ref = rest
    else:
        sink_ref, wout_ref, n2_ref, x1_ref, h2_ref, attn_ref = rest
    prob = ROW_TILE
    width = min(seq, prob + 2 * WINDOW)
    mix_tile = x_ref.shape[0]
    steps_per_seq = max(seq // mix_tile, 1)
    for sub in range(mix_tile // prob):
        rows = slice(sub * prob, (sub + 1) * prob)
        if latent:
            q0 = (pl.program_id(0) % steps_per_seq) * mix_tile + sub * prob
            start = pl.multiple_of(jnp.clip(q0 - WINDOW, 0, seq - width), BLOCK)
            qpos = q0 + (lax.broadcasted_iota(jnp.int32, (GROUP * prob, width), 0) & (prob - 1))
            kpos = start + lax.broadcasted_iota(jnp.int32, (GROUP * prob, width), 1)
            mask = jnp.abs(kpos - qpos) <= WINDOW
        else:
            start = sub * prob
            mask = None
        for kv in range(N_KV_HEADS):
            lanes = slice(kv * HEAD_DIM, (kv + 1) * HEAD_DIM)
            parts = [(k_ref[pl.ds(start, width), lanes].astype(BF16),
                      v_ref[pl.ds(start, width), lanes].astype(BF16), mask)]
            if latent:
                parts.append((ck_ref[:, lanes].astype(BF16), cvv_ref[:, lanes].astype(BF16), None))
            sink_col = jnp.concatenate(
                [jnp.full((prob, 1), sink_ref[kv * GROUP + g], F32) for g in range(GROUP)], axis=0)
            qg = jnp.concatenate(
                [q_ref[rows, (kv * GROUP + g) * HEAD_DIM:(kv * GROUP + g + 1) * HEAD_DIM]
                 for g in range(GROUP)], axis=0)
            o = _attend(qg, sink_col, parts)
            for g in range(GROUP):
                h = kv * GROUP + g
                attn_ref[rows, h * HEAD_DIM:(h + 1) * HEAD_DIM] = o[g * prob:(g + 1) * prob].astype(BF16)
    g1 = mod_ref[2:3, :]
    sh2 = mod_ref[3:4, :]
    sc2 = mod_ref[4:5, :]
    mixed = (jnp.dot(attn_ref[...], wout_ref[0:ATTN_WIDTH, :], preferred_element_type=F32)
             + jnp.dot(cv_ref[...], wout_ref[ATTN_WIDTH:, :], preferred_element_type=F32))
    x1 = x_ref[...] + g1 * mixed
    x1_ref[...] = x1
    h2 = x1 * lax.rsqrt(jnp.mean(x1 * x1, axis=-1, keepdims=True) + EPS) * n2_ref[...]
    h2_ref[...] = (h2 * (1.0 + sc2) + sh2).astype(BF16)


def _mix(x3, mod3, mod_row, q3, k3, v3, conv3, cache, sink, w_out, norm2):
    b, seq, _ = x3.shape
    latent = cache is not None
    t = b * seq
    if latent:
        mix_tile = ROW_TILE
        assert seq % mix_tile == 0
        steps_per_seq = seq // mix_tile
        seq_of = lambda i: i // steps_per_seq
        kv_spec = pl.BlockSpec((seq, KV_WIDTH), lambda i: (seq_of(i), 0))
    else:
        mix_tile = MIX_TILE
        assert seq == ROW_TILE and t % mix_tile == 0
        seqs_per_step = mix_tile // seq
        seq_of = lambda i: i * seqs_per_step
        kv_spec = pl.BlockSpec((mix_tile, KV_WIDTH), lambda i: (i, 0))
    tile = lambda w: pl.BlockSpec((mix_tile, w), lambda i: (i, 0))
    flat = lambda a: a.reshape(t, a.shape[-1])
    in_specs = [tile(D_MODEL),
                pl.BlockSpec((None, 6, D_MODEL), lambda i: (mod_row(seq_of(i)), 0, 0)),
                tile(ATTN_WIDTH), kv_spec, kv_spec, tile(CONV_CH)]
    args = [flat(x3), mod3, flat(q3), flat(k3), flat(v3), flat(conv3)]
    if latent:
        past = cache[0].shape[1]
        in_specs += [pl.BlockSpec((None, past, KV_WIDTH), lambda i: (seq_of(i), 0, 0))] * 2
        args += list(cache)
    in_specs += [pl.BlockSpec(memory_space=pltpu.SMEM), _full((D_MODEL, D_MODEL)), _full((1, D_MODEL))]
    args += [sink, w_out, norm2]
    x1, h2 = pl.pallas_call(
        functools.partial(_mix_kernel, latent, seq),
        grid=(t // mix_tile,),
        in_specs=in_specs,
        out_specs=[tile(D_MODEL), tile(D_MODEL)],
        out_shape=[jax.ShapeDtypeStruct((t, D_MODEL), F32),
                   jax.ShapeDtypeStruct((t, D_MODEL), BF16)],
        scratch_shapes=[pltpu.VMEM((mix_tile, ATTN_WIDTH), BF16)],
        compiler_params=_params("arbitrary"),
        name="mix_latent" if latent else "mix_context",
    )(*args)
    return x1.reshape(b, seq, D_MODEL), h2.reshape(b, seq, D_MODEL)


def _merge_exchange_pairs(n):
    pairs, p = [], 1
    while p < n:
        k = p
        while k >= 1:
            for j in range(k % p, n - k, 2 * k):
                for i in range(min(k, n - j - k)):
                    if (i + j) // (2 * p) == (i + j + k) // (2 * p):
                        pairs.append((i + j, i + j + k))
            k //= 2
        p *= 2
    return pairs


def _top_desc_sorted(x, count):
    lists = [x[8 * g:8 * (g + 1)] for g in range(count)]
    for a, b in _merge_exchange_pairs(count):
        lists[a], lists[b] = jnp.maximum(lists[a], lists[b]), jnp.minimum(lists[a], lists[b])
    slot = lax.broadcasted_iota(jnp.int32, (count, x.shape[1]), 0)
    out = jnp.zeros((count, x.shape[1]), F32)
    for k in range(count):
        m = jnp.max(lists[0], axis=0, keepdims=True)
        out = jnp.where(slot == k, m, out)
        hit = lists[0] == m
        for lvl in range(count - 1 - k):
            lists[lvl] = jnp.where(hit, lists[lvl + 1], lists[lvl])
    return out


def _top_pair_sums(v1, v2, count):
    cols = v1.shape[1]
    row = lax.broadcasted_iota(jnp.int32, (8, cols), 0)
    lists = [jnp.where(row < count // (k + 1), v1[0:8] + v2[k:k + 1], NEG_INF) for k in range(count)]
    tail = v1[8:count] + v2[0:1]
    slot = lax.broadcasted_iota(jnp.int32, (count, cols), 0)
    out = jnp.zeros((count, cols), F32)
    for n in range(count):
        m = jnp.max(jnp.maximum(lists[0], tail), axis=0, keepdims=True)
        out = jnp.where(slot == n, m, out)
        hit = lists[0] == m
        for lvl in range(count - 1 - n):
            lists[lvl] = jnp.where(hit, lists[lvl + 1], lists[lvl])
        tail = jnp.where(tail == m, NEG_INF, tail)
    return out


def _route_kernel(h2_ref, wq_ref, keys_ref, h2t_ref, rank_ref, p2_ref, cnt_ref, p1_ref, qp_ref):
    h2 = h2_ref[...]
    h2t_ref[...] = pltpu.bitcast(h2.astype(F32).T.astype(BF16), jnp.uint32)
    qp_ref[...] = jnp.dot(h2, wq_ref[...], preferred_element_type=F32).astype(BF16)
    nt = (((1,), (1,)), ((), ()))

    def head(h, carry):
        col = pl.multiple_of(h * PEER_QDIM, PEER_QDIM)
        s1 = lax.dot_general(keys_ref[h, 0], qp_ref[:, pl.ds(col, PEER_HALF)], nt,
                             preferred_element_type=F32)
        s2 = lax.dot_general(keys_ref[h, 1], qp_ref[:, pl.ds(col + PEER_HALF, PEER_HALF)], nt,
                             preferred_element_type=F32)
        v1 = _top_desc_sorted(s1, PEER_TOPK)
        v2 = _top_desc_sorted(s2, PEER_TOPK)
        best = _top_pair_sums(v1, v2, PEER_TOPK)
        tau = best[PEER_TOPK - 1:PEER_TOPK]
        z = jnp.sum(jnp.exp(best - best[0:1]), axis=0, keepdims=True)
        rank = jnp.zeros_like(s2)
        cnt_top = jnp.zeros_like(v1)
        for k in range(PEER_TOPK):
            vk = v2[k:k + 1]
            rank = jnp.where(vk > s2, k + 1.0, rank)
            cnt_top = jnp.where(v1 + vk >= tau, k + 1.0, cnt_top)
        cnt = jnp.zeros_like(s1)
        for j in range(PEER_TOPK):
            cnt = jnp.where(s1 == v1[j:j + 1], cnt_top[j:j + 1], cnt)
        steps = N_EXPERTS // PEER_EXPERTS
        cnt_ref[:, h] = cnt.reshape(steps, N_KEYS // steps, ROUTE_TILE)
        p1_ref[:, h] = jnp.exp(s1 - v1[0:1]).reshape(steps, N_KEYS // steps, ROUTE_TILE)
        rank_ref[h] = pltpu.bitcast(rank.astype(BF16), jnp.uint32)
        p2_ref[h] = pltpu.bitcast((jnp.exp(s2 - v2[0:1]) * (0.5 / z)).astype(BF16), jnp.uint32)
        return carry

    lax.fori_loop(0, PEER_HEADS, head, 0, unroll=2)


def _route(h2, wq, keys):
    t = h2.shape[0]
    tok4 = pl.BlockSpec((PEER_HEADS, N_KEYS // 2, ROUTE_TILE), lambda i: (0, 0, i))
    steps = N_EXPERTS // PEER_EXPERTS
    tok3 = pl.BlockSpec((steps, PEER_HEADS, N_KEYS // steps, ROUTE_TILE), lambda i: (0, 0, 0, i))
    return pl.pallas_call(
        _route_kernel,
        grid=(t // ROUTE_TILE,),
        in_specs=[pl.BlockSpec((ROUTE_TILE, D_MODEL), lambda i: (i, 0)),
                  _full((D_MODEL, PEER_HEADS * PEER_QDIM)),
                  _full((PEER_HEADS, 2, N_KEYS, PEER_HALF))],
        out_specs=[pl.BlockSpec((D_MODEL // 2, ROUTE_TILE), lambda i: (0, i)), tok4, tok4, tok3, tok3],
        out_shape=[jax.ShapeDtypeStruct((D_MODEL // 2, t), jnp.uint32),
                   jax.ShapeDtypeStruct((PEER_HEADS, N_KEYS // 2, t), jnp.uint32),
                   jax.ShapeDtypeStruct((PEER_HEADS, N_KEYS // 2, t), jnp.uint32),
                   jax.ShapeDtypeStruct((steps, PEER_HEADS, N_KEYS // steps, t), F32),
                   jax.ShapeDtypeStruct((steps, PEER_HEADS, N_KEYS // steps, t), F32)],
        scratch_shapes=[pltpu.VMEM((ROUTE_TILE, PEER_HEADS * PEER_QDIM), BF16)],
        compiler_params=_params("arbitrary"),
        name="peer_route",
    )(h2, wq, keys)


def _pack_kernel(transpose, x_ref, o_ref):
    x = x_ref[...]
    if transpose:
        x = x.T
    o_ref[...] = pltpu.bitcast(x.astype(BF16), jnp.uint32)


def _pack_table(table, transpose):
    rows, cols = table.shape
    blk = 2048
    if transpose:
        out_spec = pl.BlockSpec((cols // 2, blk), lambda i: (0, i))
        out_shape = jax.ShapeDtypeStruct((cols // 2, rows), jnp.uint32)
    else:
        out_spec = pl.BlockSpec((blk // 2, cols), lambda i: (i, 0))
        out_shape = jax.ShapeDtypeStruct((rows // 2, cols), jnp.uint32)
    return pl.pallas_call(
        functools.partial(_pack_kernel, transpose),
        grid=(rows // blk,),
        in_specs=[pl.BlockSpec((blk, cols), lambda i: (i, 0))],
        out_specs=out_spec,
        out_shape=out_shape,
        compiler_params=_params("arbitrary"),
        name="pack_table_t" if transpose else "pack_table",
    )(table)


def _gate_rows(cs, act_ref, wt_ref, rank_ref, p2_ref, cnt_ref, p1_ref):
    tiles = (N_KEYS // BF16_SUBLANES, BF16_SUBLANES, LANES)
    for lt in range(PEER_TOKENS // LANES):
        lanes = slice(lt * LANES, (lt + 1) * LANES)
        gates = [jnp.zeros(tiles, BF16) for _ in cs]
        for h in range(PEER_HEADS):
            rank = pltpu.bitcast(rank_ref[h, :, lanes], BF16).reshape(tiles)
            p2 = pltpu.bitcast(p2_ref[h, :, lanes], BF16).reshape(tiles)
            for n, c in enumerate(cs):
                cnt = jnp.broadcast_to(cnt_ref[h, c:c + 1, lanes], (BF16_SUBLANES, LANES)).astype(BF16)
                p1 = jnp.broadcast_to(p1_ref[h, c:c + 1, lanes], (BF16_SUBLANES, LANES)).astype(BF16)
                gates[n] = gates[n] + jnp.where(rank < cnt[None], p2, jnp.zeros((), BF16)) * p1[None]
        for n, c in enumerate(cs):
            rows = slice(c * N_KEYS, (c + 1) * N_KEYS)
            a = act_ref[rows, lanes]
            gl = a + a * lax.erf(a * (1.0 / math.sqrt(2.0)))
            wt_ref[rows, lanes] = gl.astype(BF16) * gates[n].reshape(N_KEYS, LANES)


def _expert_kernel(pairs_per_block, u_ref, vt_ref, h2t_ref, rank_a, p2_a, rank_b, p2_b, cnt_a, p1_a,
                   cnt_b, p1_b, x1_ref, mod_ref, y_ref, acc_ref, act_a, act_b, wt_a, wt_b):
    g = pl.program_id(0)

    @pl.when(g == 0)
    def _():
        acc_ref[...] = jnp.zeros_like(acc_ref)
        act_b[...] = jnp.zeros_like(act_b)
        wt_a[...] = jnp.zeros_like(wt_a)

    key_rows = PEER_EXPERTS // N_KEYS

    def half_step(half, wt_old, act_old, wt_new, act_new, rank_ref, p2_ref, cnt_ref, p1_ref):
        regions = 2
        gate_rows = key_rows // regions
        for idx in range(regions):
            @pl.when(g >= -(half * regions + idx))
            def _(idx=idx):
                for c in range(idx * gate_rows, (idx + 1) * gate_rows, GATE_ROWS):
                    _gate_rows(range(c, c + GATE_ROWS), act_old, wt_new, rank_ref, p2_ref, cnt_ref, p1_ref)
                for part in range(DOT_SPLIT):
                    if idx == 0:
                        m = D_MODEL // DOT_SPLIT
                        vt = pltpu.bitcast(vt_ref[part * m // 2:(part + 1) * m // 2,
                                                  half * PEER_EXPERTS:(half + 1) * PEER_EXPERTS], BF16)
                        acc_ref[part * m:(part + 1) * m, :] += jnp.dot(vt, wt_old[...],
                                                                       preferred_element_type=F32)
                    else:
                        m = PEER_EXPERTS // DOT_SPLIT
                        lo = (half * PEER_EXPERTS + part * m) // 2
                        u = pltpu.bitcast(u_ref[lo:lo + m // 2, :], BF16)
                        act_new[part * m:(part + 1) * m, :] = jnp.dot(u, pltpu.bitcast(h2t_ref[...], BF16),
                                                                      preferred_element_type=F32)

    half_step(0, wt_a, act_b, wt_b, act_a, rank_a, p2_a, cnt_a, p1_a)
    half_step(1, wt_b, act_a, wt_a, act_b, rank_b, p2_b, cnt_b, p1_b)

    @pl.when((g > 0) & (g % pairs_per_block == 0))
    def _():
        g2 = mod_ref[5:6, :]
        y_ref[...] = x1_ref[...] + g2 * acc_ref[...].T
        acc_ref[...] = jnp.zeros_like(acc_ref)


def _experts(u_bf, vt_bf, h2t, rank, p2, cnt, p1, x1, mod3, mod_row):
    t = x1.shape[0]
    ppb = N_EXPERTS // (2 * PEER_EXPERTS)
    steps = (t // PEER_TOKENS) * ppb + 1
    cur = lambda g: jnp.minimum(g, steps - 2)
    prv = lambda g: jnp.maximum(g - 1, 0)
    packed = lambda f: pl.BlockSpec((PEER_HEADS, N_KEYS // 2, PEER_TOKENS), lambda g: (0, 0, f(g) // ppb))
    rows = lambda f, half: pl.BlockSpec((None, PEER_HEADS, PEER_EXPERTS // N_KEYS, PEER_TOKENS),
                                        lambda g: (2 * (f(g) % ppb) + half, 0, 0, f(g) // ppb))
    return pl.pallas_call(
        functools.partial(_expert_kernel, ppb),
        grid=(steps,),
        in_specs=[pl.BlockSpec((PEER_EXPERTS, D_MODEL), lambda g: (cur(g) % ppb, 0)),
                  pl.BlockSpec((D_MODEL // 2, 2 * PEER_EXPERTS), lambda g: (0, prv(g) % ppb)),
                  pl.BlockSpec((D_MODEL // 2, PEER_TOKENS), lambda g: (0, cur(g) // ppb)),
                  packed(prv), packed(prv), packed(cur), packed(cur),
                  rows(prv, 1), rows(prv, 1), rows(cur, 0), rows(cur, 0),
                  pl.BlockSpec((PEER_TOKENS, D_MODEL), lambda g: (prv(g) // ppb, 0)),
                  pl.BlockSpec((None, 6, D_MODEL), lambda g: (mod_row(prv(g) // ppb), 0, 0))],
        out_specs=pl.BlockSpec((PEER_TOKENS, D_MODEL), lambda g: (prv(g) // ppb, 0)),
        out_shape=jax.ShapeDtypeStruct((t, D_MODEL), F32),
        scratch_shapes=[pltpu.VMEM((D_MODEL, PEER_TOKENS), F32),
                        pltpu.VMEM((PEER_EXPERTS, PEER_TOKENS), F32),
                        pltpu.VMEM((PEER_EXPERTS, PEER_TOKENS), F32),
                        pltpu.VMEM((PEER_EXPERTS, PEER_TOKENS), BF16),
                        pltpu.VMEM((PEER_EXPERTS, PEER_TOKENS), BF16)],
        compiler_params=_params("arbitrary"),
        name="peer_experts",
    )(u_bf, vt_bf, h2t, rank, p2, rank, p2, cnt, p1, cnt, p1, x1, mod3)


def _rope_tables(seq_len):
    rows = seq_len // GRID_W
    row = jnp.repeat(jnp.arange(rows, dtype=F32), GRID_W)
    col = jnp.tile(jnp.arange(GRID_W, dtype=F32), rows)
    inv = ROPE_THETA ** (-jnp.arange(AXIS_FREQS, dtype=F32) / AXIS_FREQS)
    ang_row = row[:, None] * inv
    ang_col = col[:, None] * inv
    zeros = jnp.zeros_like(ang_row)
    cos = jnp.concatenate([jnp.cos(ang_row)] * 2 + [jnp.cos(ang_col)] * 2, axis=-1)
    sin_lo = jnp.concatenate([-jnp.sin(ang_row), zeros, -jnp.sin(ang_col), zeros], axis=-1)
    sin_hi = jnp.concatenate([zeros, jnp.sin(ang_row), zeros, jnp.sin(ang_col)], axis=-1)
    return tuple(jnp.tile(t, (1, N_HEADS)) for t in (cos, sin_lo, sin_hi))


def _head_mean_matrix(width):
    idx = jnp.arange(width) // HEAD_DIM
    return jnp.where(idx[:, None] == idx[None, :], 1.0 / HEAD_DIM, 0.0).astype(BF16)


def _trunk(x3, mod3, mod_of_seq, cache, w, rope):
    b, seq, _ = x3.shape
    t = b * seq
    tiles_per_seq = seq // ROW_TILE
    q, k, v, xg, *cache_out = _inproj(x3.reshape(t, D_MODEL), mod3, mod_of_seq, seq, w["norm1"], w["w_in"],
                                      w["qn"], w["kn"], w["gq"], w["gk"], rope)
    conv = _conv_module(xg, tiles_per_seq, w["conv_dw"], w["conv_dw_b"], w["conv_ln_g"], w["conv_ln_b"],
                        w["conv_pw"])
    x1, h2 = _mix(x3, mod3, mod_of_seq, q.reshape(b, seq, ATTN_WIDTH), k.reshape(b, seq, KV_WIDTH),
                  v.reshape(b, seq, KV_WIDTH), conv.reshape(b, seq, CONV_CH), cache, w["sink"],
                  w["w_out"], w["norm2"])
    h2t, rank, p2, cnt, p1 = _route(h2.reshape(t, D_MODEL), w["peer_wq"], w["peer_keys"])
    per_seq = seq // PEER_TOKENS
    if per_seq == 0:
        seqs_per_block = PEER_TOKENS // seq
        row_of_block = lambda i: mod_of_seq(i * seqs_per_block)
    else:
        row_of_block = lambda i: mod_of_seq(i // per_seq)
    y = _experts(w["peer_u"], w["peer_vt"], h2t, rank, p2, cnt, p1, x1.reshape(t, D_MODEL), mod3, row_of_block)
    return y.reshape(b, seq, D_MODEL), cache_out


def kernel(x_prompt, x_sample, cache_k, cache_v, c, c_ctx, norm1, norm2, w_ada, b_ada, w_in, w_out,
           q_norm, k_norm, attn_sink, conv_dw, conv_dw_b, conv_ln_g, conv_ln_b, conv_pw,
           peer_wq, peer_keys, peer_u, peer_v):
    depth = norm1.shape[0]
    batch, seq, _ = x_prompt.shape
    dec_batch, dec_seq, _ = x_sample.shape
    past = cache_k.shape[2]
    assert dec_batch + 1 <= 8 and depth >= 1
    cond8 = jnp.concatenate([c_ctx[None, :], c, jnp.zeros((8 - 1 - dec_batch, D_MODEL), F32)], axis=0)
    rope = _rope_tables(dec_seq)
    xp, xs = x_prompt, x_sample
    new_ks, new_vs = [], []
    for l in range(depth):
        w = dict(
            norm1=norm1[l][None], norm2=norm2[l][None],
            w_in=w_in[l].astype(BF16), w_out=w_out[l].astype(BF16),
            qn=jnp.tile(q_norm[l], N_HEADS)[None], kn=jnp.tile(k_norm[l], N_KV_HEADS)[None],
            gq=_head_mean_matrix(ATTN_WIDTH), gk=_head_mean_matrix(KV_WIDTH),
            sink=attn_sink[l],
            conv_dw=conv_dw[l], conv_dw_b=conv_dw_b[l][None], conv_ln_g=conv_ln_g[l][None],
            conv_ln_b=conv_ln_b[l][None], conv_pw=conv_pw[l].astype(BF16),
            peer_wq=peer_wq[l].astype(BF16), peer_keys=peer_keys[l].astype(BF16),
            peer_u=_pack_table(peer_u[l], False), peer_vt=_pack_table(peer_v[l], True),
        )
        mod3 = _modulation(cond8, w_ada[l], b_ada[l]).reshape(8, 6, D_MODEL)
        xp, (kt_c, vt_c) = _trunk(xp, mod3, lambda s: 0, None, w, None)
        new_ks.append(kt_c.reshape(batch, N_KV_HEADS, HEAD_DIM, seq).transpose(0, 3, 1, 2))
        new_vs.append(vt_c.reshape(batch, N_KV_HEADS, HEAD_DIM, seq).transpose(0, 3, 1, 2))
        cache = (cache_k[:, l].reshape(dec_batch, past, KV_WIDTH), cache_v[:, l].reshape(dec_batch, past, KV_WIDTH))
        xs, _ = _trunk(xs, mod3, lambda s: s + 1, cache, w, rope)
    join = (lambda parts: parts[0][:, None]) if depth == 1 else (lambda parts: jnp.stack(parts, axis=1))
    return (xp, xs, join(new_ks), join(new_vs))
```

```python
import functools
import math

import jax
import jax.numpy as jnp
from jax import lax
from jax.experimental import pallas as pl
from jax.experimental.pallas import tpu as pltpu

F32 = jnp.float32
BF16 = jnp.bfloat16

D_MODEL = 1024
GRID_W = 64
N_HEADS = 8
N_KV_HEADS = 2
GROUP = N_HEADS // N_KV_HEADS
HEAD_DIM = 64
ATTN_WIDTH = N_HEADS * HEAD_DIM
KV_WIDTH = N_KV_HEADS * HEAD_DIM
WINDOW = 128
BLOCK = 128
ROPE_THETA = 10000.0
AXIS_DIM = HEAD_DIM // 2
AXIS_FREQS = AXIS_DIM // 2
CONV_CH = D_MODEL - ATTN_WIDTH
CONV_K = 31
CONV_PAD = CONV_K // 2
IN_WIDTH = ATTN_WIDTH + 2 * KV_WIDTH + 2 * CONV_CH
N_KEYS = 128
N_EXPERTS = N_KEYS * N_KEYS
PEER_HEADS = 8
PEER_QDIM = 256
PEER_HALF = PEER_QDIM // 2
PEER_TOPK = 16
EPS = 1e-6

LANES = 128
BF16_SUBLANES = 16
ROW_TILE = 256
INPROJ_TILE = 512
HEAD_STACK = 4
MIX_TILE = 512
HALO = 16
ROUTE_TILE = 512
PEER_TOKENS = 512
PEER_EXPERTS = 1024
GATE_ROWS = 4
VMEM_LIMIT = 48 * 1024 * 1024

NEG_INF = float("-inf")


def _params(*sem, flags=None):
    return pltpu.CompilerParams(dimension_semantics=sem, vmem_limit_bytes=VMEM_LIMIT, flags=flags)


def _full(shape):
    return pl.BlockSpec(shape, lambda *_: (0,) * len(shape))


def _mod_kernel(cond_ref, w_ref, b_ref, o_ref):
    cnd = cond_ref[...]
    act = cnd * jax.nn.sigmoid(cnd)
    o_ref[...] = jnp.dot(act, w_ref[...], precision=lax.Precision.HIGHEST,
                         preferred_element_type=F32) + b_ref[...]


def _modulation(cond8, w_ada, b_ada):
    n = w_ada.shape[1]
    nb = n // D_MODEL
    return pl.pallas_call(
        _mod_kernel,
        grid=(nb,),
        in_specs=[_full((8, D_MODEL)),
                  pl.BlockSpec((D_MODEL, D_MODEL), lambda j: (0, j)),
                  pl.BlockSpec((1, D_MODEL), lambda j: (0, j))],
        out_specs=pl.BlockSpec((8, D_MODEL), lambda j: (0, j)),
        out_shape=jax.ShapeDtypeStruct((8, n), F32),
        compiler_params=_params("arbitrary"),
        name="modulation",
    )(cond8, w_ada, b_ada.reshape(1, n))


def _group_rms(t, gmat, gain):
    ms = jnp.dot((t * t).astype(BF16), gmat, preferred_element_type=F32)
    return t * lax.rsqrt(ms + EPS) * gain


def _rope(t, cos, sin_lo, sin_hi):
    w = t.shape[1]
    return (t * cos + pltpu.roll(t, w - AXIS_FREQS, 1) * sin_lo
            + pltpu.roll(t, AXIS_FREQS, 1) * sin_hi)


def _inproj_kernel(latent, x_ref, mod_ref, n1_ref, win_ref, qn_ref, kn_ref, gq_ref, gk_ref, *rest):
    if latent:
        cos_ref, slo_ref, shi_ref, q_ref, k_ref, v_ref, xg_ref = rest
    else:
        q_ref, k_ref, v_ref, xg_ref, kt_ref, vt_ref = rest
    x = x_ref[...]
    sh1 = mod_ref[0:1, :]
    sc1 = mod_ref[1:2, :]
    h = x * lax.rsqrt(jnp.mean(x * x, axis=-1, keepdims=True) + EPS) * n1_ref[...]
    h = h * (1.0 + sc1) + sh1
    z = jnp.dot(h.astype(BF16), win_ref[...], preferred_element_type=F32)
    q = _group_rms(z[:, :ATTN_WIDTH], gq_ref[...], qn_ref[...])
    k = _group_rms(z[:, ATTN_WIDTH:ATTN_WIDTH + KV_WIDTH], gk_ref[...], kn_ref[...])
    if latent:
        q = _rope(q, cos_ref[...], slo_ref[...], shi_ref[...])
        k = _rope(k, cos_ref[:, :KV_WIDTH], slo_ref[:, :KV_WIDTH], shi_ref[:, :KV_WIDTH])
    q_ref[...] = (q * (1.0 / math.sqrt(HEAD_DIM))).astype(BF16)
    v = z[:, ATTN_WIDTH + KV_WIDTH:ATTN_WIDTH + 2 * KV_WIDTH]
    k_ref[...] = k
    v_ref[...] = v
    if not latent:
        seq = kt_ref.shape[2]
        for s in range(kt_ref.shape[0]):
            kt_ref[s] = k[s * seq:(s + 1) * seq, :].T
            vt_ref[s] = v[s * seq:(s + 1) * seq, :].T
    u = z[:, ATTN_WIDTH + 2 * KV_WIDTH:]
    xg_ref[...] = u[:, :CONV_CH] * jax.nn.sigmoid(u[:, CONV_CH:])


def _inproj(x2d, mod3, mod_of_seq, seq, norm1, w_in, qn, kn, gq, gk, rope):
    t = x2d.shape[0]
    latent = rope is not None
    row = lambda i: (i, 0)
    tile = INPROJ_TILE
    if seq >= tile:
        assert seq % tile == 0
        tiles_per_seq = seq // tile
        mod_row = lambda i: mod_of_seq(i // tiles_per_seq)
    else:
        assert tile % seq == 0 and not latent
        tiles_per_seq = 1
        mod_row = lambda i: mod_of_seq(i * (tile // seq))
    in_specs = [pl.BlockSpec((tile, D_MODEL), row),
                pl.BlockSpec((None, 6, D_MODEL), lambda i: (mod_row(i), 0, 0)),
                _full((1, D_MODEL)), _full((D_MODEL, IN_WIDTH)),
                _full((1, ATTN_WIDTH)), _full((1, KV_WIDTH)),
                _full((ATTN_WIDTH, ATTN_WIDTH)), _full((KV_WIDTH, KV_WIDTH))]
    args = [x2d, mod3, norm1, w_in, qn, kn, gq, gk]
    if latent:
        pos = lambda i: (i % tiles_per_seq, 0)
        in_specs += [pl.BlockSpec((tile, ATTN_WIDTH), pos)] * 3
        args += list(rope)
    out_specs = [pl.BlockSpec((tile, ATTN_WIDTH), row),
                 pl.BlockSpec((tile, KV_WIDTH), row),
                 pl.BlockSpec((tile, KV_WIDTH), row),
                 pl.BlockSpec((tile, CONV_CH), row)]
    out_shape = [jax.ShapeDtypeStruct((t, ATTN_WIDTH), BF16),
                 jax.ShapeDtypeStruct((t, KV_WIDTH), F32),
                 jax.ShapeDtypeStruct((t, KV_WIDTH), F32),
                 jax.ShapeDtypeStruct((t, CONV_CH), F32)]
    if not latent:
        out_specs += [pl.BlockSpec((tile // seq, KV_WIDTH, seq), lambda i: (i, 0, 0))] * 2
        out_shape += [jax.ShapeDtypeStruct((t // seq, KV_WIDTH, seq), F32)] * 2
    return pl.pallas_call(
        functools.partial(_inproj_kernel, latent),
        grid=(t // tile,),
        in_specs=in_specs,
        out_specs=out_specs,
        out_shape=out_shape,
        compiler_params=_params("arbitrary"),
        name="inproj_latent" if latent else "inproj_context",
    )(*args)


def _conv_kernel(tiles_per_seq, prev_ref, cur_ref, nxt_ref, dw_ref, db_ref, lg_ref, lb_ref, pw_ref,
                 o_ref, xs_ref, act_ref):
    i = pl.program_id(0)
    pos = i % tiles_per_seq
    zero = jnp.zeros((HALO, CONV_CH), F32)
    xs_ref[0, 0:HALO, :] = jnp.where(pos > 0, prev_ref[...], zero)
    xs_ref[0, HALO:HALO + ROW_TILE, :] = cur_ref[...]
    xs_ref[0, HALO + ROW_TILE:2 * HALO + ROW_TILE, :] = jnp.where(pos < tiles_per_seq - 1, nxt_ref[...], zero)
    span = ROW_TILE + 2 * HALO - 8
    for r in range(1, 8):
        xs_ref[r, 0:span, :] = xs_ref[0, r:r + span, :]
    chunk = 32

    def body(c, carry):
        base = pl.multiple_of(c * chunk, chunk)
        acc = jnp.zeros((chunk, CONV_CH), F32) + db_ref[...]
        for k in range(CONV_K):
            kk = k + HALO - CONV_PAD
            start = pl.multiple_of(base + 8 * (kk // 8), 8)
            acc = acc + xs_ref[kk % 8, pl.ds(start, chunk), :] * dw_ref[k:k + 1, :]
        act_ref[pl.ds(base, chunk), :] = acc
        return carry

    lax.fori_loop(0, ROW_TILE // chunk, body, 0)
    acc = act_ref[...]
    mu = jnp.mean(acc, axis=-1, keepdims=True)
    cen = acc - mu
    var = jnp.mean(cen * cen, axis=-1, keepdims=True)
    y = cen * lax.rsqrt(var + EPS) * lg_ref[...] + lb_ref[...]
    y = y * jax.nn.sigmoid(y)
    o_ref[...] = jnp.dot(y.astype(BF16), pw_ref[...], preferred_element_type=F32).astype(BF16)


def _conv_module(xg, tiles_per_seq, dw, db, lg, lb, pw):
    t = xg.shape[0]
    nt = t // ROW_TILE
    per = ROW_TILE // HALO
    last = t // HALO - 1
    return pl.pallas_call(
        functools.partial(_conv_kernel, tiles_per_seq),
        grid=(nt,),
        in_specs=[pl.BlockSpec((HALO, CONV_CH), lambda i: (jnp.maximum(i * per - 1, 0), 0)),
                  pl.BlockSpec((ROW_TILE, CONV_CH), lambda i: (i, 0)),
                  pl.BlockSpec((HALO, CONV_CH), lambda i: (jnp.minimum((i + 1) * per, last), 0)),
                  _full((CONV_K, CONV_CH)), _full((1, CONV_CH)), _full((1, CONV_CH)), _full((1, CONV_CH)),
                  _full((CONV_CH, CONV_CH))],
        out_specs=pl.BlockSpec((ROW_TILE, CONV_CH), lambda i: (i, 0)),
        out_shape=jax.ShapeDtypeStruct((t, CONV_CH), BF16),
        scratch_shapes=[pltpu.VMEM((8, ROW_TILE + 2 * HALO, CONV_CH), F32),
                        pltpu.VMEM((ROW_TILE, CONV_CH), F32)],
        compiler_params=_params("arbitrary"),
        name="conv_module",
    )(xg, xg, xg, dw, db, lg, lb, pw)


def _attend(qg, sink_col, parts):
    m = sink_col
    scores = []
    for keys, _, mask in parts:
        s = lax.dot_general(qg, keys, (((1,), (1,)), ((), ())), preferred_element_type=F32)
        if mask is not None:
            s = jnp.where(mask, s, NEG_INF)
        m = jnp.maximum(m, jnp.max(s, axis=1, keepdims=True))
        scores.append(s)
    out = None
    for s, (_, vals, _) in zip(scores, parts):
        p = jnp.exp(s - m)
        ext = jnp.concatenate([vals, jnp.ones_like(vals)], axis=1)
        o = jnp.dot(p.astype(BF16), ext, preferred_element_type=F32)
        out = o if out is None else out + o
    denom = jnp.exp(sink_col - m) + out[:, HEAD_DIM:HEAD_DIM + 1]
    return out[:, :HEAD_DIM] / denom


def _mix_kernel(latent, seq, x_ref, mod_ref, q_ref, k_ref, v_ref, cv_ref, *rest):
    if latent:
        ck_ref, cvv_ref, sink_ref, wout_ref, n2_ref, x1_ref, h2_ref, attn_ref = rest
    else:
        sink_ref, wout_ref, n2_ref, x1_ref, h2_ref, attn_ref = rest
    prob = ROW_TILE
    width = min(seq, prob + 2 * WINDOW)
    mix_tile = x_ref.shape[0]
    steps_per_seq = max(seq // mix_tile, 1)
    for sub in range(mix_tile // prob):
        rows = slice(sub * prob, (sub + 1) * prob)
        if latent:
            q0 = (pl.program_id(0) % steps_per_seq) * mix_tile + sub * prob
            start = pl.multiple_of(jnp.clip(q0 - WINDOW, 0, seq - width), BLOCK)
            qpos = q0 + (lax.broadcasted_iota(jnp.int32, (HEAD_STACK * prob, width), 0) & (prob - 1))
            kpos = start + lax.broadcasted_iota(jnp.int32, (HEAD_STACK * prob, width), 1)
            mask = jnp.abs(kpos - qpos) <= WINDOW
        else:
            start = sub * prob
            mask = None
        for kv in range(N_KV_HEADS):
            lanes = slice(kv * HEAD_DIM, (kv + 1) * HEAD_DIM)
            parts = [(k_ref[pl.ds(start, width), lanes].astype(BF16),
                      v_ref[pl.ds(start, width), lanes].astype(BF16), mask)]
            if latent:
                parts.append((ck_ref[:, lanes].astype(BF16), cvv_ref[:, lanes].astype(BF16), None))
            for h0 in range(kv * GROUP, (kv + 1) * GROUP, HEAD_STACK):
                heads = range(h0, h0 + HEAD_STACK)
                sink_col = jnp.concatenate([jnp.full((prob, 1), sink_ref[h], F32) for h in heads], axis=0)
                qg = jnp.concatenate([q_ref[rows, h * HEAD_DIM:(h + 1) * HEAD_DIM] for h in heads], axis=0)
                o = _attend(qg, sink_col, parts)
                for n, h in enumerate(heads):
                    attn_ref[rows, h * HEAD_DIM:(h + 1) * HEAD_DIM] = o[n * prob:(n + 1) * prob].astype(BF16)
    g1 = mod_ref[2:3, :]
    sh2 = mod_ref[3:4, :]
    sc2 = mod_ref[4:5, :]
    mixed = (jnp.dot(attn_ref[...], wout_ref[0:ATTN_WIDTH, :], preferred_element_type=F32)
             + jnp.dot(cv_ref[...], wout_ref[ATTN_WIDTH:, :], preferred_element_type=F32))
    x1 = x_ref[...] + g1 * mixed
    x1_ref[...] = x1
    h2 = x1 * lax.rsqrt(jnp.mean(x1 * x1, axis=-1, keepdims=True) + EPS) * n2_ref[...]
    h2_ref[...] = (h2 * (1.0 + sc2) + sh2).astype(BF16)


def _mix(x3, mod3, mod_row, q3, k3, v3, conv3, cache, sink, w_out, norm2):
    b, seq, _ = x3.shape
    latent = cache is not None
    t = b * seq
    mix_tile = MIX_TILE
    if latent:
        assert seq % mix_tile == 0
        steps_per_seq = seq // mix_tile
        seq_of = lambda i: i // steps_per_seq
        kv_spec = pl.BlockSpec((seq, KV_WIDTH), lambda i: (seq_of(i), 0))
    else:
        assert seq == ROW_TILE and t % mix_tile == 0
        seqs_per_step = mix_tile // seq
        seq_of = lambda i: i * seqs_per_step
        kv_spec = pl.BlockSpec((mix_tile, KV_WIDTH), lambda i: (i, 0))
    tile = lambda w: pl.BlockSpec((mix_tile, w), lambda i: (i, 0))
    flat = lambda a: a.reshape(t, a.shape[-1])
    in_specs = [tile(D_MODEL),
                pl.BlockSpec((None, 6, D_MODEL), lambda i: (mod_row(seq_of(i)), 0, 0)),
                tile(ATTN_WIDTH), kv_spec, kv_spec, tile(CONV_CH)]
    args = [flat(x3), mod3, flat(q3), flat(k3), flat(v3), flat(conv3)]
    if latent:
        past = cache[0].shape[1]
        in_specs += [pl.BlockSpec((None, past, KV_WIDTH), lambda i: (seq_of(i), 0, 0))] * 2
        args += list(cache)
    in_specs += [pl.BlockSpec(memory_space=pltpu.SMEM), _full((D_MODEL, D_MODEL)), _full((1, D_MODEL))]
    args += [sink, w_out, norm2]
    x1, h2 = pl.pallas_call(
        functools.partial(_mix_kernel, latent, seq),
        grid=(t // mix_tile,),
        in_specs=in_specs,
        out_specs=[tile(D_MODEL), tile(D_MODEL)],
        out_shape=[jax.ShapeDtypeStruct((t, D_MODEL), F32),
                   jax.ShapeDtypeStruct((t, D_MODEL), BF16)],
        scratch_shapes=[pltpu.VMEM((mix_tile, ATTN_WIDTH), BF16)],
        compiler_params=_params("arbitrary"),
        name="mix_latent" if latent else "mix_context",
    )(*args)
    return x1.reshape(b, seq, D_MODEL), h2.reshape(b, seq, D_MODEL)


def _merge_exchange_pairs(n):
    pairs, p = [], 1
    while p < n:
        k = p
        while k >= 1:
            for j in range(k % p, n - k, 2 * k):
                for i in range(min(k, n - j - k)):
                    if (i + j) // (2 * p) == (i + j + k) // (2 * p):
                        pairs.append((i + j, i + j + k))
            k //= 2
        p *= 2
    return pairs


def _top_desc_sorted(x, count):
    lists = [x[8 * g:8 * (g + 1)] for g in range(count)]
    for a, b in _merge_exchange_pairs(count):
        lists[a], lists[b] = jnp.maximum(lists[a], lists[b]), jnp.minimum(lists[a], lists[b])
    slot = lax.broadcasted_iota(jnp.int32, (count, x.shape[1]), 0)
    out = jnp.zeros((count, x.shape[1]), F32)
    for k in range(count):
        m = jnp.max(lists[0], axis=0, keepdims=True)
        out = jnp.where(slot == k, m, out)
        hit = lists[0] == m
        for lvl in range(count - 1 - k):
            lists[lvl] = jnp.where(hit, lists[lvl + 1], lists[lvl])
    return out


def _top_pair_sums(v1, v2, count):
    cols = v1.shape[1]
    row = lax.broadcasted_iota(jnp.int32, (8, cols), 0)
    lists = [jnp.where(row < count // (k + 1), v1[0:8] + v2[k:k + 1], NEG_INF) for k in range(count)]
    tail = v1[8:count] + v2[0:1]
    slot = lax.broadcasted_iota(jnp.int32, (count, cols), 0)
    out = jnp.zeros((count, cols), F32)
    for n in range(count):
        m = jnp.max(jnp.maximum(lists[0], tail), axis=0, keepdims=True)
        out = jnp.where(slot == n, m, out)
        hit = lists[0] == m
        for lvl in range(count - 1 - n):
            lists[lvl] = jnp.where(hit, lists[lvl + 1], lists[lvl])
        tail = jnp.where(tail == m, NEG_INF, tail)
    return out


def _route_kernel(h2_ref, wq_ref, keys_ref, h2t_ref, rank_ref, p2_ref, cnt_ref, p1_ref, qp_ref):
    h2 = h2_ref[...]
    h2t_ref[...] = pltpu.bitcast(h2.astype(F32).T.astype(BF16), jnp.uint32)
    qp_ref[...] = jnp.dot(h2, wq_ref[...], preferred_element_type=F32).astype(BF16)
    nt = (((1,), (1,)), ((), ()))

    def head(h, carry):
        col = pl.multiple_of(h * PEER_QDIM, PEER_QDIM)
        s1 = lax.dot_general(keys_ref[h, 0], qp_ref[:, pl.ds(col, PEER_HALF)], nt,
                             preferred_element_type=F32)
        s2 = lax.dot_general(keys_ref[h, 1], qp_ref[:, pl.ds(col + PEER_HALF, PEER_HALF)], nt,
                             preferred_element_type=F32)
        v1 = _top_desc_sorted(s1, PEER_TOPK)
        v2 = _top_desc_sorted(s2, PEER_TOPK)
        best = _top_pair_sums(v1, v2, PEER_TOPK)
        tau = best[PEER_TOPK - 1:PEER_TOPK]
        z = jnp.sum(jnp.exp(best - best[0:1]), axis=0, keepdims=True)
        rank = jnp.zeros_like(s2)
        cnt_top = jnp.zeros_like(v1)
        for k in range(PEER_TOPK):
            vk = v2[k:k + 1]
            rank = jnp.where(vk > s2, k + 1.0, rank)
            cnt_top = jnp.where(v1 + vk >= tau, k + 1.0, cnt_top)
        cnt = jnp.zeros_like(s1)
        for j in range(PEER_TOPK):
            cnt = jnp.where(s1 == v1[j:j + 1], cnt_top[j:j + 1], cnt)
        steps = N_EXPERTS // PEER_EXPERTS
        cnt_ref[:, h] = cnt.reshape(steps, N_KEYS // steps, ROUTE_TILE)
        p1_ref[:, h] = jnp.exp(s1 - v1[0:1]).reshape(steps, N_KEYS // steps, ROUTE_TILE)
        rank_ref[h] = pltpu.bitcast(rank.astype(BF16), jnp.uint32)
        p2_ref[h] = pltpu.bitcast((jnp.exp(s2 - v2[0:1]) * (0.5 / z)).astype(BF16), jnp.uint32)
        return carry

    lax.fori_loop(0, PEER_HEADS, head, 0, unroll=2)


def _route(h2, wq, keys):
    t = h2.shape[0]
    tok4 = pl.BlockSpec((PEER_HEADS, N_KEYS // 2, ROUTE_TILE), lambda i: (0, 0, i))
    steps = N_EXPERTS // PEER_EXPERTS
    tok3 = pl.BlockSpec((steps, PEER_HEADS, N_KEYS // steps, ROUTE_TILE), lambda i: (0, 0, 0, i))
    return pl.pallas_call(
        _route_kernel,
        grid=(t // ROUTE_TILE,),
        in_specs=[pl.BlockSpec((ROUTE_TILE, D_MODEL), lambda i: (i, 0)),
                  _full((D_MODEL, PEER_HEADS * PEER_QDIM)),
                  _full((PEER_HEADS, 2, N_KEYS, PEER_HALF))],
        out_specs=[pl.BlockSpec((D_MODEL // 2, ROUTE_TILE), lambda i: (0, i)), tok4, tok4, tok3, tok3],
        out_shape=[jax.ShapeDtypeStruct((D_MODEL // 2, t), jnp.uint32),
                   jax.ShapeDtypeStruct((PEER_HEADS, N_KEYS // 2, t), jnp.uint32),
                   jax.ShapeDtypeStruct((PEER_HEADS, N_KEYS // 2, t), jnp.uint32),
                   jax.ShapeDtypeStruct((steps, PEER_HEADS, N_KEYS // steps, t), F32),
                   jax.ShapeDtypeStruct((steps, PEER_HEADS, N_KEYS // steps, t), F32)],
        scratch_shapes=[pltpu.VMEM((ROUTE_TILE, PEER_HEADS * PEER_QDIM), BF16)],
        compiler_params=_params("arbitrary"),
        name="peer_route",
    )(h2, wq, keys)


def _pack_kernel(transpose, x_ref, o_ref):
    x = x_ref[...]
    if transpose:
        x = x.T
    o_ref[...] = pltpu.bitcast(x.astype(BF16), jnp.uint32)


def _pack_table(table, transpose):
    rows, cols = table.shape
    blk = 2048
    if transpose:
        out_spec = pl.BlockSpec((cols // 2, blk), lambda i: (0, i))
        out_shape = jax.ShapeDtypeStruct((cols // 2, rows), jnp.uint32)
    else:
        out_spec = pl.BlockSpec((blk // 2, cols), lambda i: (i, 0))
        out_shape = jax.ShapeDtypeStruct((rows // 2, cols), jnp.uint32)
    return pl.pallas_call(
        functools.partial(_pack_kernel, transpose),
        grid=(rows // blk,),
        in_specs=[pl.BlockSpec((blk, cols), lambda i: (i, 0))],
        out_specs=out_spec,
        out_shape=out_shape,
        compiler_params=_params("arbitrary"),
        name="pack_table_t" if transpose else "pack_table",
    )(table)


def _gate_rows(cs, act_ref, wt_ref, rank_ref, p2_ref, cnt_ref, p1_ref):
    tiles = (N_KEYS // BF16_SUBLANES, BF16_SUBLANES, LANES)
    for lt in range(PEER_TOKENS // LANES):
        lanes = slice(lt * LANES, (lt + 1) * LANES)
        gates = [jnp.zeros(tiles, BF16) for _ in cs]
        for h in range(PEER_HEADS):
            rank = pltpu.bitcast(rank_ref[h, :, lanes], BF16).reshape(tiles)
            p2 = pltpu.bitcast(p2_ref[h, :, lanes], BF16).reshape(tiles)
            for n, c in enumerate(cs):
                cnt = jnp.broadcast_to(cnt_ref[h, c:c + 1, lanes], (BF16_SUBLANES, LANES)).astype(BF16)
                p1 = jnp.broadcast_to(p1_ref[h, c:c + 1, lanes], (BF16_SUBLANES, LANES)).astype(BF16)
                gates[n] = gates[n] + jnp.where(rank < cnt[None], p2, jnp.zeros((), BF16)) * p1[None]
        for n, c in enumerate(cs):
            rows = slice(c * N_KEYS, (c + 1) * N_KEYS)
            a = act_ref[rows, lanes]
            gl = a + a * lax.erf(a * (1.0 / math.sqrt(2.0)))
            wt_ref[rows, lanes] = gl.astype(BF16) * gates[n].reshape(N_KEYS, LANES)


def _expert_kernel(pairs_per_block, u_ref, vt_ref, h2t_ref, rank_a, p2_a, rank_b, p2_b, cnt_a, p1_a,
                   cnt_b, p1_b, x1_ref, mod_ref, y_ref, acc_ref, act_a, act_b, wt_a, wt_b):
    g = pl.program_id(0)

    @pl.when(g == 0)
    def _():
        acc_ref[...] = jnp.zeros_like(acc_ref)
        act_b[...] = jnp.zeros_like(act_b)
        wt_a[...] = jnp.zeros_like(wt_a)

    key_rows = PEER_EXPERTS // N_KEYS

    def half_step(half, wt_old, act_old, wt_new, act_new, rank_ref, p2_ref, cnt_ref, p1_ref):
        gate_rows = key_rows // 2
        for idx in range(2):
            @pl.when(g >= -(half * 2 + idx))
            def _(idx=idx):
                for c in range(idx * gate_rows, (idx + 1) * gate_rows, GATE_ROWS):
                    _gate_rows(range(c, c + GATE_ROWS), act_old, wt_new, rank_ref, p2_ref, cnt_ref, p1_ref)
                if idx == 0:
                    vt = pltpu.bitcast(vt_ref[:, half * PEER_EXPERTS:(half + 1) * PEER_EXPERTS], BF16)
                    acc_ref[...] += jnp.dot(vt, wt_old[...], preferred_element_type=F32)
                else:
                    lo = half * PEER_EXPERTS // 2
                    u = pltpu.bitcast(u_ref[lo:lo + PEER_EXPERTS // 2, :], BF16)
                    act_new[...] = jnp.dot(u, pltpu.bitcast(h2t_ref[...], BF16), preferred_element_type=F32)

    half_step(0, wt_a, act_b, wt_b, act_a, rank_a, p2_a, cnt_a, p1_a)
    half_step(1, wt_b, act_a, wt_a, act_b, rank_b, p2_b, cnt_b, p1_b)

    @pl.when((g > 0) & (g % pairs_per_block == 0))
    def _():
        g2 = mod_ref[5:6, :]
        y_ref[...] = x1_ref[...] + g2 * acc_ref[...].T
        acc_ref[...] = jnp.zeros_like(acc_ref)


def _experts(u_bf, vt_bf, h2t, rank, p2, cnt, p1, x1, mod3, mod_row):
    t = x1.shape[0]
    ppb = N_EXPERTS // (2 * PEER_EXPERTS)
    steps = (t // PEER_TOKENS) * ppb + 1
    cur = lambda g: jnp.minimum(g, steps - 2)
    prv = lambda g: jnp.maximum(g - 1, 0)
    packed = lambda f: pl.BlockSpec((PEER_HEADS, N_KEYS // 2, PEER_TOKENS), lambda g: (0, 0, f(g) // ppb))
    rows = lambda f, half: pl.BlockSpec((None, PEER_HEADS, PEER_EXPERTS // N_KEYS, PEER_TOKENS),
                                        lambda g: (2 * (f(g) % ppb) + half, 0, 0, f(g) // ppb))
    return pl.pallas_call(
        functools.partial(_expert_kernel, ppb),
        grid=(steps,),
        in_specs=[pl.BlockSpec((PEER_EXPERTS, D_MODEL), lambda g: (cur(g) % ppb, 0)),
                  pl.BlockSpec((D_MODEL // 2, 2 * PEER_EXPERTS), lambda g: (0, prv(g) % ppb)),
                  pl.BlockSpec((D_MODEL // 2, PEER_TOKENS), lambda g: (0, cur(g) // ppb)),
                  packed(prv), packed(prv), packed(cur), packed(cur),
                  rows(prv, 1), rows(prv, 1), rows(cur, 0), rows(cur, 0),
                  pl.BlockSpec((PEER_TOKENS, D_MODEL), lambda g: (prv(g) // ppb, 0)),
                  pl.BlockSpec((None, 6, D_MODEL), lambda g: (mod_row(prv(g) // ppb), 0, 0))],
        out_specs=pl.BlockSpec((PEER_TOKENS, D_MODEL), lambda g: (prv(g) // ppb, 0)),
        out_shape=jax.ShapeDtypeStruct((t, D_MODEL), F32),
        scratch_shapes=[pltpu.VMEM((D_MODEL, PEER_TOKENS), F32),
                        pltpu.VMEM((PEER_EXPERTS, PEER_TOKENS), F32),
                        pltpu.VMEM((PEER_EXPERTS, PEER_TOKENS), F32),
                        pltpu.VMEM((PEER_EXPERTS, PEER_TOKENS), BF16),
                        pltpu.VMEM((PEER_EXPERTS, PEER_TOKENS), BF16)],
        compiler_params=_params("arbitrary"),
        name="peer_experts",
    )(u_bf, vt_bf, h2t, rank, p2, rank, p2, cnt, p1, cnt, p1, x1, mod3)


def _rope_tables(seq_len):
    rows = seq_len // GRID_W
    row = jnp.repeat(jnp.arange(rows, dtype=F32), GRID_W)
    col = jnp.tile(jnp.arange(GRID_W, dtype=F32), rows)
    inv = ROPE_THETA ** (-jnp.arange(AXIS_FREQS, dtype=F32) / AXIS_FREQS)
    ang_row = row[:, None] * inv
    ang_col = col[:, None] * inv
    zeros = jnp.zeros_like(ang_row)
    cos = jnp.concatenate([jnp.cos(ang_row)] * 2 + [jnp.cos(ang_col)] * 2, axis=-1)
    sin_lo = jnp.concatenate([-jnp.sin(ang_row), zeros, -jnp.sin(ang_col), zeros], axis=-1)
    sin_hi = jnp.concatenate([zeros, jnp.sin(ang_row), zeros, jnp.sin(ang_col)], axis=-1)
    return tuple(jnp.tile(t, (1, N_HEADS)) for t in (cos, sin_lo, sin_hi))


def _head_mean_matrix(width):
    idx = jnp.arange(width) // HEAD_DIM
    return jnp.where(idx[:, None] == idx[None, :], 1.0 / HEAD_DIM, 0.0).astype(BF16)


def _trunk(x3, mod3, mod_of_seq, cache, w, rope):
    b, seq, _ = x3.shape
    t = b * seq
    tiles_per_seq = seq // ROW_TILE
    q, k, v, xg, *cache_out = _inproj(x3.reshape(t, D_MODEL), mod3, mod_of_seq, seq, w["norm1"], w["w_in"],
                                      w["qn"], w["kn"], w["gq"], w["gk"], rope)
    conv = _conv_module(xg, tiles_per_seq, w["conv_dw"], w["conv_dw_b"], w["conv_ln_g"], w["conv_ln_b"],
                        w["conv_pw"])
    x1, h2 = _mix(x3, mod3, mod_of_seq, q.reshape(b, seq, ATTN_WIDTH), k.reshape(b, seq, KV_WIDTH),
                  v.reshape(b, seq, KV_WIDTH), conv.reshape(b, seq, CONV_CH), cache, w["sink"],
                  w["w_out"], w["norm2"])
    h2t, rank, p2, cnt, p1 = _route(h2.reshape(t, D_MODEL), w["peer_wq"], w["peer_keys"])
    per_seq = seq // PEER_TOKENS
    if per_seq == 0:
        seqs_per_block = PEER_TOKENS // seq
        row_of_block = lambda i: mod_of_seq(i * seqs_per_block)
    else:
        row_of_block = lambda i: mod_of_seq(i // per_seq)
    y = _experts(w["peer_u"], w["peer_vt"], h2t, rank, p2, cnt, p1, x1.reshape(t, D_MODEL), mod3, row_of_block)
    return y.reshape(b, seq, D_MODEL), cache_out


def kernel(x_prompt, x_sample, cache_k, cache_v, c, c_ctx, norm1, norm2, w_ada, b_ada, w_in, w_out,
           q_norm, k_norm, attn_sink, conv_dw, conv_dw_b, conv_ln_g, conv_ln_b, conv_pw,
           peer_wq, peer_keys, peer_u, peer_v):
    depth = norm1.shape[0]
    batch, seq, _ = x_prompt.shape
    dec_batch, dec_seq, _ = x_sample.shape
    past = cache_k.shape[2]
    assert dec_batch + 1 <= 8 and depth >= 1
    cond8 = jnp.concatenate([c_ctx[None, :], c, jnp.zeros((8 - 1 - dec_batch, D_MODEL), F32)], axis=0)
    rope = _rope_tables(dec_seq)
    xp, xs = x_prompt, x_sample
    new_ks, new_vs = [], []
    for l in range(depth):
        w = dict(
            norm1=norm1[l][None], norm2=norm2[l][None],
            w_in=w_in[l].astype(BF16), w_out=w_out[l].astype(BF16),
            qn=jnp.tile(q_norm[l], N_HEADS)[None], kn=jnp.tile(k_norm[l], N_KV_HEADS)[None],
            gq=_head_mean_matrix(ATTN_WIDTH), gk=_head_mean_matrix(KV_WIDTH),
            sink=attn_sink[l],
            conv_dw=conv_dw[l], conv_dw_b=conv_dw_b[l][None], conv_ln_g=conv_ln_g[l][None],
            conv_ln_b=conv_ln_b[l][None], conv_pw=conv_pw[l].astype(BF16),
            peer_wq=peer_wq[l].astype(BF16), peer_keys=peer_keys[l].astype(BF16),
            peer_u=_pack_table(peer_u[l], False), peer_vt=_pack_table(peer_v[l], True),
        )
        mod3 = _modulation(cond8, w_ada[l], b_ada[l]).reshape(8, 6, D_MODEL)
        xp, (kt_c, vt_c) = _trunk(xp, mod3, lambda s: 0, None, w, None)
        new_ks.append(kt_c.reshape(batch, N_KV_HEADS, HEAD_DIM, seq).transpose(0, 3, 1, 2))
        new_vs.append(vt_c.reshape(batch, N_KV_HEADS, HEAD_DIM, seq).transpose(0, 3, 1, 2))
        cache = (cache_k[:, l].reshape(dec_batch, past, KV_WIDTH), cache_v[:, l].reshape(dec_batch, past, KV_WIDTH))
        xs, _ = _trunk(xs, mod3, lambda s: s + 1, cache, w, rope)
    join = (lambda parts: parts[0][:, None]) if depth == 1 else (lambda parts: jnp.stack(parts, axis=1))
    return (xp, xs, join(new_ks), join(new_vs))
```

```python
import functools
import math

import jax
import jax.numpy as jnp
from jax import lax
from jax.experimental import pallas as pl
from jax.experimental.pallas import tpu as pltpu

F32 = jnp.float32
BF16 = jnp.bfloat16

D_MODEL = 1024
GRID_W = 64
N_HEADS = 8
N_KV_HEADS = 2
GROUP = N_HEADS // N_KV_HEADS
HEAD_DIM = 64
ATTN_WIDTH = N_HEADS * HEAD_DIM
KV_WIDTH = N_KV_HEADS * HEAD_DIM
WINDOW = 128
BLOCK = 128
ROPE_THETA = 10000.0
AXIS_DIM = HEAD_DIM // 2
AXIS_FREQS = AXIS_DIM // 2
CONV_CH = D_MODEL - ATTN_WIDTH
CONV_K = 31
CONV_PAD = CONV_K // 2
IN_WIDTH = ATTN_WIDTH + 2 * KV_WIDTH + 2 * CONV_CH
N_KEYS = 128
N_EXPERTS = N_KEYS * N_KEYS
PEER_HEADS = 8
PEER_QDIM = 256
PEER_HALF = PEER_QDIM // 2
PEER_TOPK = 16
EPS = 1e-6

LANES = 128
BF16_SUBLANES = 16
ROW_TILE = 256
INPROJ_TILE = 512
HEAD_STACK = 4
MIX_TILE = 512
HALO = 16
ROUTE_TILE = 512
PEER_TOKENS = 512
PEER_EXPERTS = 1024
GATE_ROWS = 4
VMEM_LIMIT = 48 * 1024 * 1024

NEG_INF = float("-inf")


def _params(*sem, flags=None):
    return pltpu.CompilerParams(dimension_semantics=sem, vmem_limit_bytes=VMEM_LIMIT, flags=flags)


def _full(shape):
    return pl.BlockSpec(shape, lambda *_: (0,) * len(shape))


def _mod_kernel(cond_ref, w_ref, b_ref, o_ref):
    cnd = cond_ref[...]
    act = cnd * jax.nn.sigmoid(cnd)
    o_ref[...] = jnp.dot(act, w_ref[...], precision=lax.Precision.HIGHEST,
                         preferred_element_type=F32) + b_ref[...]


def _modulation(cond8, w_ada, b_ada):
    n = w_ada.shape[1]
    nb = n // D_MODEL
    return pl.pallas_call(
        _mod_kernel,
        grid=(nb,),
        in_specs=[_full((8, D_MODEL)),
                  pl.BlockSpec((D_MODEL, D_MODEL), lambda j: (0, j)),
                  pl.BlockSpec((1, D_MODEL), lambda j: (0, j))],
        out_specs=pl.BlockSpec((8, D_MODEL), lambda j: (0, j)),
        out_shape=jax.ShapeDtypeStruct((8, n), F32),
        compiler_params=_params("arbitrary"),
        name="modulation",
    )(cond8, w_ada, b_ada.reshape(1, n))


def _group_rms(t, gmat, gain):
    ms = jnp.dot((t * t).astype(BF16), gmat, preferred_element_type=F32)
    return t * lax.rsqrt(ms + EPS) * gain


def _rope(t, cos, sin_lo, sin_hi):
    w = t.shape[1]
    return (t * cos + pltpu.roll(t, w - AXIS_FREQS, 1) * sin_lo
            + pltpu.roll(t, AXIS_FREQS, 1) * sin_hi)


def _inproj_kernel(latent, x_ref, mod_ref, n1_ref, win_ref, qn_ref, kn_ref, gq_ref, gk_ref, *rest):
    if latent:
        cos_ref, slo_ref, shi_ref, q_ref, k_ref, v_ref, xg_ref = rest
    else:
        q_ref, k_ref, v_ref, xg_ref, kt_ref, vt_ref = rest
    x = x_ref[...]
    sh1 = mod_ref[0:1, :]
    sc1 = mod_ref[1:2, :]
    h = x * lax.rsqrt(jnp.mean(x * x, axis=-1, keepdims=True) + EPS) * n1_ref[...]
    h = h * (1.0 + sc1) + sh1
    z = jnp.dot(h.astype(BF16), win_ref[...], preferred_element_type=F32)
    q = _group_rms(z[:, :ATTN_WIDTH], gq_ref[...], qn_ref[...])
    k = _group_rms(z[:, ATTN_WIDTH:ATTN_WIDTH + KV_WIDTH], gk_ref[...], kn_ref[...])
    if latent:
        q = _rope(q, cos_ref[...], slo_ref[...], shi_ref[...])
        k = _rope(k, cos_ref[:, :KV_WIDTH], slo_ref[:, :KV_WIDTH], shi_ref[:, :KV_WIDTH])
    q_ref[...] = (q * (1.0 / math.sqrt(HEAD_DIM))).astype(BF16)
    v = z[:, ATTN_WIDTH + KV_WIDTH:ATTN_WIDTH + 2 * KV_WIDTH]
    k_ref[...] = k
    v_ref[...] = v
    if not latent:
        seq = kt_ref.shape[2]
        for s in range(kt_ref.shape[0]):
            kt_ref[s] = k[s * seq:(s + 1) * seq, :].T
            vt_ref[s] = v[s * seq:(s + 1) * seq, :].T
    u = z[:, ATTN_WIDTH + 2 * KV_WIDTH:]
    xg_ref[...] = u[:, :CONV_CH] * jax.nn.sigmoid(u[:, CONV_CH:])


def _inproj(x2d, mod3, mod_of_seq, seq, norm1, w_in, qn, kn, gq, gk, rope):
    t = x2d.shape[0]
    latent = rope is not None
    row = lambda i: (i, 0)
    tile = INPROJ_TILE
    if seq >= tile:
        assert seq % tile == 0
        tiles_per_seq = seq // tile
        mod_row = lambda i: mod_of_seq(i // tiles_per_seq)
    else:
        assert tile % seq == 0 and not latent
        tiles_per_seq = 1
        mod_row = lambda i: mod_of_seq(i * (tile // seq))
    in_specs = [pl.BlockSpec((tile, D_MODEL), row),
                pl.BlockSpec((None, 6, D_MODEL), lambda i: (mod_row(i), 0, 0)),
                _full((1, D_MODEL)), _full((D_MODEL, IN_WIDTH)),
                _full((1, ATTN_WIDTH)), _full((1, KV_WIDTH)),
                _full((ATTN_WIDTH, ATTN_WIDTH)), _full((KV_WIDTH, KV_WIDTH))]
    args = [x2d, mod3, norm1, w_in, qn, kn, gq, gk]
    if latent:
        pos = lambda i: (i % tiles_per_seq, 0)
        in_specs += [pl.BlockSpec((tile, ATTN_WIDTH), pos)] * 3
        args += list(rope)
    out_specs = [pl.BlockSpec((tile, ATTN_WIDTH), row),
                 pl.BlockSpec((tile, KV_WIDTH), row),
                 pl.BlockSpec((tile, KV_WIDTH), row),
                 pl.BlockSpec((tile, CONV_CH), row)]
    out_shape = [jax.ShapeDtypeStruct((t, ATTN_WIDTH), BF16),
                 jax.ShapeDtypeStruct((t, KV_WIDTH), F32),
                 jax.ShapeDtypeStruct((t, KV_WIDTH), F32),
                 jax.ShapeDtypeStruct((t, CONV_CH), F32)]
    if not latent:
        out_specs += [pl.BlockSpec((tile // seq, KV_WIDTH, seq), lambda i: (i, 0, 0))] * 2
        out_shape += [jax.ShapeDtypeStruct((t // seq, KV_WIDTH, seq), F32)] * 2
    return pl.pallas_call(
        functools.partial(_inproj_kernel, latent),
        grid=(t // tile,),
        in_specs=in_specs,
        out_specs=out_specs,
        out_shape=out_shape,
        compiler_params=_params("arbitrary"),
        name="inproj_latent" if latent else "inproj_context",
    )(*args)


def _conv_kernel(tiles_per_seq, prev_ref, cur_ref, nxt_ref, dw_ref, db_ref, lg_ref, lb_ref, pw_ref,
                 o_ref, xs_ref, act_ref):
    i = pl.program_id(0)
    pos = i % tiles_per_seq
    zero = jnp.zeros((HALO, CONV_CH), F32)
    xs_ref[0, 0:HALO, :] = jnp.where(pos > 0, prev_ref[...], zero)
    xs_ref[0, HALO:HALO + ROW_TILE, :] = cur_ref[...]
    xs_ref[0, HALO + ROW_TILE:2 * HALO + ROW_TILE, :] = jnp.where(pos < tiles_per_seq - 1, nxt_ref[...], zero)
    span = ROW_TILE + 2 * HALO - 8
    for r in range(1, 8):
        xs_ref[r, 0:span, :] = xs_ref[0, r:r + span, :]
    chunk = 64

    def body(c, carry):
        base = pl.multiple_of(c * chunk, chunk)
        acc = jnp.zeros((chunk, CONV_CH), F32) + db_ref[...]
        for k in range(CONV_K):
            kk = k + HALO - CONV_PAD
            start = pl.multiple_of(base + 8 * (kk // 8), 8)
            acc = acc + xs_ref[kk % 8, pl.ds(start, chunk), :] * dw_ref[k:k + 1, :]
        act_ref[pl.ds(base, chunk), :] = acc
        return carry

    lax.fori_loop(0, ROW_TILE // chunk, body, 0)
    acc = act_ref[...]
    mu = jnp.mean(acc, axis=-1, keepdims=True)
    cen = acc - mu
    var = jnp.mean(cen * cen, axis=-1, keepdims=True)
    y = cen * lax.rsqrt(var + EPS) * lg_ref[...] + lb_ref[...]
    y = y * jax.nn.sigmoid(y)
    o_ref[...] = jnp.dot(y.astype(BF16), pw_ref[...], preferred_element_type=F32).astype(BF16)


def _conv_module(xg, tiles_per_seq, dw, db, lg, lb, pw):
    t = xg.shape[0]
    nt = t // ROW_TILE
    per = ROW_TILE // HALO
    last = t // HALO - 1
    return pl.pallas_call(
        functools.partial(_conv_kernel, tiles_per_seq),
        grid=(nt,),
        in_specs=[pl.BlockSpec((HALO, CONV_CH), lambda i: (jnp.maximum(i * per - 1, 0), 0)),
                  pl.BlockSpec((ROW_TILE, CONV_CH), lambda i: (i, 0)),
                  pl.BlockSpec((HALO, CONV_CH), lambda i: (jnp.minimum((i + 1) * per, last), 0)),
                  _full((CONV_K, CONV_CH)), _full((1, CONV_CH)), _full((1, CONV_CH)), _full((1, CONV_CH)),
                  _full((CONV_CH, CONV_CH))],
        out_specs=pl.BlockSpec((ROW_TILE, CONV_CH), lambda i: (i, 0)),
        out_shape=jax.ShapeDtypeStruct((t, CONV_CH), BF16),
        scratch_shapes=[pltpu.VMEM((8, ROW_TILE + 2 * HALO, CONV_CH), F32),
                        pltpu.VMEM((ROW_TILE, CONV_CH), F32)],
        compiler_params=_params("arbitrary"),
        name="conv_module",
    )(xg, xg, xg, dw, db, lg, lb, pw)


def _attend(qg, sink_col, parts):
    m = sink_col
    scores = []
    for keys, _, mask in parts:
        s = lax.dot_general(qg, keys, (((1,), (1,)), ((), ())), preferred_element_type=F32)
        if mask is not None:
            s = jnp.where(mask, s, NEG_INF)
        m = jnp.maximum(m, jnp.max(s, axis=1, keepdims=True))
        scores.append(s)
    out = None
    for s, (_, vals, _) in zip(scores, parts):
        p = jnp.exp(s - m)
        ext = jnp.concatenate([vals, jnp.ones_like(vals)], axis=1)
        o = jnp.dot(p.astype(BF16), ext, preferred_element_type=F32)
        out = o if out is None else out + o
    denom = jnp.exp(sink_col - m) + out[:, HEAD_DIM:HEAD_DIM + 1]
    return out[:, :HEAD_DIM] / denom


def _mix_kernel(latent, seq, x_ref, mod_ref, q_ref, k_ref, v_ref, cv_ref, *rest):
    if latent:
        ck_ref, cvv_ref, sink_ref, wout_ref, n2_ref, x1_ref, h2_ref, attn_ref = rest
    else:
        sink_ref, wout_ref, n2_ref, x1_ref, h2_ref, attn_ref = rest
    prob = ROW_TILE
    width = min(seq, prob + 2 * WINDOW)
    mix_tile = x_ref.shape[0]
    steps_per_seq = max(seq // mix_tile, 1)
    for sub in range(mix_tile // prob):
        rows = slice(sub * prob, (sub + 1) * prob)
        if latent:
            q0 = (pl.program_id(0) % steps_per_seq) * mix_tile + sub * prob
            start = pl.multiple_of(jnp.clip(q0 - WINDOW, 0, seq - width), BLOCK)
            qpos = q0 + (lax.broadcasted_iota(jnp.int32, (HEAD_STACK * prob, width), 0) & (prob - 1))
            kpos = start + lax.broadcasted_iota(jnp.int32, (HEAD_STACK * prob, width), 1)
            mask = jnp.abs(kpos - qpos) <= WINDOW
        else:
            start = sub * prob
            mask = None
        for kv in range(N_KV_HEADS):
            lanes = slice(kv * HEAD_DIM, (kv + 1) * HEAD_DIM)
            parts = [(k_ref[pl.ds(start, width), lanes].astype(BF16),
                      v_ref[pl.ds(start, width), lanes].astype(BF16), mask)]
            if latent:
                parts.append((ck_ref[:, lanes].astype(BF16), cvv_ref[:, lanes].astype(BF16), None))
            for h0 in range(kv * GROUP, (kv + 1) * GROUP, HEAD_STACK):
                heads = range(h0, h0 + HEAD_STACK)
                sink_col = jnp.concatenate([jnp.full((prob, 1), sink_ref[h], F32) for h in heads], axis=0)
                qg = jnp.concatenate([q_ref[rows, h * HEAD_DIM:(h + 1) * HEAD_DIM] for h in heads], axis=0)
                o = _attend(qg, sink_col, parts)
                for n, h in enumerate(heads):
                    attn_ref[rows, h * HEAD_DIM:(h + 1) * HEAD_DIM] = o[n * prob:(n + 1) * prob].astype(BF16)
    g1 = mod_ref[2:3, :]
    sh2 = mod_ref[3:4, :]
    sc2 = mod_ref[4:5, :]
    mixed = (jnp.dot(attn_ref[...], wout_ref[0:ATTN_WIDTH, :], preferred_element_type=F32)
             + jnp.dot(cv_ref[...], wout_ref[ATTN_WIDTH:, :], preferred_element_type=F32))
    x1 = x_ref[...] + g1 * mixed
    x1_ref[...] = x1
    h2 = x1 * lax.rsqrt(jnp.mean(x1 * x1, axis=-1, keepdims=True) + EPS) * n2_ref[...]
    h2_ref[...] = (h2 * (1.0 + sc2) + sh2).astype(BF16)


def _mix(x3, mod3, mod_row, q3, k3, v3, conv3, cache, sink, w_out, norm2):
    b, seq, _ = x3.shape
    latent = cache is not None
    t = b * seq
    mix_tile = MIX_TILE
    if latent:
        assert seq % mix_tile == 0
        steps_per_seq = seq // mix_tile
        seq_of = lambda i: i // steps_per_seq
        kv_spec = pl.BlockSpec((seq, KV_WIDTH), lambda i: (seq_of(i), 0))
    else:
        assert seq == ROW_TILE and t % mix_tile == 0
        seqs_per_step = mix_tile // seq
        seq_of = lambda i: i * seqs_per_step
        kv_spec = pl.BlockSpec((mix_tile, KV_WIDTH), lambda i: (i, 0))
    tile = lambda w: pl.BlockSpec((mix_tile, w), lambda i: (i, 0))
    flat = lambda a: a.reshape(t, a.shape[-1])
    in_specs = [tile(D_MODEL),
                pl.BlockSpec((None, 6, D_MODEL), lambda i: (mod_row(seq_of(i)), 0, 0)),
                tile(ATTN_WIDTH), kv_spec, kv_spec, tile(CONV_CH)]
    args = [flat(x3), mod3, flat(q3), flat(k3), flat(v3), flat(conv3)]
    if latent:
        past = cache[0].shape[1]
        in_specs += [pl.BlockSpec((None, past, KV_WIDTH), lambda i: (seq_of(i), 0, 0))] * 2
        args += list(cache)
    in_specs += [pl.BlockSpec(memory_space=pltpu.SMEM), _full((D_MODEL, D_MODEL)), _full((1, D_MODEL))]
    args += [sink, w_out, norm2]
    x1, h2 = pl.pallas_call(
        functools.partial(_mix_kernel, latent, seq),
        grid=(t // mix_tile,),
        in_specs=in_specs,
        out_specs=[tile(D_MODEL), tile(D_MODEL)],
        out_shape=[jax.ShapeDtypeStruct((t, D_MODEL), F32),
                   jax.ShapeDtypeStruct((t, D_MODEL), BF16)],
        scratch_shapes=[pltpu.VMEM((mix_tile, ATTN_WIDTH), BF16)],
        compiler_params=_params("arbitrary"),
        name="mix_latent" if latent else "mix_context",
    )(*args)
    return x1.reshape(b, seq, D_MODEL), h2.reshape(b, seq, D_MODEL)


def _merge_exchange_pairs(n):
    pairs, p = [], 1
    while p < n:
        k = p
        while k >= 1:
            for j in range(k % p, n - k, 2 * k):
                for i in range(min(k, n - j - k)):
                    if (i + j) // (2 * p) == (i + j + k) // (2 * p):
                        pairs.append((i + j, i + j + k))
            k //= 2
        p *= 2
    return pairs


def _top_desc_sorted(x, count):
    lists = [x[8 * g:8 * (g + 1)] for g in range(count)]
    for a, b in _merge_exchange_pairs(count):
        lists[a], lists[b] = jnp.maximum(lists[a], lists[b]), jnp.minimum(lists[a], lists[b])
    slot = lax.broadcasted_iota(jnp.int32, (count, x.shape[1]), 0)
    out = jnp.zeros((count, x.shape[1]), F32)
    for k in range(count):
        m = jnp.max(lists[0], axis=0, keepdims=True)
        out = jnp.where(slot == k, m, out)
        hit = lists[0] == m
        for lvl in range(count - 1 - k):
            lists[lvl] = jnp.where(hit, lists[lvl + 1], lists[lvl])
    return out


def _top_pair_sums(v1, v2, count):
    cols = v1.shape[1]
    row = lax.broadcasted_iota(jnp.int32, (8, cols), 0)
    lists = [jnp.where(row < count // (k + 1), v1[0:8] + v2[k:k + 1], NEG_INF) for k in range(count)]
    tail = v1[8:count] + v2[0:1]
    slot = lax.broadcasted_iota(jnp.int32, (count, cols), 0)
    out = jnp.zeros((count, cols), F32)
    for n in range(count):
        m = jnp.max(jnp.maximum(lists[0], tail), axis=0, keepdims=True)
        out = jnp.where(slot == n, m, out)
        hit = lists[0] == m
        for lvl in range(count - 1 - n):
            lists[lvl] = jnp.where(hit, lists[lvl + 1], lists[lvl])
        tail = jnp.where(tail == m, NEG_INF, tail)
    return out


def _route_kernel(h2_ref, wq_ref, keys_ref, h2t_ref, rank_ref, p2_ref, cnt_ref, p1_ref, qp_ref):
    h2 = h2_ref[...]
    h2t_ref[...] = pltpu.bitcast(h2.astype(F32).T.astype(BF16), jnp.uint32)
    qp_ref[...] = jnp.dot(h2, wq_ref[...], preferred_element_type=F32).astype(BF16)
    nt = (((1,), (1,)), ((), ()))

    def head(h, carry):
        col = pl.multiple_of(h * PEER_QDIM, PEER_QDIM)
        s1 = lax.dot_general(keys_ref[h, 0], qp_ref[:, pl.ds(col, PEER_HALF)], nt,
                             preferred_element_type=F32)
        s2 = lax.dot_general(keys_ref[h, 1], qp_ref[:, pl.ds(col + PEER_HALF, PEER_HALF)], nt,
                             preferred_element_type=F32)
        v1 = _top_desc_sorted(s1, PEER_TOPK)
        v2 = _top_desc_sorted(s2, PEER_TOPK)
        best = _top_pair_sums(v1, v2, PEER_TOPK)
        tau = best[PEER_TOPK - 1:PEER_TOPK]
        z = jnp.sum(jnp.exp(best - best[0:1]), axis=0, keepdims=True)
        rank = jnp.zeros_like(s2)
        cnt_top = jnp.zeros_like(v1)
        for k in range(PEER_TOPK):
            vk = v2[k:k + 1]
            rank = jnp.where(vk > s2, k + 1.0, rank)
            cnt_top = jnp.where(v1 + vk >= tau, k + 1.0, cnt_top)
        cnt = jnp.zeros_like(s1)
        for j in range(PEER_TOPK):
            cnt = jnp.where(s1 == v1[j:j + 1], cnt_top[j:j + 1], cnt)
        steps = N_EXPERTS // PEER_EXPERTS
        cnt_ref[:, h] = cnt.reshape(steps, N_KEYS // steps, ROUTE_TILE)
        p1_ref[:, h] = jnp.exp(s1 - v1[0:1]).reshape(steps, N_KEYS // steps, ROUTE_TILE)
        rank_ref[h] = pltpu.bitcast(rank.astype(BF16), jnp.uint32)
        p2_ref[h] = pltpu.bitcast((jnp.exp(s2 - v2[0:1]) * (0.5 / z)).astype(BF16), jnp.uint32)
        return carry

    lax.fori_loop(0, PEER_HEADS, head, 0, unroll=4)


def _route(h2, wq, keys):
    t = h2.shape[0]
    tok4 = pl.BlockSpec((PEER_HEADS, N_KEYS // 2, ROUTE_TILE), lambda i: (0, 0, i))
    steps = N_EXPERTS // PEER_EXPERTS
    tok3 = pl.BlockSpec((steps, PEER_HEADS, N_KEYS // steps, ROUTE_TILE), lambda i: (0, 0, 0, i))
    return pl.pallas_call(
        _route_kernel,
        grid=(t // ROUTE_TILE,),
        in_specs=[pl.BlockSpec((ROUTE_TILE, D_MODEL), lambda i: (i, 0)),
                  _full((D_MODEL, PEER_HEADS * PEER_QDIM)),
                  _full((PEER_HEADS, 2, N_KEYS, PEER_HALF))],
        out_specs=[pl.BlockSpec((D_MODEL // 2, ROUTE_TILE), lambda i: (0, i)), tok4, tok4, tok3, tok3],
        out_shape=[jax.ShapeDtypeStruct((D_MODEL // 2, t), jnp.uint32),
                   jax.ShapeDtypeStruct((PEER_HEADS, N_KEYS // 2, t), jnp.uint32),
                   jax.ShapeDtypeStruct((PEER_HEADS, N_KEYS // 2, t), jnp.uint32),
                   jax.ShapeDtypeStruct((steps, PEER_HEADS, N_KEYS // steps, t), F32),
                   jax.ShapeDtypeStruct((steps, PEER_HEADS, N_KEYS // steps, t), F32)],
        scratch_shapes=[pltpu.VMEM((ROUTE_TILE, PEER_HEADS * PEER_QDIM), BF16)],
        compiler_params=_params("arbitrary"),
        name="peer_route",
    )(h2, wq, keys)


def _pack_kernel(transpose, x_ref, o_ref):
    x = x_ref[...]
    if transpose:
        x = x.T
    o_ref[...] = pltpu.bitcast(x.astype(BF16), jnp.uint32)


def _pack_table(table, transpose):
    rows, cols = table.shape
    blk = 2048
    if transpose:
        out_spec = pl.BlockSpec((cols // 2, blk), lambda i: (0, i))
        out_shape = jax.ShapeDtypeStruct((cols // 2, rows), jnp.uint32)
    else:
        out_spec = pl.BlockSpec((blk // 2, cols), lambda i: (i, 0))
        out_shape = jax.ShapeDtypeStruct((rows // 2, cols), jnp.uint32)
    return pl.pallas_call(
        functools.partial(_pack_kernel, transpose),
        grid=(rows // blk,),
        in_specs=[pl.BlockSpec((blk, cols), lambda i: (i, 0))],
        out_specs=out_spec,
        out_shape=out_shape,
        compiler_params=_params("arbitrary"),
        name="pack_table_t" if transpose else "pack_table",
    )(table)


def _gate_rows(cs, act_ref, wt_ref, rank_ref, p2_ref, cnt_ref, p1_ref):
    tiles = (N_KEYS // BF16_SUBLANES, BF16_SUBLANES, LANES)
    for lt in range(PEER_TOKENS // LANES):
        lanes = slice(lt * LANES, (lt + 1) * LANES)
        gates = [jnp.zeros(tiles, BF16) for _ in cs]
        for h in range(PEER_HEADS):
            rank = pltpu.bitcast(rank_ref[h, :, lanes], BF16).reshape(tiles)
            p2 = pltpu.bitcast(p2_ref[h, :, lanes], BF16).reshape(tiles)
            for n, c in enumerate(cs):
                cnt = jnp.broadcast_to(cnt_ref[h, c:c + 1, lanes], (BF16_SUBLANES, LANES)).astype(BF16)
                p1 = jnp.broadcast_to(p1_ref[h, c:c + 1, lanes], (BF16_SUBLANES, LANES)).astype(BF16)
                gates[n] = gates[n] + jnp.where(rank < cnt[None], p2, jnp.zeros((), BF16)) * p1[None]
        for n, c in enumerate(cs):
            rows = slice(c * N_KEYS, (c + 1) * N_KEYS)
            a = act_ref[rows, lanes]
            gl = a + a * lax.erf(a * (1.0 / math.sqrt(2.0)))
            wt_ref[rows, lanes] = gl.astype(BF16) * gates[n].reshape(N_KEYS, LANES)


def _expert_kernel(pairs_per_block, u_ref, vt_ref, h2t_ref, rank_a, p2_a, rank_b, p2_b, cnt_a, p1_a,
                   cnt_b, p1_b, x1_ref, mod_ref, y_ref, acc_ref, act_a, act_b, wt_a, wt_b):
    g = pl.program_id(0)

    @pl.when(g == 0)
    def _():
        acc_ref[...] = jnp.zeros_like(acc_ref)
        act_b[...] = jnp.zeros_like(act_b)
        wt_a[...] = jnp.zeros_like(wt_a)

    key_rows = PEER_EXPERTS // N_KEYS

    def half_step(half, wt_old, act_old, wt_new, act_new, rank_ref, p2_ref, cnt_ref, p1_ref):
        gate_rows = key_rows // 2
        for idx in range(2):
            @pl.when(g >= -(half * 2 + idx))
            def _(idx=idx):
                for c in range(idx * gate_rows, (idx + 1) * gate_rows, GATE_ROWS):
                    _gate_rows(range(c, c + GATE_ROWS), act_old, wt_new, rank_ref, p2_ref, cnt_ref, p1_ref)
                if idx == 0:
                    vt = pltpu.bitcast(vt_ref[:, half * PEER_EXPERTS:(half + 1) * PEER_EXPERTS], BF16)
                    acc_ref[...] += jnp.dot(vt, wt_old[...], preferred_element_type=F32)
                else:
                    lo = half * PEER_EXPERTS // 2
                    u = pltpu.bitcast(u_ref[lo:lo + PEER_EXPERTS // 2, :], BF16)
                    act_new[...] = jnp.dot(u, pltpu.bitcast(h2t_ref[...], BF16), preferred_element_type=F32)

    half_step(0, wt_a, act_b, wt_b, act_a, rank_a, p2_a, cnt_a, p1_a)
    half_step(1, wt_b, act_a, wt_a, act_b, rank_b, p2_b, cnt_b, p1_b)

    @pl.when((g > 0) & (g % pairs_per_block == 0))
    def _():
        g2 = mod_ref[5:6, :]
        y_ref[...] = x1_ref[...] + g2 * acc_ref[...].T
        acc_ref[...] = jnp.zeros_like(acc_ref)


def _experts(u_bf, vt_bf, h2t, rank, p2, cnt, p1, x1, mod3, mod_row):
    t = x1.shape[0]
    ppb = N_EXPERTS // (2 * PEER_EXPERTS)
    steps = (t // PEER_TOKENS) * ppb + 1
    cur = lambda g: jnp.minimum(g, steps - 2)
    prv = lambda g: jnp.maximum(g - 1, 0)
    packed = lambda f: pl.BlockSpec((PEER_HEADS, N_KEYS // 2, PEER_TOKENS), lambda g: (0, 0, f(g) // ppb))
    rows = lambda f, half: pl.BlockSpec((None, PEER_HEADS, PEER_EXPERTS // N_KEYS, PEER_TOKENS),
                                        lambda g: (2 * (f(g) % ppb) + half, 0, 0, f(g) // ppb))
    return pl.pallas_call(
        functools.partial(_expert_kernel, ppb),
        grid=(steps,),
        in_specs=[pl.BlockSpec((PEER_EXPERTS, D_MODEL), lambda g: (cur(g) % ppb, 0)),
                  pl.BlockSpec((D_MODEL // 2, 2 * PEER_EXPERTS), lambda g: (0, prv(g) % ppb)),
                  pl.BlockSpec((D_MODEL // 2, PEER_TOKENS), lambda g: (0, cur(g) // ppb)),
                  packed(prv), packed(prv), packed(cur), packed(cur),
                  rows(prv, 1), rows(prv, 1), rows(cur, 0), rows(cur, 0),
                  pl.BlockSpec((PEER_TOKENS, D_MODEL), lambda g: (prv(g) // ppb, 0)),
                  pl.BlockSpec((None, 6, D_MODEL), lambda g: (mod_row(prv(g) // ppb), 0, 0))],
        out_specs=pl.BlockSpec((PEER_TOKENS, D_MODEL), lambda g: (prv(g) // ppb, 0)),
        out_shape=jax.ShapeDtypeStruct((t, D_MODEL), F32),
        scratch_shapes=[pltpu.VMEM((D_MODEL, PEER_TOKENS), F32),
                        pltpu.VMEM((PEER_EXPERTS, PEER_TOKENS), F32),
                        pltpu.VMEM((PEER_EXPERTS, PEER_TOKENS), F32),
                        pltpu.VMEM((PEER_EXPERTS, PEER_TOKENS), BF16),
                        pltpu.VMEM((PEER_EXPERTS, PEER_TOKENS), BF16)],
        compiler_params=_params("arbitrary"),
        name="peer_experts",
    )(u_bf, vt_bf, h2t, rank, p2, rank, p2, cnt, p1, cnt, p1, x1, mod3)


def _rope_tables(seq_len):
    rows = seq_len // GRID_W
    row = jnp.repeat(jnp.arange(rows, dtype=F32), GRID_W)
    col = jnp.tile(jnp.arange(GRID_W, dtype=F32), rows)
    inv = ROPE_THETA ** (-jnp.arange(AXIS_FREQS, dtype=F32) / AXIS_FREQS)
    ang_row = row[:, None] * inv
    ang_col = col[:, None] * inv
    zeros = jnp.zeros_like(ang_row)
    cos = jnp.concatenate([jnp.cos(ang_row)] * 2 + [jnp.cos(ang_col)] * 2, axis=-1)
    sin_lo = jnp.concatenate([-jnp.sin(ang_row), zeros, -jnp.sin(ang_col), zeros], axis=-1)
    sin_hi = jnp.concatenate([zeros, jnp.sin(ang_row), zeros, jnp.sin(ang_col)], axis=-1)
    return tuple(jnp.tile(t, (1, N_HEADS)) for t in (cos, sin_lo, sin_hi))


def _head_mean_matrix(width):
    idx = jnp.arange(width) // HEAD_DIM
    return jnp.where(idx[:, None] == idx[None, :], 1.0 / HEAD_DIM, 0.0).astype(BF16)


def _trunk(x3, mod3, mod_of_seq, cache, w, rope):
    b, seq, _ = x3.shape
    t = b * seq
    tiles_per_seq = seq // ROW_TILE
    q, k, v, xg, *cache_out = _inproj(x3.reshape(t, D_MODEL), mod3, mod_of_seq, seq, w["norm1"], w["w_in"],
                                      w["qn"], w["kn"], w["gq"], w["gk"], rope)
    conv = _conv_module(xg, tiles_per_seq, w["conv_dw"], w["conv_dw_b"], w["conv_ln_g"], w["conv_ln_b"],
                        w["conv_pw"])
    x1, h2 = _mix(x3, mod3, mod_of_seq, q.reshape(b, seq, ATTN_WIDTH), k.reshape(b, seq, KV_WIDTH),
                  v.reshape(b, seq, KV_WIDTH), conv.reshape(b, seq, CONV_CH), cache, w["sink"],
                  w["w_out"], w["norm2"])
    h2t, rank, p2, cnt, p1 = _route(h2.reshape(t, D_MODEL), w["peer_wq"], w["peer_keys"])
    per_seq = seq // PEER_TOKENS
    if per_seq == 0:
        seqs_per_block = PEER_TOKENS // seq
        row_of_block = lambda i: mod_of_seq(i * seqs_per_block)
    else:
        row_of_block = lambda i: mod_of_seq(i // per_seq)
    y = _experts(w["peer_u"], w["peer_vt"], h2t, rank, p2, cnt, p1, x1.reshape(t, D_MODEL), mod3, row_of_block)
    return y.reshape(b, seq, D_MODEL), cache_out


def kernel(x_prompt, x_sample, cache_k, cache_v, c, c_ctx, norm1, norm2, w_ada, b_ada, w_in, w_out,
           q_norm, k_norm, attn_sink, conv_dw, conv_dw_b, conv_ln_g, conv_ln_b, conv_pw,
           peer_wq, peer_keys, peer_u, peer_v):
    depth = norm1.shape[0]
    batch, seq, _ = x_prompt.shape
    dec_batch, dec_seq, _ = x_sample.shape
    past = cache_k.shape[2]
    assert dec_batch + 1 <= 8 and depth >= 1
    cond8 = jnp.concatenate([c_ctx[None, :], c, jnp.zeros((8 - 1 - dec_batch, D_MODEL), F32)], axis=0)
    rope = _rope_tables(dec_seq)
    xp, xs = x_prompt, x_sample
    new_ks, new_vs = [], []
    for l in range(depth):
        w = dict(
            norm1=norm1[l][None], norm2=norm2[l][None],
            w_in=w_in[l].astype(BF16), w_out=w_out[l].astype(BF16),
            qn=jnp.tile(q_norm[l], N_HEADS)[None], kn=jnp.tile(k_norm[l], N_KV_HEADS)[None],
            gq=_head_mean_matrix(ATTN_WIDTH), gk=_head_mean_matrix(KV_WIDTH),
            sink=attn_sink[l],
            conv_dw=conv_dw[l], conv_dw_b=conv_dw_b[l][None], conv_ln_g=conv_ln_g[l][None],
            conv_ln_b=conv_ln_b[l][None], conv_pw=conv_pw[l].astype(BF16),
            peer_wq=peer_wq[l].astype(BF16), peer_keys=peer_keys[l].astype(BF16),
            peer_u=_pack_table(peer_u[l], False), peer_vt=_pack_table(peer_v[l], True),
        )
        mod3 = _modulation(cond8, w_ada[l], b_ada[l]).reshape(8, 6, D_MODEL)
        xp, (kt_c, vt_c) = _trunk(xp, mod3, lambda s: 0, None, w, None)
        new_ks.append(kt_c.reshape(batch, N_KV_HEADS, HEAD_DIM, seq).transpose(0, 3, 1, 2))
        new_vs.append(vt_c.reshape(batch, N_KV_HEADS, HEAD_DIM, seq).transpose(0, 3, 1, 2))
        cache = (cache_k[:, l].reshape(dec_batch, past, KV_WIDTH), cache_v[:, l].reshape(dec_batch, past, KV_WIDTH))
        xs, _ = _trunk(xs, mod3, lambda s: s + 1, cache, w, rope)
    join = (lambda parts: parts[0][:, None]) if depth == 1 else (lambda parts: jnp.stack(parts, axis=1))
    return (xp, xs, join(new_ks), join(new_vs))
```

```python
import functools
import math

import jax
import jax.numpy as jnp
from jax import lax
from jax.experimental import pallas as pl
from jax.experimental.pallas import tpu as pltpu

F32 = jnp.float32
BF16 = jnp.bfloat16

D_MODEL = 1024
GRID_W = 64
N_HEADS = 8
N_KV_HEADS = 2
GROUP = N_HEADS // N_KV_HEADS
HEAD_DIM = 64
ATTN_WIDTH = N_HEADS * HEAD_DIM
KV_WIDTH = N_KV_HEADS * HEAD_DIM
WINDOW = 128
BLOCK = 128
ROPE_THETA = 10000.0
AXIS_DIM = HEAD_DIM // 2
AXIS_FREQS = AXIS_DIM // 2
CONV_CH = D_MODEL - ATTN_WIDTH
CONV_K = 31
CONV_PAD = CONV_K // 2
IN_WIDTH = ATTN_WIDTH + 2 * KV_WIDTH + 2 * CONV_CH
N_KEYS = 128
N_EXPERTS = N_KEYS * N_KEYS
PEER_HEADS = 8
PEER_QDIM = 256
PEER_HALF = PEER_QDIM // 2
PEER_TOPK = 16
EPS = 1e-6

LANES = 128
BF16_SUBLANES = 16
ROW_TILE = 256
INPROJ_TILE = 512
HEAD_STACK = 4
MIX_TILE = 512
HALO = 16
ROUTE_TILE = 512
PEER_TOKENS = 512
PEER_EXPERTS = 1024
GATE_ROWS = 4
VMEM_LIMIT = 48 * 1024 * 1024

NEG_INF = float("-inf")


def _params(*sem, flags=None):
    return pltpu.CompilerParams(dimension_semantics=sem, vmem_limit_bytes=VMEM_LIMIT, flags=flags)


def _full(shape):
    return pl.BlockSpec(shape, lambda *_: (0,) * len(shape))


def _mod_kernel(cond_ref, w_ref, b_ref, o_ref):
    cnd = cond_ref[...]
    act = cnd * jax.nn.sigmoid(cnd)
    o_ref[...] = jnp.dot(act, w_ref[...], precision=lax.Precision.HIGHEST,
                         preferred_element_type=F32) + b_ref[...]


def _modulation(cond8, w_ada, b_ada):
    n = w_ada.shape[1]
    nb = n // D_MODEL
    return pl.pallas_call(
        _mod_kernel,
        grid=(nb,),
        in_specs=[_full((8, D_MODEL)),
                  pl.BlockSpec((D_MODEL, D_MODEL), lambda j: (0, j)),
                  pl.BlockSpec((1, D_MODEL), lambda j: (0, j))],
        out_specs=pl.BlockSpec((8, D_MODEL), lambda j: (0, j)),
        out_shape=jax.ShapeDtypeStruct((8, n), F32),
        compiler_params=_params("arbitrary"),
        name="modulation",
    )(cond8, w_ada, b_ada.reshape(1, n))


def _group_rms(t, gmat, gain):
    ms = jnp.dot((t * t).astype(BF16), gmat, preferred_element_type=F32)
    return t * lax.rsqrt(ms + EPS) * gain


def _rope(t, cos, sin_lo, sin_hi):
    w = t.shape[1]
    return (t * cos + pltpu.roll(t, w - AXIS_FREQS, 1) * sin_lo
            + pltpu.roll(t, AXIS_FREQS, 1) * sin_hi)


def _inproj_kernel(latent, x_ref, mod_ref, n1_ref, win_ref, qn_ref, kn_ref, gq_ref, gk_ref, *rest):
    if latent:
        cos_ref, slo_ref, shi_ref, q_ref, k_ref, v_ref, xg_ref = rest
    else:
        q_ref, k_ref, v_ref, xg_ref, kt_ref, vt_ref = rest
    x = x_ref[...]
    sh1 = mod_ref[0:1, :]
    sc1 = mod_ref[1:2, :]
    h = x * lax.rsqrt(jnp.mean(x * x, axis=-1, keepdims=True) + EPS) * n1_ref[...]
    h = h * (1.0 + sc1) + sh1
    z = jnp.dot(h.astype(BF16), win_ref[...], preferred_element_type=F32)
    q = _group_rms(z[:, :ATTN_WIDTH], gq_ref[...], qn_ref[...])
    k = _group_rms(z[:, ATTN_WIDTH:ATTN_WIDTH + KV_WIDTH], gk_ref[...], kn_ref[...])
    if latent:
        q = _rope(q, cos_ref[...], slo_ref[...], shi_ref[...])
        k = _rope(k, cos_ref[:, :KV_WIDTH], slo_ref[:, :KV_WIDTH], shi_ref[:, :KV_WIDTH])
    q_ref[...] = (q * (1.0 / math.sqrt(HEAD_DIM))).astype(BF16)
    v = z[:, ATTN_WIDTH + KV_WIDTH:ATTN_WIDTH + 2 * KV_WIDTH]
    k_ref[...] = k
    v_ref[...] = v
    if not latent:
        seq = kt_ref.shape[2]
        for s in range(kt_ref.shape[0]):
            kt_ref[s] = k[s * seq:(s + 1) * seq, :].T
            vt_ref[s] = v[s * seq:(s + 1) * seq, :].T
    u = z[:, ATTN_WIDTH + 2 * KV_WIDTH:]
    xg_ref[...] = u[:, :CONV_CH] * jax.nn.sigmoid(u[:, CONV_CH:])


def _inproj(x2d, mod3, mod_of_seq, seq, norm1, w_in, qn, kn, gq, gk, rope):
    t = x2d.shape[0]
    latent = rope is not None
    row = lambda i: (i, 0)
    tile = INPROJ_TILE
    if seq >= tile:
        assert seq % tile == 0
        tiles_per_seq = seq // tile
        mod_row = lambda i: mod_of_seq(i // tiles_per_seq)
    else:
        assert tile % seq == 0 and not latent
        tiles_per_seq = 1
        mod_row = lambda i: mod_of_seq(i * (tile // seq))
    in_specs = [pl.BlockSpec((tile, D_MODEL), row),
                pl.BlockSpec((None, 6, D_MODEL), lambda i: (mod_row(i), 0, 0)),
                _full((1, D_MODEL)), _full((D_MODEL, IN_WIDTH)),
                _full((1, ATTN_WIDTH)), _full((1, KV_WIDTH)),
                _full((ATTN_WIDTH, ATTN_WIDTH)), _full((KV_WIDTH, KV_WIDTH))]
    args = [x2d, mod3, norm1, w_in, qn, kn, gq, gk]
    if latent:
        pos = lambda i: (i % tiles_per_seq, 0)
        in_specs += [pl.BlockSpec((tile, ATTN_WIDTH), pos)] * 3
        args += list(rope)
    out_specs = [pl.BlockSpec((tile, ATTN_WIDTH), row),
                 pl.BlockSpec((tile, KV_WIDTH), row),
                 pl.BlockSpec((tile, KV_WIDTH), row),
                 pl.BlockSpec((tile, CONV_CH), row)]
    out_shape = [jax.ShapeDtypeStruct((t, ATTN_WIDTH), BF16),
                 jax.ShapeDtypeStruct((t, KV_WIDTH), F32),
                 jax.ShapeDtypeStruct((t, KV_WIDTH), F32),
                 jax.ShapeDtypeStruct((t, CONV_CH), F32)]
    if not latent:
        out_specs += [pl.BlockSpec((tile // seq, KV_WIDTH, seq), lambda i: (i, 0, 0))] * 2
        out_shape += [jax.ShapeDtypeStruct((t // seq, KV_WIDTH, seq), F32)] * 2
    return pl.pallas_call(
        functools.partial(_inproj_kernel, latent),
        grid=(t // tile,),
        in_specs=in_specs,
        out_specs=out_specs,
        out_shape=out_shape,
        compiler_params=_params("arbitrary"),
        name="inproj_latent" if latent else "inproj_context",
    )(*args)


def _conv_kernel(tiles_per_seq, prev_ref, cur_ref, nxt_ref, dw_ref, db_ref, lg_ref, lb_ref, pw_ref,
                 o_ref, xs_ref, act_ref):
    i = pl.program_id(0)
    pos = i % tiles_per_seq
    zero = jnp.zeros((HALO, CONV_CH), F32)
    xs_ref[0, 0:HALO, :] = jnp.where(pos > 0, prev_ref[...], zero)
    xs_ref[0, HALO:HALO + ROW_TILE, :] = cur_ref[...]
    xs_ref[0, HALO + ROW_TILE:2 * HALO + ROW_TILE, :] = jnp.where(pos < tiles_per_seq - 1, nxt_ref[...], zero)
    span = ROW_TILE + 2 * HALO - 8
    for r in range(1, 8):
        xs_ref[r, 0:span, :] = xs_ref[0, r:r + span, :]
    chunk = 64

    def body(c, carry):
        base = pl.multiple_of(c * chunk, chunk)
        acc = jnp.zeros((chunk, CONV_CH), F32) + db_ref[...]
        for k in range(CONV_K):
            kk = k + HALO - CONV_PAD
            start = pl.multiple_of(base + 8 * (kk // 8), 8)
            acc = acc + xs_ref[kk % 8, pl.ds(start, chunk), :] * dw_ref[k:k + 1, :]
        act_ref[pl.ds(base, chunk), :] = acc
        return carry

    lax.fori_loop(0, ROW_TILE // chunk, body, 0)
    acc = act_ref[...]
    mu = jnp.mean(acc, axis=-1, keepdims=True)
    cen = acc - mu
    var = jnp.mean(cen * cen, axis=-1, keepdims=True)
    y = cen * lax.rsqrt(var + EPS) * lg_ref[...] + lb_ref[...]
    y = y * jax.nn.sigmoid(y)
    o_ref[...] = jnp.dot(y.astype(BF16), pw_ref[...], preferred_element_type=F32).astype(BF16)


def _conv_module(xg, tiles_per_seq, dw, db, lg, lb, pw):
    t = xg.shape[0]
    nt = t // ROW_TILE
    per = ROW_TILE // HALO
    last = t // HALO - 1
    return pl.pallas_call(
        functools.partial(_conv_kernel, tiles_per_seq),
        grid=(nt,),
        in_specs=[pl.BlockSpec((HALO, CONV_CH), lambda i: (jnp.maximum(i * per - 1, 0), 0)),
                  pl.BlockSpec((ROW_TILE, CONV_CH), lambda i: (i, 0)),
                  pl.BlockSpec((HALO, CONV_CH), lambda i: (jnp.minimum((i + 1) * per, last), 0)),
                  _full((CONV_K, CONV_CH)), _full((1, CONV_CH)), _full((1, CONV_CH)), _full((1, CONV_CH)),
                  _full((CONV_CH, CONV_CH))],
        out_specs=pl.BlockSpec((ROW_TILE, CONV_CH), lambda i: (i, 0)),
        out_shape=jax.ShapeDtypeStruct((t, CONV_CH), BF16),
        scratch_shapes=[pltpu.VMEM((8, ROW_TILE + 2 * HALO, CONV_CH), F32),
                        pltpu.VMEM((ROW_TILE, CONV_CH), F32)],
        compiler_params=_params("arbitrary"),
        name="conv_module",
    )(xg, xg, xg, dw, db, lg, lb, pw)


def _attend(qg, sink_col, parts):
    m = sink_col
    scores = []
    for keys, _, mask, keys_transposed in parts:
        dims = (((1,), (0,)), ((), ())) if keys_transposed else (((1,), (1,)), ((), ()))
        s = lax.dot_general(qg, keys, dims, preferred_element_type=F32)
        if mask is not None:
            s = jnp.where(mask, s, NEG_INF)
        m = jnp.maximum(m, jnp.max(s, axis=1, keepdims=True))
        scores.append(s)
    out = None
    for s, (_, vals, _, _) in zip(scores, parts):
        p = jnp.exp(s - m)
        ext = jnp.concatenate([vals, jnp.ones_like(vals)], axis=1)
        o = jnp.dot(p.astype(BF16), ext, preferred_element_type=F32)
        out = o if out is None else out + o
    denom = jnp.exp(sink_col - m) + out[:, HEAD_DIM:HEAD_DIM + 1]
    return out[:, :HEAD_DIM] / denom


def _mix_kernel(latent, seq, x_ref, mod_ref, q_ref, k_ref, v_ref, cv_ref, *rest):
    if latent:
        ck_ref, cvv_ref, sink_ref, wout_ref, n2_ref, x1_ref, h2_ref, attn_ref = rest
    else:
        sink_ref, wout_ref, n2_ref, x1_ref, h2_ref, attn_ref = rest
    prob = ROW_TILE
    width = min(seq, prob + 2 * WINDOW)
    mix_tile = x_ref.shape[0]
    steps_per_seq = max(seq // mix_tile, 1)
    for sub in range(mix_tile // prob):
        rows = slice(sub * prob, (sub + 1) * prob)
        if latent:
            q0 = (pl.program_id(0) % steps_per_seq) * mix_tile + sub * prob
            start = pl.multiple_of(jnp.clip(q0 - WINDOW, 0, seq - width), BLOCK)
            qpos = q0 + (lax.broadcasted_iota(jnp.int32, (HEAD_STACK * prob, width), 0) & (prob - 1))
            kpos = start + lax.broadcasted_iota(jnp.int32, (HEAD_STACK * prob, width), 1)
            mask = jnp.abs(kpos - qpos) <= WINDOW
        else:
            start = sub * prob
            mask = None
        for kv in range(N_KV_HEADS):
            lanes = slice(kv * HEAD_DIM, (kv + 1) * HEAD_DIM)
            vals = v_ref[pl.ds(start, width), lanes].astype(BF16)
            if latent:
                parts = [(k_ref[pl.ds(start, width), lanes].astype(BF16), vals, mask, False),
                         (ck_ref[:, lanes].astype(BF16), cvv_ref[:, lanes].astype(BF16), None, False)]
            else:
                parts = [(k_ref[sub, lanes, :].astype(BF16), vals, mask, True)]
            for h0 in range(kv * GROUP, (kv + 1) * GROUP, HEAD_STACK):
                heads = range(h0, h0 + HEAD_STACK)
                sink_col = jnp.concatenate([jnp.full((prob, 1), sink_ref[h], F32) for h in heads], axis=0)
                qg = jnp.concatenate([q_ref[rows, h * HEAD_DIM:(h + 1) * HEAD_DIM] for h in heads], axis=0)
                o = _attend(qg, sink_col, parts)
                for n, h in enumerate(heads):
                    attn_ref[rows, h * HEAD_DIM:(h + 1) * HEAD_DIM] = o[n * prob:(n + 1) * prob].astype(BF16)
    g1 = mod_ref[2:3, :]
    sh2 = mod_ref[3:4, :]
    sc2 = mod_ref[4:5, :]
    mixed = (jnp.dot(attn_ref[...], wout_ref[0:ATTN_WIDTH, :], preferred_element_type=F32)
             + jnp.dot(cv_ref[...], wout_ref[ATTN_WIDTH:, :], preferred_element_type=F32))
    x1 = x_ref[...] + g1 * mixed
    x1_ref[...] = x1
    h2 = x1 * lax.rsqrt(jnp.mean(x1 * x1, axis=-1, keepdims=True) + EPS) * n2_ref[...]
    h2_ref[...] = (h2 * (1.0 + sc2) + sh2).astype(BF16)


def _mix(x3, mod3, mod_row, q3, k3, v3, conv3, cache, sink, w_out, norm2):
    b, seq, _ = x3.shape
    latent = cache is not None
    t = b * seq
    mix_tile = MIX_TILE
    if latent:
        assert seq % mix_tile == 0
        steps_per_seq = seq // mix_tile
        seq_of = lambda i: i // steps_per_seq
        kv_spec = pl.BlockSpec((seq, KV_WIDTH), lambda i: (seq_of(i), 0))
    else:
        assert seq == ROW_TILE and t % mix_tile == 0
        seqs_per_step = mix_tile // seq
        seq_of = lambda i: i * seqs_per_step
        kv_spec = pl.BlockSpec((mix_tile, KV_WIDTH), lambda i: (i, 0))
    tile = lambda w: pl.BlockSpec((mix_tile, w), lambda i: (i, 0))
    flat = lambda a: a.reshape(t, a.shape[-1])
    in_specs = [tile(D_MODEL),
                pl.BlockSpec((None, 6, D_MODEL), lambda i: (mod_row(seq_of(i)), 0, 0)),
                tile(ATTN_WIDTH), kv_spec, kv_spec, tile(CONV_CH)]
    args = [flat(x3), mod3, flat(q3), flat(k3) if latent else k3, flat(v3), flat(conv3)]
    if not latent:
        in_specs[3] = pl.BlockSpec((mix_tile // seq, KV_WIDTH, seq), lambda i: (i, 0, 0))
    if latent:
        past = cache[0].shape[1]
        in_specs += [pl.BlockSpec((None, past, KV_WIDTH), lambda i: (seq_of(i), 0, 0))] * 2
        args += list(cache)
    in_specs += [pl.BlockSpec(memory_space=pltpu.SMEM), _full((D_MODEL, D_MODEL)), _full((1, D_MODEL))]
    args += [sink, w_out, norm2]
    x1, h2 = pl.pallas_call(
        functools.partial(_mix_kernel, latent, seq),
        grid=(t // mix_tile,),
        in_specs=in_specs,
        out_specs=[tile(D_MODEL), tile(D_MODEL)],
        out_shape=[jax.ShapeDtypeStruct((t, D_MODEL), F32),
                   jax.ShapeDtypeStruct((t, D_MODEL), BF16)],
        scratch_shapes=[pltpu.VMEM((mix_tile, ATTN_WIDTH), BF16)],
        compiler_params=_params("arbitrary"),
        name="mix_latent" if latent else "mix_context",
    )(*args)
    return x1.reshape(b, seq, D_MODEL), h2.reshape(b, seq, D_MODEL)


def _merge_exchange_pairs(n):
    pairs, p = [], 1
    while p < n:
        k = p
        while k >= 1:
            for j in range(k % p, n - k, 2 * k):
                for i in range(min(k, n - j - k)):
                    if (i + j) // (2 * p) == (i + j + k) // (2 * p):
                        pairs.append((i + j, i + j + k))
            k //= 2
        p *= 2
    return pairs


def _top_desc_sorted(x, count):
    lists = [x[8 * g:8 * (g + 1)] for g in range(count)]
    for a, b in _merge_exchange_pairs(count):
        lists[a], lists[b] = jnp.maximum(lists[a], lists[b]), jnp.minimum(lists[a], lists[b])
    slot = lax.broadcasted_iota(jnp.int32, (count, x.shape[1]), 0)
    out = jnp.zeros((count, x.shape[1]), F32)
    for k in range(count):
        m = jnp.max(lists[0], axis=0, keepdims=True)
        out = jnp.where(slot == k, m, out)
        hit = lists[0] == m
        for lvl in range(count - 1 - k):
            lists[lvl] = jnp.where(hit, lists[lvl + 1], lists[lvl])
    return out


def _top_pair_sums(v1, v2, count):
    cols = v1.shape[1]
    row = lax.broadcasted_iota(jnp.int32, (8, cols), 0)
    lists = [jnp.where(row < count // (k + 1), v1[0:8] + v2[k:k + 1], NEG_INF) for k in range(count)]
    tail = v1[8:count] + v2[0:1]
    slot = lax.broadcasted_iota(jnp.int32, (count, cols), 0)
    out = jnp.zeros((count, cols), F32)
    for n in range(count):
        m = jnp.max(jnp.maximum(lists[0], tail), axis=0, keepdims=True)
        out = jnp.where(slot == n, m, out)
        hit = lists[0] == m
        for lvl in range(count - 1 - n):
            lists[lvl] = jnp.where(hit, lists[lvl + 1], lists[lvl])
        tail = jnp.where(tail == m, NEG_INF, tail)
    return out


def _route_kernel(h2_ref, wq_ref, keys_ref, h2t_ref, rank_ref, p2_ref, cnt_ref, p1_ref, qp_ref):
    h2 = h2_ref[...]
    h2t_ref[...] = pltpu.bitcast(h2.astype(F32).T.astype(BF16), jnp.uint32)
    qp_ref[...] = jnp.dot(h2, wq_ref[...], preferred_element_type=F32).astype(BF16)
    nt = (((1,), (1,)), ((), ()))

    def head(h, carry):
        col = pl.multiple_of(h * PEER_QDIM, PEER_QDIM)
        s1 = lax.dot_general(keys_ref[h, 0], qp_ref[:, pl.ds(col, PEER_HALF)], nt,
                             preferred_element_type=F32)
        s2 = lax.dot_general(keys_ref[h, 1], qp_ref[:, pl.ds(col + PEER_HALF, PEER_HALF)], nt,
                             preferred_element_type=F32)
        v1 = _top_desc_sorted(s1, PEER_TOPK)
        v2 = _top_desc_sorted(s2, PEER_TOPK)
        best = _top_pair_sums(v1, v2, PEER_TOPK)
        tau = best[PEER_TOPK - 1:PEER_TOPK]
        z = jnp.sum(jnp.exp(best - best[0:1]), axis=0, keepdims=True)
        rank = jnp.zeros_like(s2)
        cnt_top = jnp.zeros_like(v1)
        for k in range(PEER_TOPK):
            vk = v2[k:k + 1]
            rank = jnp.where(vk > s2, k + 1.0, rank)
            cnt_top = jnp.where(v1 + vk >= tau, k + 1.0, cnt_top)
        cnt = jnp.zeros_like(s1)
        for j in range(PEER_TOPK):
            cnt = jnp.where(s1 == v1[j:j + 1], cnt_top[j:j + 1], cnt)
        steps = N_EXPERTS // PEER_EXPERTS
        cnt_ref[:, h] = cnt.reshape(steps, N_KEYS // steps, ROUTE_TILE)
        p1_ref[:, h] = jnp.exp(s1 - v1[0:1]).reshape(steps, N_KEYS // steps, ROUTE_TILE)
        rank_ref[h] = pltpu.bitcast(rank.astype(BF16), jnp.uint32)
        p2_ref[h] = pltpu.bitcast((jnp.exp(s2 - v2[0:1]) * (0.5 / z)).astype(BF16), jnp.uint32)
        return carry

    lax.fori_loop(0, PEER_HEADS, head, 0, unroll=4)


def _route(h2, wq, keys):
    t = h2.shape[0]
    tok4 = pl.BlockSpec((PEER_HEADS, N_KEYS // 2, ROUTE_TILE), lambda i: (0, 0, i))
    steps = N_EXPERTS // PEER_EXPERTS
    tok3 = pl.BlockSpec((steps, PEER_HEADS, N_KEYS // steps, ROUTE_TILE), lambda i: (0, 0, 0, i))
    return pl.pallas_call(
        _route_kernel,
        grid=(t // ROUTE_TILE,),
        in_specs=[pl.BlockSpec((ROUTE_TILE, D_MODEL), lambda i: (i, 0)),
                  _full((D_MODEL, PEER_HEADS * PEER_QDIM)),
                  _full((PEER_HEADS, 2, N_KEYS, PEER_HALF))],
        out_specs=[pl.BlockSpec((D_MODEL // 2, ROUTE_TILE), lambda i: (0, i)), tok4, tok4, tok3, tok3],
        out_shape=[jax.ShapeDtypeStruct((D_MODEL // 2, t), jnp.uint32),
                   jax.ShapeDtypeStruct((PEER_HEADS, N_KEYS // 2, t), jnp.uint32),
                   jax.ShapeDtypeStruct((PEER_HEADS, N_KEYS // 2, t), jnp.uint32),
                   jax.ShapeDtypeStruct((steps, PEER_HEADS, N_KEYS // steps, t), F32),
                   jax.ShapeDtypeStruct((steps, PEER_HEADS, N_KEYS // steps, t), F32)],
        scratch_shapes=[pltpu.VMEM((ROUTE_TILE, PEER_HEADS * PEER_QDIM), BF16)],
        compiler_params=_params("arbitrary"),
        name="peer_route",
    )(h2, wq, keys)


def _pack_kernel(transpose, x_ref, o_ref):
    x = x_ref[...]
    if transpose:
        x = x.T
    o_ref[...] = pltpu.bitcast(x.astype(BF16), jnp.uint32)


def _pack_table(table, transpose):
    rows, cols = table.shape
    blk = 2048
    if transpose:
        out_spec = pl.BlockSpec((cols // 2, blk), lambda i: (0, i))
        out_shape = jax.ShapeDtypeStruct((cols // 2, rows), jnp.uint32)
    else:
        out_spec = pl.BlockSpec((blk // 2, cols), lambda i: (i, 0))
        out_shape = jax.ShapeDtypeStruct((rows // 2, cols), jnp.uint32)
    return pl.pallas_call(
        functools.partial(_pack_kernel, transpose),
        grid=(rows // blk,),
        in_specs=[pl.BlockSpec((blk, cols), lambda i: (i, 0))],
        out_specs=out_spec,
        out_shape=out_shape,
        compiler_params=_params("arbitrary"),
        name="pack_table_t" if transpose else "pack_table",
    )(table)


def _gate_rows(cs, act_ref, wt_ref, rank_ref, p2_ref, cnt_ref, p1_ref):
    tiles = (N_KEYS // BF16_SUBLANES, BF16_SUBLANES, LANES)
    for lt in range(PEER_TOKENS // LANES):
        lanes = slice(lt * LANES, (lt + 1) * LANES)
        gates = [jnp.zeros(tiles, BF16) for _ in cs]
        for h in range(PEER_HEADS):
            rank = pltpu.bitcast(rank_ref[h, :, lanes], BF16).reshape(tiles)
            p2 = pltpu.bitcast(p2_ref[h, :, lanes], BF16).reshape(tiles)
            for n, c in enumerate(cs):
                cnt = jnp.broadcast_to(cnt_ref[h, c:c + 1, lanes], (BF16_SUBLANES, LANES)).astype(BF16)
                p1 = jnp.broadcast_to(p1_ref[h, c:c + 1, lanes], (BF16_SUBLANES, LANES)).astype(BF16)
                gates[n] = gates[n] + jnp.where(rank < cnt[None], p2, jnp.zeros((), BF16)) * p1[None]
        for n, c in enumerate(cs):
            rows = slice(c * N_KEYS, (c + 1) * N_KEYS)
            a = act_ref[rows, lanes]
            gl = a + a * lax.erf(a * (1.0 / math.sqrt(2.0)))
            wt_ref[rows, lanes] = gl.astype(BF16) * gates[n].reshape(N_KEYS, LANES)


def _expert_kernel(pairs_per_block, u_ref, vt_ref, h2t_ref, rank_a, p2_a, rank_b, p2_b, cnt_a, p1_a,
                   cnt_b, p1_b, x1_ref, mod_ref, y_ref, acc_ref, act_a, act_b, wt_a, wt_b):
    g = pl.program_id(0)

    @pl.when(g == 0)
    def _():
        acc_ref[...] = jnp.zeros_like(acc_ref)
        act_b[...] = jnp.zeros_like(act_b)
        wt_a[...] = jnp.zeros_like(wt_a)

    key_rows = PEER_EXPERTS // N_KEYS

    def half_step(half, wt_old, act_old, wt_new, act_new, rank_ref, p2_ref, cnt_ref, p1_ref):
        gate_rows = key_rows // 2
        for idx in range(2):
            @pl.when(g >= -(half * 2 + idx))
            def _(idx=idx):
                for c in range(idx * gate_rows, (idx + 1) * gate_rows, GATE_ROWS):
                    _gate_rows(range(c, c + GATE_ROWS), act_old, wt_new, rank_ref, p2_ref, cnt_ref, p1_ref)
                if idx == 0:
                    vt = pltpu.bitcast(vt_ref[:, half * PEER_EXPERTS:(half + 1) * PEER_EXPERTS], BF16)
                    acc_ref[...] += jnp.dot(vt, wt_old[...], preferred_element_type=F32)
                else:
                    lo = half * PEER_EXPERTS // 2
                    u = pltpu.bitcast(u_ref[lo:lo + PEER_EXPERTS // 2, :], BF16)
                    act_new[...] = jnp.dot(u, pltpu.bitcast(h2t_ref[...], BF16), preferred_element_type=F32)

    half_step(0, wt_a, act_b, wt_b, act_a, rank_a, p2_a, cnt_a, p1_a)
    half_step(1, wt_b, act_a, wt_a, act_b, rank_b, p2_b, cnt_b, p1_b)

    @pl.when((g > 0) & (g % pairs_per_block == 0))
    def _():
        g2 = mod_ref[5:6, :]
        y_ref[...] = x1_ref[...] + g2 * acc_ref[...].T
        acc_ref[...] = jnp.zeros_like(acc_ref)


def _experts(u_bf, vt_bf, h2t, rank, p2, cnt, p1, x1, mod3, mod_row):
    t = x1.shape[0]
    ppb = N_EXPERTS // (2 * PEER_EXPERTS)
    steps = (t // PEER_TOKENS) * ppb + 1
    cur = lambda g: jnp.minimum(g, steps - 2)
    prv = lambda g: jnp.maximum(g - 1, 0)
    packed = lambda f: pl.BlockSpec((PEER_HEADS, N_KEYS // 2, PEER_TOKENS), lambda g: (0, 0, f(g) // ppb))
    rows = lambda f, half: pl.BlockSpec((None, PEER_HEADS, PEER_EXPERTS // N_KEYS, PEER_TOKENS),
                                        lambda g: (2 * (f(g) % ppb) + half, 0, 0, f(g) // ppb))
    return pl.pallas_call(
        functools.partial(_expert_kernel, ppb),
        grid=(steps,),
        in_specs=[pl.BlockSpec((PEER_EXPERTS, D_MODEL), lambda g: (cur(g) % ppb, 0)),
                  pl.BlockSpec((D_MODEL // 2, 2 * PEER_EXPERTS), lambda g: (0, prv(g) % ppb)),
                  pl.BlockSpec((D_MODEL // 2, PEER_TOKENS), lambda g: (0, cur(g) // ppb)),
                  packed(prv), packed(prv), packed(cur), packed(cur),
                  rows(prv, 1), rows(prv, 1), rows(cur, 0), rows(cur, 0),
                  pl.BlockSpec((PEER_TOKENS, D_MODEL), lambda g: (prv(g) // ppb, 0)),
                  pl.BlockSpec((None, 6, D_MODEL), lambda g: (mod_row(prv(g) // ppb), 0, 0))],
        out_specs=pl.BlockSpec((PEER_TOKENS, D_MODEL), lambda g: (prv(g) // ppb, 0)),
        out_shape=jax.ShapeDtypeStruct((t, D_MODEL), F32),
        scratch_shapes=[pltpu.VMEM((D_MODEL, PEER_TOKENS), F32),
                        pltpu.VMEM((PEER_EXPERTS, PEER_TOKENS), F32),
                        pltpu.VMEM((PEER_EXPERTS, PEER_TOKENS), F32),
                        pltpu.VMEM((PEER_EXPERTS, PEER_TOKENS), BF16),
                        pltpu.VMEM((PEER_EXPERTS, PEER_TOKENS), BF16)],
        compiler_params=_params("arbitrary"),
        name="peer_experts",
    )(u_bf, vt_bf, h2t, rank, p2, rank, p2, cnt, p1, cnt, p1, x1, mod3)


def _rope_tables(seq_len):
    rows = seq_len // GRID_W
    row = jnp.repeat(jnp.arange(rows, dtype=F32), GRID_W)
    col = jnp.tile(jnp.arange(GRID_W, dtype=F32), rows)
    inv = ROPE_THETA ** (-jnp.arange(AXIS_FREQS, dtype=F32) / AXIS_FREQS)
    ang_row = row[:, None] * inv
    ang_col = col[:, None] * inv
    zeros = jnp.zeros_like(ang_row)
    cos = jnp.concatenate([jnp.cos(ang_row)] * 2 + [jnp.cos(ang_col)] * 2, axis=-1)
    sin_lo = jnp.concatenate([-jnp.sin(ang_row), zeros, -jnp.sin(ang_col), zeros], axis=-1)
    sin_hi = jnp.concatenate([zeros, jnp.sin(ang_row), zeros, jnp.sin(ang_col)], axis=-1)
    return tuple(jnp.tile(t, (1, N_HEADS)) for t in (cos, sin_lo, sin_hi))


def _head_mean_matrix(width):
    idx = jnp.arange(width) // HEAD_DIM
    return jnp.where(idx[:, None] == idx[None, :], 1.0 / HEAD_DIM, 0.0).astype(BF16)


def _trunk(x3, mod3, mod_of_seq, cache, w, rope):
    b, seq, _ = x3.shape
    t = b * seq
    tiles_per_seq = seq // ROW_TILE
    q, k, v, xg, *cache_out = _inproj(x3.reshape(t, D_MODEL), mod3, mod_of_seq, seq, w["norm1"], w["w_in"],
                                      w["qn"], w["kn"], w["gq"], w["gk"], rope)
    conv = _conv_module(xg, tiles_per_seq, w["conv_dw"], w["conv_dw_b"], w["conv_ln_g"], w["conv_ln_b"],
                        w["conv_pw"])
    keys = k.reshape(b, seq, KV_WIDTH) if cache is not None else cache_out[0]
    x1, h2 = _mix(x3, mod3, mod_of_seq, q.reshape(b, seq, ATTN_WIDTH), keys,
                  v.reshape(b, seq, KV_WIDTH), conv.reshape(b, seq, CONV_CH), cache, w["sink"],
                  w["w_out"], w["norm2"])
    h2t, rank, p2, cnt, p1 = _route(h2.reshape(t, D_MODEL), w["peer_wq"], w["peer_keys"])
    per_seq = seq // PEER_TOKENS
    if per_seq == 0:
        seqs_per_block = PEER_TOKENS // seq
        row_of_block = lambda i: mod_of_seq(i * seqs_per_block)
    else:
        row_of_block = lambda i: mod_of_seq(i // per_seq)
    y = _experts(w["peer_u"], w["peer_vt"], h2t, rank, p2, cnt, p1, x1.reshape(t, D_MODEL), mod3, row_of_block)
    return y.reshape(b, seq, D_MODEL), cache_out


def kernel(x_prompt, x_sample, cache_k, cache_v, c, c_ctx, norm1, norm2, w_ada, b_ada, w_in, w_out,
           q_norm, k_norm, attn_sink, conv_dw, conv_dw_b, conv_ln_g, conv_ln_b, conv_pw,
           peer_wq, peer_keys, peer_u, peer_v):
    depth = norm1.shape[0]
    batch, seq, _ = x_prompt.shape
    dec_batch, dec_seq, _ = x_sample.shape
    past = cache_k.shape[2]
    assert dec_batch + 1 <= 8 and depth >= 1
    cond8 = jnp.concatenate([c_ctx[None, :], c, jnp.zeros((8 - 1 - dec_batch, D_MODEL), F32)], axis=0)
    rope = _rope_tables(dec_seq)
    xp, xs = x_prompt, x_sample
    new_ks, new_vs = [], []
    for l in range(depth):
        w = dict(
            norm1=norm1[l][None], norm2=norm2[l][None],
            w_in=w_in[l].astype(BF16), w_out=w_out[l].astype(BF16),
            qn=jnp.tile(q_norm[l], N_HEADS)[None], kn=jnp.tile(k_norm[l], N_KV_HEADS)[None],
            gq=_head_mean_matrix(ATTN_WIDTH), gk=_head_mean_matrix(KV_WIDTH),
            sink=attn_sink[l],
            conv_dw=conv_dw[l], conv_dw_b=conv_dw_b[l][None], conv_ln_g=conv_ln_g[l][None],
            conv_ln_b=conv_ln_b[l][None], conv_pw=conv_pw[l].astype(BF16),
            peer_wq=peer_wq[l].astype(BF16), peer_keys=peer_keys[l].astype(BF16),
            peer_u=_pack_table(peer_u[l], False), peer_vt=_pack_table(peer_v[l], True),
        )
        mod3 = _modulation(cond8, w_ada[l], b_ada[l]).reshape(8, 6, D_MODEL)
        xp, (kt_c, vt_c) = _trunk(xp, mod3, lambda s: 0, None, w, None)
        new_ks.append(kt_c.reshape(batch, N_KV_HEADS, HEAD_DIM, seq).transpose(0, 3, 1, 2))
        new_vs.append(vt_c.reshape(batch, N_KV_HEADS, HEAD_DIM, seq).transpose(0, 3, 1, 2))
        cache = (cache_k[:, l].reshape(dec_batch, past, KV_WIDTH), cache_v[:, l].reshape(dec_batch, past, KV_WIDTH))
        xs, _ = _trunk(xs, mod3, lambda s: s + 1, cache, w, rope)
    join = (lambda parts: parts[0][:, None]) if depth == 1 else (lambda parts: jnp.stack(parts, axis=1))
    return (xp, xs, join(new_ks), join(new_vs))
```
